```python
import math
import jax
import jax.numpy as jnp
from jax import lax
import numpy as np

D_MODEL = 1024
BATCH = 8
SEQ = 4096
DEPTH = 2

GRID_W = 64
CTX_LEN = 256
N_MIXERS = 2
N_ATTN_LAYERS = (DEPTH + 1) // 2
N_GLA_LAYERS = DEPTH // 2

ALPHA = (2.0 * DEPTH) ** 0.25
BETA = (8.0 * DEPTH) ** -0.25
LN_EPS = 1e-5

DIFF_HEADS = 8
DIFF_HEAD_DIM = D_MODEL // DIFF_HEADS // 2
Q_BLOCK = 128
ROPE_BASE = 10000.0
ROT_FREQS = DIFF_HEAD_DIM // 4

GLA_HEADS = 4
GLA_DK = D_MODEL // 2 // GLA_HEADS
GLA_DV = D_MODEL // GLA_HEADS
GLA_GATE_RANK = 16
GLA_TAU = 16.0
GLA_CHUNK = 64

N_EXPERTS = 16
N_GROUPS = 4
EXPERTS_PER_GROUP = N_EXPERTS // N_GROUPS
TOP_K = 2
D_EXPERT = D_MODEL // 2
MOE_BLOCK = 128

kernel_name = 'hybrid_diffattn_gla_grouped_moe_prefix'


def layer_norm(x, g, b):
    xf = x.astype(jnp.float32)
    mu = jnp.mean(xf, axis=-1, keepdims=True)
    var = jnp.mean(jnp.square(xf - mu), axis=-1, keepdims=True)
    y = (xf - mu) * lax.rsqrt(var + LN_EPS) * g.astype(jnp.float32) + b.astype(jnp.float32)
    return y.astype(x.dtype)


def rms_norm(x, g):
    xf = x.astype(jnp.float32)
    y = xf * lax.rsqrt(jnp.mean(jnp.square(xf), axis=-1, keepdims=True) + LN_EPS) * g.astype(jnp.float32)
    return y.astype(x.dtype)


def modulate(h, shift, scale):
    return h * (1.0 + scale) + shift


def rope_2d_tables(rows, dtype):
    row = jnp.repeat(jnp.arange(rows, dtype=jnp.float32), GRID_W)
    col = jnp.tile(jnp.arange(GRID_W, dtype=jnp.float32), rows)
    inv_freq = ROPE_BASE ** (-jnp.arange(ROT_FREQS, dtype=jnp.float32) / ROT_FREQS)
    ang_r = row[:, None] * inv_freq
    ang_c = col[:, None] * inv_freq
    ang = jnp.concatenate([ang_r, ang_r, ang_c, ang_c], axis=-1)
    return jnp.cos(ang).astype(dtype), jnp.sin(ang).astype(dtype)


def apply_rope_2d(x, cos, sin):
    xr = x.reshape(x.shape[:-1] + (2, 2, ROT_FREQS))
    rot = jnp.concatenate([-xr[..., 1:, :], xr[..., :1, :]], axis=-2).reshape(x.shape)
    return x * cos[None, :, None, None, :] + rot * sin[None, :, None, None, :]


def diff_attention(h_lat, h_ctx, w_qkv, w_o, lam_vecs, subln_g, lambda_init, cos, sin, need_ctx):
    B, S, _ = h_lat.shape

    def project(h):
        T = h.shape[1]
        q, k, v = jnp.split(h @ w_qkv, 3, axis=-1)
        q = q.reshape(B, T, DIFF_HEADS, 2, DIFF_HEAD_DIM)
        k = k.reshape(B, T, DIFF_HEADS, 2, DIFF_HEAD_DIM)
        v = v.reshape(B, T, DIFF_HEADS, 2 * DIFF_HEAD_DIM)
        return q, k, v

    q_l, k_l, v_l = project(h_lat)
    q_l = apply_rope_2d(q_l, cos, sin)
    k_l = apply_rope_2d(k_l, cos, sin)
    q_c, k_c, v_c = project(h_ctx)

    lf = lam_vecs.astype(jnp.float32)
    lam = jnp.exp(jnp.sum(lf[0] * lf[1])) - jnp.exp(jnp.sum(lf[2] * lf[3])) + lambda_init
    scale = DIFF_HEAD_DIM ** -0.5

    def diff_softmax(q, k, v):
        s = jnp.einsum('bqhmd,bkhmd->bhmqk', q, k, preferred_element_type=jnp.float32) * scale
        p = jax.nn.softmax(s, axis=-1)
        a = p[:, :, 0] - lam * p[:, :, 1]
        return jnp.einsum('bhqk,bkhe->bqhe', a.astype(v.dtype), v)

    k_all = jnp.concatenate([k_c, k_l], axis=1)
    v_all = jnp.concatenate([v_c, v_l], axis=1)
    n_blocks = S // Q_BLOCK
    q_blocks = q_l.reshape(B, n_blocks, Q_BLOCK, DIFF_HEADS, 2, DIFF_HEAD_DIM).swapaxes(0, 1)
    o_l = lax.map(lambda qb: diff_softmax(qb, k_all, v_all), q_blocks)
    o_l = o_l.swapaxes(0, 1).reshape(B, S, DIFF_HEADS, 2 * DIFF_HEAD_DIM)

    def finish(o):
        o = rms_norm(o, subln_g) * (1.0 - lambda_init)
        return o.reshape(o.shape[0], o.shape[1], DIFF_HEADS * 2 * DIFF_HEAD_DIM) @ w_o

    y_lat = finish(o_l)
    y_ctx = finish(diff_softmax(q_c, k_c, v_c)) if need_ctx else None
    return y_lat, y_ctx


def gla_chunked(q, k, v, g, h0):
    B, H, T, DK = q.shape
    DV = v.shape[-1]
    n_chunks = T // GLA_CHUNK

    def to_chunks(a):
        return a.astype(jnp.float32).reshape(B, H, n_chunks, GLA_CHUNK, a.shape[-1]).transpose(2, 0, 1, 3, 4)

    lower_tri = jnp.tril(jnp.ones((GLA_CHUNK, GLA_CHUNK), dtype=bool))[None, None, :, :, None]

    def chunk_step(state, inp):
        qb, kb, vb, gb = inp
        b = jnp.cumsum(gb, axis=2)
        rel = jnp.where(lower_tri, b[:, :, :, None, :] - b[:, :, None, :, :], -jnp.inf)
        scores = jnp.einsum('bhtd,bhsd,bhtsd->bhts', qb, kb, jnp.exp(rel))
        o = jnp.einsum('bhts,bhsv->bhtv', scores, vb) + jnp.einsum('bhtd,bhdv->bhtv', qb * jnp.exp(b), state)
        b_last = b[:, :, -1:, :]
        state = jnp.exp(b_last[:, :, 0, :, None]) * state + jnp.einsum('bhsd,bhsv->bhdv', kb * jnp.exp(b_last - b), vb)
        return state, o

    state, o = lax.scan(chunk_step, h0, (to_chunks(q), to_chunks(k), to_chunks(v), to_chunks(g)))
    o = o.transpose(1, 2, 0, 3, 4).reshape(B, H, T, DV)
    return o.astype(v.dtype), state


def gla_mixer(h_lat, h_ctx, w_in, wg_down, wg_up, bg, norm_g, w_o, need_ctx):
    B = h_lat.shape[0]
    dk_tot = GLA_HEADS * GLA_DK
    dv_tot = GLA_HEADS * GLA_DV

    def heads(a, dh):
        return a.reshape(B, a.shape[1], GLA_HEADS, dh).transpose(0, 2, 1, 3)

    def project(h):
        q, k, v, r = jnp.split(h @ w_in, [dk_tot, 2 * dk_tot, 2 * dk_tot + dv_tot], axis=-1)
        gates = [heads(jax.nn.log_sigmoid(((h @ wg_down[d]) @ wg_up[d] + bg[d]).astype(jnp.float32)) / GLA_TAU, GLA_DK)
                 for d in range(2)]
        return heads(q, GLA_DK) * (GLA_DK ** -0.5), heads(k, GLA_DK), heads(v, GLA_DV), r, gates[0], gates[1]

    q_l, k_l, v_l, r_l, gf_l, gb_l = project(h_lat)
    q_c, k_c, v_c, r_c, gf_c, gb_c = project(h_ctx)
    h0 = jnp.zeros((B, GLA_HEADS, GLA_DK, GLA_DV), jnp.float32)

    def flip(a):
        return jnp.flip(a, axis=2)

    o_cf, s_f = gla_chunked(q_c, k_c, v_c, gf_c, h0)
    o_cb, s_b = gla_chunked(flip(q_c), flip(k_c), flip(v_c), flip(gb_c), h0)
    o_lf, _ = gla_chunked(q_l, k_l, v_l, gf_l, s_f)
    o_lb, _ = gla_chunked(flip(q_l), flip(k_l), flip(v_l), flip(gb_l), s_b)

    def finish(o, r):
        T = o.shape[2]
        o = rms_norm(o.transpose(0, 2, 1, 3), norm_g).reshape(B, T, dv_tot)
        return (o * jax.nn.silu(r)) @ w_o

    y_lat = finish(o_lf + flip(o_lb), r_l)
    y_ctx = finish(o_cf + flip(o_cb), r_c) if need_ctx else None
    return y_lat, y_ctx


def grouped_top2_moe(h, router_w, router_b, w_gate, w_up, w_down):
    T, D = h.shape
    logits = jnp.einsum('td,de->te', h, router_w, preferred_element_type=jnp.float32)
    affinity = jax.nn.sigmoid(logits)
    select = (affinity + router_b.astype(jnp.float32)).reshape(T, N_GROUPS, EXPERTS_PER_GROUP)
    group_score = jnp.sum(lax.top_k(select, TOP_K)[0], axis=-1)
    group = jnp.argmax(group_score, axis=-1)
    in_group = select[jnp.arange(T), group]
    _, local = lax.top_k(in_group, TOP_K)
    expert = group[:, None] * EXPERTS_PER_GROUP + local
    weight = jnp.take_along_axis(affinity, expert, axis=-1)
    weight = weight / jnp.sum(weight, axis=-1, keepdims=True)

    n_assign = T * TOP_K
    flat_e = expert.reshape(-1)
    flat_w = weight.reshape(-1)
    flat_t = jnp.repeat(jnp.arange(T, dtype=jnp.int32), TOP_K)
    order = jnp.argsort(flat_e)
    se, st, sw = flat_e[order], flat_t[order], flat_w[order]
    counts = jnp.bincount(flat_e, length=N_EXPERTS)
    padded = (counts + MOE_BLOCK - 1) // MOE_BLOCK * MOE_BLOCK
    seg_start = jnp.cumsum(counts) - counts
    pad_end = jnp.cumsum(padded)
    pad_start = pad_end - padded
    dest = pad_start[se] + jnp.arange(n_assign) - seg_start[se]
    n_pad = n_assign + N_EXPERTS * MOE_BLOCK
    n_blocks = n_pad // MOE_BLOCK
    x_pad = jnp.zeros((n_pad, D), h.dtype).at[dest].set(h[st])
    tok_pad = jnp.full((n_pad,), T, jnp.int32).at[dest].set(st)
    w_pad = jnp.zeros((n_pad,), jnp.float32).at[dest].set(sw)
    block_expert = jnp.minimum(jnp.searchsorted(pad_end, jnp.arange(n_blocks) * MOE_BLOCK, side='right'), N_EXPERTS - 1)

    def expert_block(args):
        xb, e = args
        return (jax.nn.silu(xb @ w_gate[e]) * (xb @ w_up[e])) @ w_down[e]

    y_pad = lax.map(expert_block, (x_pad.reshape(n_blocks, MOE_BLOCK, D), block_expert)).reshape(n_pad, D)
    out = jnp.zeros((T + 1, D), h.dtype).at[tok_pad].add(y_pad * w_pad[:, None].astype(h.dtype))
    return out[:T]


def setup_inputs(seed: int = 0) -> dict:
    key = jax.random.key(seed)
    ks = jax.random.split(key, 24)

    def nrm(k, shape, scale):
        return jax.random.normal(k, shape, jnp.float32) * scale

    dk_tot = GLA_HEADS * GLA_DK
    dv_tot = GLA_HEADS * GLA_DV
    return {
        'x': nrm(ks[0], (BATCH, SEQ, D_MODEL), 1.0),
        'c': nrm(ks[1], (BATCH, D_MODEL), 1.0),
        'ctx': nrm(ks[2], (BATCH, CTX_LEN, D_MODEL), 1.0),
        'c_ctx': nrm(ks[3], (D_MODEL,), 1.0),
        'ada_w': nrm(ks[4], (DEPTH, D_MODEL, 6 * D_MODEL), D_MODEL ** -0.5),
        'ada_b': nrm(ks[5], (DEPTH, 6 * D_MODEL), 0.02),
        'ln_g': 1.0 + nrm(ks[6], (DEPTH, 2, D_MODEL), 0.02),
        'ln_b': nrm(ks[7], (DEPTH, 2, D_MODEL), 0.02),
        'attn_w_qkv': nrm(ks[8], (N_ATTN_LAYERS, D_MODEL, 3 * D_MODEL), D_MODEL ** -0.5),
        'attn_w_o': nrm(ks[9], (N_ATTN_LAYERS, D_MODEL, D_MODEL), BETA * D_MODEL ** -0.5),
        'attn_lambda': nrm(ks[10], (N_ATTN_LAYERS, 4, DIFF_HEAD_DIM), 0.1),
        'attn_subln_g': 1.0 + nrm(ks[11], (N_ATTN_LAYERS, 2 * DIFF_HEAD_DIM), 0.02),
        'gla_w_in': nrm(ks[12], (N_GLA_LAYERS, D_MODEL, 2 * dk_tot + 2 * dv_tot), D_MODEL ** -0.5),
        'gla_wg_down': nrm(ks[13], (N_GLA_LAYERS, 2, D_MODEL, GLA_GATE_RANK), D_MODEL ** -0.5),
        'gla_wg_up': nrm(ks[14], (N_GLA_LAYERS, 2, GLA_GATE_RANK, dk_tot), GLA_GATE_RANK ** -0.5),
        'gla_bg': nrm(ks[15], (N_GLA_LAYERS, 2, dk_tot), 0.1),
        'gla_norm_g': 1.0 + nrm(ks[16], (N_GLA_LAYERS, GLA_DV), 0.02),
        'gla_w_o': nrm(ks[17], (N_GLA_LAYERS, dv_tot, D_MODEL), BETA * dv_tot ** -0.5),
        'router_w': nrm(ks[18], (D_MODEL, N_EXPERTS), D_MODEL ** -0.5),
        'router_b': nrm(ks[19], (N_EXPERTS,), 0.01),
        'moe_w_gate': nrm(ks[20], (DEPTH, N_EXPERTS, D_MODEL, D_EXPERT), D_MODEL ** -0.5),
        'moe_w_up': nrm(ks[21], (DEPTH, N_EXPERTS, D_MODEL, D_EXPERT), D_MODEL ** -0.5),
        'moe_w_down': nrm(ks[22], (DEPTH, N_EXPERTS, D_EXPERT, D_MODEL), BETA * D_EXPERT ** -0.5),
    }


def reference(x, c, ctx, c_ctx, ada_w, ada_b, ln_g, ln_b, attn_w_qkv, attn_w_o, attn_lambda, attn_subln_g,
              gla_w_in, gla_wg_down, gla_wg_up, gla_bg, gla_norm_g, gla_w_o, router_w, router_b,
              moe_w_gate, moe_w_up, moe_w_down):
    B, S, D = x.shape
    C = ctx.shape[1]
    ROWS = S // GRID_W
    cos, sin = rope_2d_tables(ROWS, x.dtype)
    silu_c = jax.nn.silu(c)
    silu_cc = jax.nn.silu(c_ctx)

    for i in range(DEPTH):
        need_ctx = i < DEPTH - 1
        mod = (silu_c @ ada_w[i] + ada_b[i])[:, None, :]
        mod_c = silu_cc @ ada_w[i] + ada_b[i]
        sh_m, sc_m, g_m, sh_f, sc_f, g_f = jnp.split(mod, 6, axis=-1)
        csh_m, csc_m, cg_m, csh_f, csc_f, cg_f = jnp.split(mod_c, 6, axis=-1)

        h_lat = modulate(x, sh_m, sc_m)
        h_ctx = modulate(ctx, csh_m, csc_m)
        j = i // N_MIXERS
        if i % N_MIXERS == 0:
            y_lat, y_ctx = diff_attention(h_lat, h_ctx, attn_w_qkv[j], attn_w_o[j], attn_lambda[j], attn_subln_g[j],
                                          0.8 - 0.6 * math.exp(-0.3 * i), cos, sin, need_ctx)
        else:
            y_lat, y_ctx = gla_mixer(h_lat, h_ctx, gla_w_in[j], gla_wg_down[j], gla_wg_up[j], gla_bg[j],
                                     gla_norm_g[j], gla_w_o[j], need_ctx)
        x = layer_norm(ALPHA * x + g_m * y_lat, ln_g[i, 0], ln_b[i, 0])
        if need_ctx:
            ctx = layer_norm(ALPHA * ctx + cg_m * y_ctx, ln_g[i, 0], ln_b[i, 0])

        tokens = modulate(x, sh_f, sc_f).reshape(B * S, D)
        if need_ctx:
            tokens = jnp.concatenate([tokens, modulate(ctx, csh_f, csc_f).reshape(B * C, D)], axis=0)
        y = grouped_top2_moe(tokens, router_w, router_b, moe_w_gate[i], moe_w_up[i], moe_w_down[i])
        x = layer_norm(ALPHA * x + g_f * y[:B * S].reshape(B, S, D), ln_g[i, 1], ln_b[i, 1])
        if need_ctx:
            ctx = layer_norm(ALPHA * ctx + cg_f * y[B * S:].reshape(B, C, D), ln_g[i, 1], ln_b[i, 1])
    return x
```

```python
import functools
import math

import numpy as np
import jax
import jax.numpy as jnp
from jax import lax
from jax.experimental import pallas as pl
from jax.experimental.pallas import tpu as pltpu

F32 = jnp.float32
BF16 = jnp.bfloat16
I32 = jnp.int32

DEPTH = 2
ALPHA = (2.0 * DEPTH) ** 0.25
LN_EPS = 1e-5
GRID_W = 64
ROPE_BASE = 10000.0
DIFF_HEADS = 8
DIFF_HEAD_DIM = 64
ROT_FREQS = DIFF_HEAD_DIM // 4
GLA_HEADS = 4
GLA_DK = 128
GLA_DV = 256
GLA_GATE_RANK = 16
GLA_TAU = 16.0
GLA_CHUNK = 64
N_EXPERTS = 16
N_GROUPS = 4
EXPERTS_PER_GROUP = 4
LOG2E = 1.4426950408889634

LANES = 128
TM = 256
TK = 256
MOE_BLK = 256
ADA_TN = 1536
VMEM_LIMIT = 48 * 1024 * 1024
NEG_BIG = -1e30


def _cparams(sem):
    return pltpu.CompilerParams(dimension_semantics=sem, vmem_limit_bytes=VMEM_LIMIT)


def _silu(x):
    return x * (1.0 / (1.0 + jnp.exp(-x)))


def _layer_norm(z, g, b):
    mu = jnp.mean(z, axis=-1, keepdims=True)
    zc = z - mu
    var = jnp.mean(zc * zc, axis=-1, keepdims=True)
    return zc * lax.rsqrt(var + LN_EPS) * g + b


def _dot(a, b):
    return jnp.dot(a, b, preferred_element_type=F32)


def _dot_nt(a, b, precision=None):
    return lax.dot_general(a, b, (((1,), (1,)), ((), ())), preferred_element_type=F32, precision=precision)


def _dot_tn(a, b):
    return lax.dot_general(a, b, (((0,), (0,)), ((), ())), preferred_element_type=F32)


def _ada_kernel(c_ref, w_ref, b_ref, o_ref):
    s = _silu(c_ref[...])
    o_ref[0] = jnp.dot(s, w_ref[0], preferred_element_type=F32, precision=lax.Precision.HIGHEST) + b_ref[0]


def _ada_call(cond, ada_w, ada_b):
    depth, d, n = ada_w.shape
    rows = cond.shape[0]
    return pl.pallas_call(
        _ada_kernel,
        out_shape=jax.ShapeDtypeStruct((depth, rows, n), F32),
        grid=(depth, n // ADA_TN),
        in_specs=[
            pl.BlockSpec((rows, d), lambda i, j: (0, 0)),
            pl.BlockSpec((1, d, ADA_TN), lambda i, j: (i, 0, j)),
            pl.BlockSpec((1, 1, ADA_TN), lambda i, j: (i, 0, j)),
        ],
        out_specs=pl.BlockSpec((1, rows, ADA_TN), lambda i, j: (i, 0, j)),
        compiler_params=_cparams(("parallel", "parallel")),
        name="ada_mod",
    )(cond, ada_w, ada_b.reshape(depth, 1, n))


def _qkv_kernel(x_ref, mod_ref, wqk_ref, wvt_ref, cos_ref, sa_ref, sb_ref, q_ref, k_ref, vt_ref, *, d):
    mod = mod_ref[0]
    h = (x_ref[0] * (1.0 + mod[:, d:2 * d]) + mod[:, 0:d]).astype(BF16)
    qk = _dot(h, wqk_ref[...])
    vt_ref[0, 0] = _dot_nt(wvt_ref[...], h).astype(BF16)
    cos, sa, sb = cos_ref[...], sa_ref[...], sb_ref[...]
    q_scale = DIFF_HEAD_DIM ** -0.5 * LOG2E
    for s in range(2 * d // LANES):
        z = qk[:, s * LANES:(s + 1) * LANES]
        zr = z * cos + pltpu.roll(z, LANES - ROT_FREQS, axis=1) * sa + pltpu.roll(z, ROT_FREQS, axis=1) * sb
        if s < d // LANES:
            q_ref[0, :, s * LANES:(s + 1) * LANES] = (zr * q_scale).astype(BF16)
        else:
            k_ref[0, :, (s * LANES - d):(s * LANES - d + LANES)] = zr.astype(BF16)


def _rope_tables(s_len, c_len):
    rows = s_len // GRID_W
    row = np.repeat(np.arange(rows, dtype=np.float32), GRID_W)
    col = np.tile(np.arange(GRID_W, dtype=np.float32), rows)
    inv_freq = np.float32(ROPE_BASE) ** (-np.arange(ROT_FREQS, dtype=np.float32) / np.float32(ROT_FREQS))
    ang_r = row[:, None] * inv_freq
    ang_c = col[:, None] * inv_freq
    ang = np.concatenate([ang_r, ang_r, ang_c, ang_c], axis=-1).astype(np.float64)
    cos, sin = np.cos(ang), np.sin(ang)
    first_half = (np.arange(DIFF_HEAD_DIM) // ROT_FREQS) % 2 == 0
    sa = np.where(first_half[None, :], -sin, 0.0)
    sb = np.where(first_half[None, :], 0.0, sin)
    pad = lambda t, fill: np.concatenate([t, np.full((c_len, DIFF_HEAD_DIM), fill)], axis=0)
    two = lambda t: np.concatenate([t, t], axis=-1).astype(np.float32)
    return two(pad(cos, 1.0)), two(pad(sa, 0.0)), two(pad(sb, 0.0))


def _qkv_call(xs, mod, wqk, wvt, tables, n_batch, ctx_row):
    b, t, d = xs.shape
    nt = t // TM
    n_lat = nt - 1
    mod_idx = lambda bi, j: (jnp.where(j < n_lat, bi, ctx_row), 0, 0)
    return pl.pallas_call(
        functools.partial(_qkv_kernel, d=d),
        out_shape=(jax.ShapeDtypeStruct((b, t, d), BF16), jax.ShapeDtypeStruct((b, t, d), BF16),
                   jax.ShapeDtypeStruct((b, nt, d, TM), BF16)),
        grid=(b, nt),
        in_specs=[
            pl.BlockSpec((1, TM, d), lambda bi, j: (bi, j, 0)),
            pl.BlockSpec((1, 1, 6 * d), mod_idx),
            pl.BlockSpec((d, 2 * d), lambda bi, j: (0, 0)),
            pl.BlockSpec((d, d), lambda bi, j: (0, 0)),
            pl.BlockSpec((TM, LANES), lambda bi, j: (j, 0)),
            pl.BlockSpec((TM, LANES), lambda bi, j: (j, 0)),
            pl.BlockSpec((TM, LANES), lambda bi, j: (j, 0)),
        ],
        out_specs=(pl.BlockSpec((1, TM, d), lambda bi, j: (bi, j, 0)),
                   pl.BlockSpec((1, TM, d), lambda bi, j: (bi, j, 0)),
                   pl.BlockSpec((1, 1, d, TM), lambda bi, j: (bi, j, 0, 0))),
        compiler_params=_cparams(("parallel", "parallel")),
        name="qkv_rope",
    )(xs, mod, wqk, wvt, *tables)


def _attn_kernel(q_ref, k_ref, vt_ref, lam_ref, g_ref, o_ref, *, n_lat, lambda_init, tq):
    q = q_ref[0]
    lane = lax.broadcasted_iota(I32, q.shape, 1)
    zero = jnp.zeros_like(q)
    qq = jnp.concatenate([jnp.where(lane < DIFF_HEAD_DIM, q, zero),
                          jnp.where(lane >= DIFF_HEAD_DIM, q, zero)], axis=0)

    def body(c, carry):
        m, l, acc = carry
        start = pl.multiple_of(c * TK, TK)
        kc = k_ref[0, pl.ds(start, TK), :]
        st = _dot_nt(kc, qq)
        m_new = jnp.maximum(m, jnp.max(st, axis=0, keepdims=True))
        alpha = jnp.exp2(m - m_new)
        p = jnp.exp2(st - m_new)
        l = l * alpha + jnp.sum(p, axis=0, keepdims=True)
        acc = acc * alpha + _dot(vt_ref[0, c], p.astype(BF16))
        return m_new, l, acc

    first_chunk = jnp.where(pl.program_id(2) < n_lat, 0, n_lat)
    init = (jnp.full((1, 2 * tq), NEG_BIG, F32), jnp.zeros((1, 2 * tq), F32), jnp.zeros((LANES, 2 * tq), F32))
    _, l, acc = lax.fori_loop(first_chunk, n_lat + 1, body, init)
    acc = acc / l
    lv = lam_ref[...]
    lam = (jnp.exp(jnp.sum(lv[0:1] * lv[1:2], axis=1, keepdims=True))
           - jnp.exp(jnp.sum(lv[2:3] * lv[3:4], axis=1, keepdims=True)) + lambda_init)
    ot = acc[:, :tq] - lam * acc[:, tq:]
    ms = jnp.mean(ot * ot, axis=0, keepdims=True)
    ot = ot * lax.rsqrt(ms + LN_EPS) * g_ref[...] * (1.0 - lambda_init)
    o_ref[0] = ot.T.astype(BF16)


def _attn_call(q, k, vt, lam_vecs, subln_g, lambda_init):
    b, t, d = q.shape
    nt = t // TM
    return pl.pallas_call(
        functools.partial(_attn_kernel, n_lat=nt - 1, lambda_init=lambda_init, tq=TM),
        out_shape=jax.ShapeDtypeStruct((b, t, d), BF16),
        grid=(b, DIFF_HEADS, nt),
        in_specs=[
            pl.BlockSpec((1, TM, LANES), lambda bi, h, qi: (bi, qi, h)),
            pl.BlockSpec((1, t, LANES), lambda bi, h, qi: (bi, 0, h)),
            pl.BlockSpec((1, nt, LANES, TM), lambda bi, h, qi: (bi, 0, h, 0)),
            pl.BlockSpec((4, DIFF_HEAD_DIM), lambda bi, h, qi: (0, 0)),
            pl.BlockSpec((LANES, 1), lambda bi, h, qi: (0, 0)),
        ],
        out_specs=pl.BlockSpec((1, TM, LANES), lambda bi, h, qi: (bi, qi, h)),
        compiler_params=_cparams(("parallel", "parallel", "parallel")),
        name="diff_attn",
    )(q, k, vt, lam_vecs, subln_g.reshape(LANES, 1))


def _route(logits_t, rb, ri_ref, rw_ref, cnt_ref, first):
    tm = logits_t.shape[1]
    aff = 1.0 / (1.0 + jnp.exp(-logits_t))
    sel = aff + rb
    row = lambda a, e: a[e:e + 1, :]
    gscore = []
    for g in range(N_GROUPS):
        a, b, c, d = (row(sel, g * EXPERTS_PER_GROUP + j) for j in range(EXPERTS_PER_GROUP))
        hi1, lo1, hi2, lo2 = jnp.maximum(a, b), jnp.minimum(a, b), jnp.maximum(c, d), jnp.minimum(c, d)
        gscore.append(jnp.maximum(hi1, hi2) + jnp.maximum(jnp.minimum(hi1, hi2), jnp.maximum(lo1, lo2)))
    best, grp = gscore[0], jnp.zeros((1, tm), I32)
    for g in range(1, N_GROUPS):
        better = gscore[g] > best
        best = jnp.where(better, gscore[g], best)
        grp = jnp.where(better, g, grp)
    in_sel, in_aff = [], []
    for j in range(EXPERTS_PER_GROUP):
        s_j, a_j = row(sel, j), row(aff, j)
        for g in range(1, N_GROUPS):
            s_j = jnp.where(grp == g, row(sel, g * EXPERTS_PER_GROUP + j), s_j)
            a_j = jnp.where(grp == g, row(aff, g * EXPERTS_PER_GROUP + j), a_j)
        in_sel.append(s_j)
        in_aff.append(a_j)
    v0, i0, a0 = in_sel[0], jnp.zeros((1, tm), I32), in_aff[0]
    for j in range(1, EXPERTS_PER_GROUP):
        better = in_sel[j] > v0
        v0 = jnp.where(better, in_sel[j], v0)
        i0 = jnp.where(better, j, i0)
        a0 = jnp.where(better, in_aff[j], a0)
    v1, i1, a1 = jnp.full((1, tm), -jnp.inf, F32), jnp.zeros((1, tm), I32), jnp.zeros((1, tm), F32)
    for j in range(EXPERTS_PER_GROUP):
        better = (in_sel[j] > v1) & (i0 != j)
        v1 = jnp.where(better, in_sel[j], v1)
        i1 = jnp.where(better, j, i1)
        a1 = jnp.where(better, in_aff[j], a1)
    e0 = grp * EXPERTS_PER_GROUP + i0
    e1 = grp * EXPERTS_PER_GROUP + i1
    wsum = a0 + a1
    erow = lax.broadcasted_iota(I32, (N_EXPERTS, tm), 0)
    oh0 = (erow == e0).astype(F32)
    oh1 = (erow == e1).astype(F32)
    both = oh0 + oh1
    tri = (lax.broadcasted_iota(I32, (tm, tm), 0) < lax.broadcasted_iota(I32, (tm, tm), 1)).astype(BF16)
    before = _dot(both.astype(BF16), tri)

    @pl.when(first)
    def _():
        cnt_ref[...] = jnp.zeros_like(cnt_ref)

    base = cnt_ref[:, 0:1] + before
    r0 = jnp.sum(oh0 * base, axis=0, keepdims=True)
    r1 = jnp.sum(oh1 * base, axis=0, keepdims=True)
    cnt_ref[...] = cnt_ref[...] + jnp.sum(both, axis=1, keepdims=True)
    ri_ref[0, 0:1, :] = e0
    ri_ref[0, 1:2, :] = e1
    ri_ref[0, 2:3, :] = r0.astype(I32)
    ri_ref[0, 3:4, :] = r1.astype(I32)
    ri_ref[0, 4:8, :] = jnp.zeros((4, tm), I32)
    rw_ref[0, 0:1, :] = a0 / wsum
    rw_ref[0, 1:2, :] = a1 / wsum
    rw_ref[0, 2:8, :] = jnp.zeros((6, tm), F32)


def _post_mixer(y, x, mod, lng, lnb, rwt, rb, x1_ref, ri_ref, rw_ref, cnt_ref, first, d):
    x1 = _layer_norm(ALPHA * x + mod[:, 2 * d:3 * d] * y, lng, lnb)
    x1_ref[0] = x1
    tok = x1 * (1.0 + mod[:, 4 * d:5 * d]) + mod[:, 3 * d:4 * d]
    logits_t = _dot_nt(rwt, tok, precision=lax.Precision.HIGHEST)
    _route(logits_t, rb, ri_ref, rw_ref, cnt_ref, first)


def _attn_out_kernel(o_ref, x_ref, mod_ref, wo_ref, lng_ref, lnb_ref, rwt_ref, rb_ref,
                     x1_ref, ri_ref, rw_ref, cnt_ref, *, d):
    first = (pl.program_id(0) == 0) & (pl.program_id(1) == 0)
    y = _dot(o_ref[0], wo_ref[...])
    _post_mixer(y, x_ref[0], mod_ref[0], lng_ref[...], lnb_ref[...], rwt_ref[...], rb_ref[...],
                x1_ref, ri_ref, rw_ref, cnt_ref, first, d)


def _route_out_shapes(b, n_tiles, t_out, d):
    return (jax.ShapeDtypeStruct((b, t_out, d), F32),
            jax.ShapeDtypeStruct((b * n_tiles, 8, TM), I32),
            jax.ShapeDtypeStruct((b * n_tiles, 8, TM), F32),
            jax.ShapeDtypeStruct((N_EXPERTS, LANES), F32))


def _route_out_specs(n_tiles, d):
    return (pl.BlockSpec((1, TM, d), lambda bi, j: (bi, j, 0)),
            pl.BlockSpec((1, 8, TM), lambda bi, j: (bi * n_tiles + j, 0, 0)),
            pl.BlockSpec((1, 8, TM), lambda bi, j: (bi * n_tiles + j, 0, 0)),
            pl.BlockSpec((N_EXPERTS, LANES), lambda bi, j: (0, 0)))


def _attn_out_call(o, xs, mod, wo, lng, lnb, rwt, rb, ctx_row):
    b, t, d = xs.shape
    nt = t // TM
    n_lat = nt - 1
    row = lambda a: pl.BlockSpec(a.shape, lambda bi, j: (0,) * a.ndim)
    return pl.pallas_call(
        functools.partial(_attn_out_kernel, d=d),
        out_shape=_route_out_shapes(b, nt, t, d),
        grid=(b, nt),
        in_specs=[
            pl.BlockSpec((1, TM, d), lambda bi, j: (bi, j, 0)),
            pl.BlockSpec((1, TM, d), lambda bi, j: (bi, j, 0)),
            pl.BlockSpec((1, 1, 6 * d), lambda bi, j: (jnp.where(j < n_lat, bi, ctx_row), 0, 0)),
            row(wo), row(lng), row(lnb), row(rwt), row(rb),
        ],
        out_specs=_route_out_specs(nt, d),
        compiler_params=_cparams(("arbitrary", "arbitrary")),
        name="attn_out_route",
    )(o, xs, mod, wo, lng, lnb, rwt, rb)


def _row_copy(src, r_src, dst, r_dst, sem):
    return pltpu.make_async_copy(src.at[pl.ds(r_src, 1)], dst.at[pl.ds(r_dst, 1)], sem)


def _dispatch_kernel(pad_ref, cnt_ref, x_ref, mod_ref, ri_ref, xs_ref, tok_ref, zero_ref, ri_smem, sem, sem_s,
                     *, d, n_steps):
    step = pl.program_id(0) * pl.num_programs(1) + pl.program_id(1)
    idx_copy = pltpu.make_async_copy(ri_ref.at[0], ri_smem, sem_s)
    idx_copy.start()
    mod = mod_ref[0]
    tok_ref[...] = x_ref[0] * (1.0 + mod[:, 4 * d:5 * d]) + mod[:, 3 * d:4 * d]
    idx_copy.wait()

    def issue(r, carry):
        for k in range(2):
            dest = pad_ref[ri_smem[k, r]] + ri_smem[2 + k, r]
            _row_copy(tok_ref, r, xs_ref, dest, sem).start()
        return carry

    lax.fori_loop(0, TM, issue, 0)

    def drain(r, carry):
        for k in range(2):
            _row_copy(tok_ref, 0, xs_ref, 0, sem).wait()
        return carry

    lax.fori_loop(0, TM, drain, 0)

    @pl.when(step == n_steps - 1)
    def _():
        zero_ref[...] = jnp.zeros_like(zero_ref)
        for e in range(N_EXPERTS):
            lo = pad_ref[e] + cnt_ref[e]
            hi = pad_ref[e + 1]

            def fill(r, carry):
                _row_copy(zero_ref, 0, xs_ref, r, sem).start()
                return carry

            def fill_wait(r, carry):
                _row_copy(zero_ref, 0, xs_ref, 0, sem).wait()
                return carry

            lax.fori_loop(lo, hi, fill, 0)
            lax.fori_loop(lo, hi, fill_wait, 0)

        zrows = zero_ref.shape[0]

        def tail_copy(r):
            return pltpu.make_async_copy(zero_ref, xs_ref.at[pl.ds(pl.multiple_of(r * zrows, zrows), zrows)], sem)

        def tail(r, carry):
            tail_copy(r).start()
            return carry

        def tail_wait(r, carry):
            tail_copy(r).wait()
            return carry

        lo, hi = pad_ref[N_EXPERTS] // zrows, xs_ref.shape[0] // zrows
        lax.fori_loop(lo, hi, tail, 0)
        lax.fori_loop(lo, hi, tail_wait, 0)


def _dispatch_call(pad_start, counts, x1, mod, ri, n_tiles, n_pad, ctx_row, n_lat):
    b, t, d = x1.shape
    grid_spec = pltpu.PrefetchScalarGridSpec(
        num_scalar_prefetch=2,
        grid=(b, n_tiles),
        in_specs=[
            pl.BlockSpec((1, TM, d), lambda bi, j, *_: (bi, j, 0)),
            pl.BlockSpec((1, 1, 6 * d), lambda bi, j, *_: (jnp.where(j < n_lat, bi, ctx_row), 0, 0)),
            pl.BlockSpec((1, 8, TM), lambda bi, j, *_: (bi * n_tiles + j, 0, 0)),
        ],
        out_specs=pl.BlockSpec(memory_space=pl.ANY),
        scratch_shapes=[pltpu.VMEM((TM, d), F32), pltpu.VMEM((8, d), F32), pltpu.SMEM((8, TM), I32),
                        pltpu.SemaphoreType.DMA, pltpu.SemaphoreType.DMA],
    )
    return pl.pallas_call(
        functools.partial(_dispatch_kernel, d=d, n_steps=b * n_tiles),
        out_shape=jax.ShapeDtypeStruct((n_pad, d), F32),
        grid_spec=grid_spec,
        compiler_params=_cparams(("arbitrary", "arbitrary")),
        name="moe_dispatch",
    )(pad_start, counts, x1, mod, ri)


def _ffn_kernel(be_ref, nu_ref, xs_ref, wg_ref, wu_ref, wd_ref, ys_ref, wg_s, wu_s, wd_s):
    i = pl.program_id(0)

    @pl.when(i < nu_ref[0])
    def _():
        prev = be_ref[jnp.maximum(i - 1, 0)]

        @pl.when((i == 0) | (be_ref[i] != prev))
        def _():
            wg_s[...] = wg_ref[0].astype(BF16)
            wu_s[...] = wu_ref[0].astype(BF16)
            wd_s[...] = wd_ref[0].astype(BF16)

        x = xs_ref[...].astype(BF16)
        g = _dot(x, wg_s[...])
        u = _dot(x, wu_s[...])
        ys_ref[...] = _dot((_silu(g) * u).astype(BF16), wd_s[...])

    @pl.when(i >= nu_ref[0])
    def _():
        ys_ref[...] = jnp.zeros_like(ys_ref)


def _ffn_call(block_expert, n_used, xs, w_gate, w_up, w_down):
    n_pad, d = xs.shape
    de = w_gate.shape[-1]
    n_blocks = n_pad // MOE_BLK
    blk = lambda i, be, nu: (jnp.minimum(i, nu[0] - 1), 0)
    exp = lambda i, be, nu: (be[jnp.minimum(i, nu[0] - 1)], 0, 0)
    grid_spec = pltpu.PrefetchScalarGridSpec(
        num_scalar_prefetch=2,
        grid=(n_blocks,),
        in_specs=[
            pl.BlockSpec((MOE_BLK, d), blk),
            pl.BlockSpec((1, d, de), exp),
            pl.BlockSpec((1, d, de), exp),
            pl.BlockSpec((1, de, d), exp),
        ],
        out_specs=pl.BlockSpec((MOE_BLK, d), lambda i, be, nu: (i, 0)),
        scratch_shapes=[pltpu.VMEM((d, de), BF16), pltpu.VMEM((d, de), BF16), pltpu.VMEM((de, d), BF16)],
    )
    return pl.pallas_call(
        _ffn_kernel,
        out_shape=jax.ShapeDtypeStruct((n_pad, d), F32),
        grid_spec=grid_spec,
        compiler_params=_cparams(("arbitrary",)),
        name="moe_ffn",
    )(block_expert, n_used, xs, w_gate, w_up, w_down)


def _combine_kernel(pad_ref, x_ref, mod_ref, ri_ref, rw_ref, lng_ref, lnb_ref, ys_ref, o_ref,
                    y0_ref, y1_ref, ri_smem, sem, sem_s, *, d):
    idx_copy = pltpu.make_async_copy(ri_ref.at[0], ri_smem, sem_s)
    idx_copy.start()
    idx_copy.wait()
    bufs = (y0_ref, y1_ref)

    def issue(r, carry):
        for k in range(2):
            src = pad_ref[ri_smem[k, r]] + ri_smem[2 + k, r]
            _row_copy(ys_ref, src, bufs[k], r, sem).start()
        return carry

    lax.fori_loop(0, TM, issue, 0)

    def drain(r, carry):
        for k in range(2):
            _row_copy(ys_ref, 0, bufs[k], 0, sem).wait()
        return carry

    lax.fori_loop(0, TM, drain, 0)
    w = rw_ref[0]
    moe = w[:, 0:1] * y0_ref[...] + w[:, 1:2] * y1_ref[...]
    mod = mod_ref[0]
    o_ref[0] = _layer_norm(ALPHA * x_ref[0] + mod[:, 5 * d:6 * d] * moe, lng_ref[...], lnb_ref[...])


def _combine_call(pad_start, x1, mod, ri, rw_t, lng, lnb, ys, n_tiles, ctx_row, n_lat):
    b, _, d = x1.shape
    row = lambda a: pl.BlockSpec(a.shape, lambda bi, j, *_: (0,) * a.ndim)
    grid_spec = pltpu.PrefetchScalarGridSpec(
        num_scalar_prefetch=1,
        grid=(b, n_tiles),
        in_specs=[
            pl.BlockSpec((1, TM, d), lambda bi, j, *_: (bi, j, 0)),
            pl.BlockSpec((1, 1, 6 * d), lambda bi, j, *_: (jnp.where(j < n_lat, bi, ctx_row), 0, 0)),
            pl.BlockSpec((1, 8, TM), lambda bi, j, *_: (bi * n_tiles + j, 0, 0)),
            pl.BlockSpec((1, TM, 8), lambda bi, j, *_: (bi * n_tiles + j, 0, 0)),
            row(lng), row(lnb),
            pl.BlockSpec(memory_space=pl.ANY),
        ],
        out_specs=pl.BlockSpec((1, TM, d), lambda bi, j, *_: (bi, j, 0)),
        scratch_shapes=[pltpu.VMEM((TM, d), F32), pltpu.VMEM((TM, d), F32), pltpu.SMEM((8, TM), I32),
                        pltpu.SemaphoreType.DMA, pltpu.SemaphoreType.DMA],
    )
    return pl.pallas_call(
        functools.partial(_combine_kernel, d=d),
        out_shape=jax.ShapeDtypeStruct((b, n_tiles * TM, d), F32),
        grid_spec=grid_spec,
        compiler_params=_cparams(("arbitrary", "arbitrary")),
        name="moe_combine",
    )(pad_start, x1, mod, ri, rw_t, lng, lnb, ys)


def _moe(x1, mod, ri, rw, counts, lng, lnb, w_gate, w_up, w_down, n_tiles, ctx_row, n_lat):
    b, _, d = x1.shape
    n_assign = 2 * b * n_tiles * TM
    n_pad = n_assign + N_EXPERTS * MOE_BLK
    n_blocks = n_pad // MOE_BLK
    cnt = counts[:, 0].astype(I32)
    padded = (cnt + MOE_BLK - 1) // MOE_BLK * MOE_BLK
    pad_end = jnp.cumsum(padded)
    pad_start = jnp.concatenate([jnp.zeros((1,), I32), pad_end]).astype(I32)
    block_expert = jnp.minimum(
        jnp.searchsorted(pad_end, jnp.arange(n_blocks, dtype=I32) * MOE_BLK, side="right"), N_EXPERTS - 1).astype(I32)
    n_used = (pad_end[-1:] // MOE_BLK).astype(I32)
    xs = _dispatch_call(pad_start, cnt, x1, mod, ri, n_tiles, n_pad, ctx_row, n_lat)
    ys = _ffn_call(block_expert, n_used, xs, w_gate, w_up, w_down)
    rw_t = jnp.swapaxes(rw, 1, 2)
    return _combine_call(pad_start, x1, mod, ri, rw_t, lng, lnb, ys, n_tiles, ctx_row, n_lat)


def _gla_proj_kernel(x_ref, mod_ref, win_ref, wgd_ref, wgu_ref, bg_ref, q_ref, k_ref, v_ref, r_ref, g_ref, *, d):
    mod = mod_ref[0]
    h = (x_ref[0] * (1.0 + mod[:, d:2 * d]) + mod[:, 0:d]).astype(BF16)
    dk = GLA_HEADS * GLA_DK
    p = _dot(h, win_ref[...])
    q_ref[0] = (p[:, 0:dk] * GLA_DK ** -0.5).astype(BF16)
    k_ref[0] = p[:, dk:2 * dk].astype(BF16)
    v_ref[0] = p[:, 2 * dk:2 * dk + d].astype(BF16)
    r_ref[0] = p[:, 2 * dk + d:].astype(BF16)
    low = _dot(h, wgd_ref[...]).astype(BF16)
    pre = _dot(low, wgu_ref[...]) + bg_ref[...]
    g_ref[0] = (jnp.minimum(pre, 0.0) - jnp.log1p(jnp.exp(-jnp.abs(pre)))) * (1.0 / GLA_TAU)


def _gla_proj_call(xs, mod, win, wgd, wgu, bg, ctx_row):
    b, t, d = xs.shape
    nt = t // TM
    n_lat = nt - 1
    dk = GLA_HEADS * GLA_DK
    row = lambda a: pl.BlockSpec(a.shape, lambda bi, j: (0,) * a.ndim)
    tile = lambda n: pl.BlockSpec((1, TM, n), lambda bi, j: (bi, j, 0))
    return pl.pallas_call(
        functools.partial(_gla_proj_kernel, d=d),
        out_shape=(jax.ShapeDtypeStruct((b, t, dk), BF16), jax.ShapeDtypeStruct((b, t, dk), BF16),
                   jax.ShapeDtypeStruct((b, t, d), BF16), jax.ShapeDtypeStruct((b, t, d), BF16),
                   jax.ShapeDtypeStruct((b, t, 2 * dk), F32)),
        grid=(b, nt),
        in_specs=[tile(d), pl.BlockSpec((1, 1, 6 * d), lambda bi, j: (jnp.where(j < n_lat, bi, ctx_row), 0, 0)),
                  row(win), row(wgd), row(wgu), row(bg)],
        out_specs=(tile(dk), tile(dk), tile(d), tile(d), tile(2 * dk)),
        compiler_params=_cparams(("parallel", "parallel")),
        name="gla_proj",
    )(xs, mod, win, wgd, wgu, bg)


def _split3(x):
    hi = x.astype(BF16)
    r1 = x - hi.astype(F32)
    mid = r1.astype(BF16)
    lo = (r1 - mid.astype(F32)).astype(BF16)
    return hi, mid, lo


def _gla_direction(q_ref, k_ref, v_ref, g_ref, o_ref, state_ref, reverse):
    L = GLA_CHUNK
    ri = lax.broadcasted_iota(I32, (L, L), 0)
    ci = lax.broadcasted_iota(I32, (L, L), 1)
    keep = (ci >= ri) if reverse else (ci <= ri)
    tri = keep.astype(BF16)
    chunks = range(TM // L)
    for c in (reversed(chunks) if reverse else chunks):
        rows = slice(c * L, (c + 1) * L)
        g = g_ref[0, rows, :]
        b_all = sum(_dot(tri, part) for part in _split3(g))
        edge = 0 if reverse else L - 1
        for h in range(GLA_HEADS):
            kcols = slice(h * GLA_DK, (h + 1) * GLA_DK)
            vcols = slice(h * GLA_DV, (h + 1) * GLA_DV)
            bh = b_all[:, kcols]
            b_last = bh[edge:edge + 1, :]
            q = q_ref[0, rows, kcols].astype(F32)
            k = k_ref[0, rows, kcols].astype(F32)
            v = v_ref[0, rows, vcols]
            qd = (q * jnp.exp(bh)).astype(BF16)
            kd = (k * jnp.exp(-bh)).astype(BF16)
            a = jnp.where(keep, _dot_nt(qd, kd), 0.0).astype(BF16)
            state_t = state_ref[h]
            o_ref[0, rows, vcols] = _dot(a, v) + _dot_nt(qd, state_t.astype(BF16))
            kl = (k * jnp.exp(b_last - bh)).astype(BF16)
            state_ref[h] = jnp.exp(b_last) * state_t + _dot_tn(v, kl)


def _gla_scan_kernel(qf, kf, vf, gf, qb, kb, vb, gb, of_ref, ob_ref, sf_ref, sb_ref):
    @pl.when(pl.program_id(1) == 0)
    def _():
        sf_ref[...] = jnp.zeros_like(sf_ref)
        sb_ref[...] = jnp.zeros_like(sb_ref)

    _gla_direction(qf, kf, vf, gf, of_ref, sf_ref, False)
    _gla_direction(qb, kb, vb, gb, ob_ref, sb_ref, True)


def _gla_scan_call(q, k, v, g):
    b, t, d = v.shape
    nt = t // TM
    n_lat = nt - 1
    dk = GLA_HEADS * GLA_DK
    fwd = lambda j: jnp.where(j == 0, n_lat, j - 1)
    bwd = lambda j: jnp.where(j == 0, n_lat, n_lat - j)
    spec = lambda n, idx, col: pl.BlockSpec((1, TM, n), lambda bi, j: (bi, idx(j), col))
    return pl.pallas_call(
        _gla_scan_kernel,
        out_shape=(jax.ShapeDtypeStruct((b, n_lat * TM, d), F32), jax.ShapeDtypeStruct((b, n_lat * TM, d), F32)),
        grid=(b, nt),
        in_specs=[spec(dk, fwd, 0), spec(dk, fwd, 0), spec(d, fwd, 0), spec(dk, fwd, 0),
                  spec(dk, bwd, 0), spec(dk, bwd, 0), spec(d, bwd, 0), spec(dk, bwd, 1)],
        out_specs=(pl.BlockSpec((1, TM, d), lambda bi, j: (bi, jnp.maximum(j - 1, 0), 0)),
                   pl.BlockSpec((1, TM, d), lambda bi, j: (bi, n_lat - jnp.maximum(j, 1), 0))),
        scratch_shapes=[pltpu.VMEM((GLA_HEADS, GLA_DV, GLA_DK), F32), pltpu.VMEM((GLA_HEADS, GLA_DV, GLA_DK), F32)],
        compiler_params=_cparams(("parallel", "arbitrary")),
        name="gla_scan",
    )(q, k, v, g, q, k, v, g)


def _gla_out_kernel(of_ref, ob_ref, r_ref, x_ref, mod_ref, ng_ref, wo_ref, lng_ref, lnb_ref, rwt_ref, rb_ref,
                    x1_ref, ri_ref, rw_ref, cnt_ref, *, d):
    first = (pl.program_id(0) == 0) & (pl.program_id(1) == 0)
    o = of_ref[0] + ob_ref[0]
    r = r_ref[0].astype(F32)
    gate = _silu(r)
    ng = ng_ref[...]
    parts = []
    for h in range(GLA_HEADS):
        cols = slice(h * GLA_DV, (h + 1) * GLA_DV)
        oh = o[:, cols]
        ms = jnp.mean(oh * oh, axis=-1, keepdims=True)
        parts.append((oh * lax.rsqrt(ms + LN_EPS) * ng * gate[:, cols]).astype(BF16))
    y = _dot(jnp.concatenate(parts, axis=-1), wo_ref[...])
    _post_mixer(y, x_ref[0], mod_ref[0], lng_ref[...], lnb_ref[...], rwt_ref[...], rb_ref[...],
                x1_ref, ri_ref, rw_ref, cnt_ref, first, d)


def _gla_out_call(o_f, o_b, r, xs, mod, ng, wo, lng, lnb, rwt, rb):
    b, s_len, d = o_f.shape
    n_lat = s_len // TM
    row = lambda a: pl.BlockSpec(a.shape, lambda bi, j: (0,) * a.ndim)
    tile = pl.BlockSpec((1, TM, d), lambda bi, j: (bi, j, 0))
    return pl.pallas_call(
        functools.partial(_gla_out_kernel, d=d),
        out_shape=_route_out_shapes(b, n_lat, s_len, d),
        grid=(b, n_lat),
        in_specs=[tile, tile, tile, tile, pl.BlockSpec((1, 1, 6 * d), lambda bi, j: (bi, 0, 0)),
                  row(ng), row(wo), row(lng), row(lnb), row(rwt), row(rb)],
        out_specs=_route_out_specs(n_lat, d),
        compiler_params=_cparams(("arbitrary", "arbitrary")),
        name="gla_out_route",
    )(o_f, o_b, r, xs, mod, ng, wo, lng, lnb, rwt, rb)


def kernel(x, c, ctx, c_ctx, ada_w, ada_b, ln_g, ln_b, attn_w_qkv, attn_w_o, attn_lambda, attn_subln_g, gla_w_in, gla_wg_down, gla_wg_up, gla_bg, gla_norm_g, gla_w_o, router_w, router_b, moe_w_gate, moe_w_up, moe_w_down):
    b, s_len, d = x.shape
    c_len = ctx.shape[1]
    assert ada_w.shape[0] == DEPTH and c_len == TM and s_len % TM == 0 and s_len % GRID_W == 0
    assert d == DIFF_HEADS * 2 * DIFF_HEAD_DIM == GLA_HEADS * GLA_DV
    n_lat = s_len // TM
    nt = n_lat + 1
    ctx_row = b

    cond_rows = -(-(b + 1) // 8) * 8
    cond = jnp.concatenate([c, c_ctx[None, :], jnp.zeros((cond_rows - b - 1, d), F32)], axis=0)
    mods = _ada_call(cond, ada_w, ada_b)
    rwt = router_w.T
    rb = router_b.reshape(N_EXPERTS, 1)
    stream = jnp.concatenate([x, ctx], axis=1)

    mod0 = mods[0].reshape(cond_rows, 1, 6 * d)
    wqkv = attn_w_qkv[0]
    wqk = wqkv[:, :2 * d].astype(BF16)
    wvt = wqkv[:, 2 * d:].T.astype(BF16)
    tables = tuple(jnp.asarray(t) for t in _rope_tables(s_len, c_len))
    q, k, vt = _qkv_call(stream, mod0, wqk, wvt, tables, b, ctx_row)
    lambda_init = 0.8 - 0.6 * math.exp(-0.3 * 0)
    o = _attn_call(q, k, vt, attn_lambda[0], attn_subln_g[0], lambda_init)
    x1, ri, rw, counts = _attn_out_call(o, stream, mod0, attn_w_o[0].astype(BF16), ln_g[0, 0][None], ln_b[0, 0][None],
                                        rwt, rb, ctx_row)
    stream = _moe(x1, mod0, ri, rw, counts, ln_g[0, 1][None], ln_b[0, 1][None],
                  moe_w_gate[0], moe_w_up[0], moe_w_down[0], nt, ctx_row, n_lat)

    mod1 = mods[1].reshape(cond_rows, 1, 6 * d)
    dk = GLA_HEADS * GLA_DK
    wgd = jnp.concatenate([gla_wg_down[0, 0], gla_wg_down[0, 1],
                           jnp.zeros((d, LANES - 2 * GLA_GATE_RANK), F32)], axis=1).astype(BF16)
    wgu = jnp.zeros((LANES, 2 * dk), F32)
    wgu = wgu.at[0:GLA_GATE_RANK, 0:dk].set(gla_wg_up[0, 0])
    wgu = wgu.at[GLA_GATE_RANK:2 * GLA_GATE_RANK, dk:].set(gla_wg_up[0, 1]).astype(BF16)
    bg = gla_bg[0].reshape(1, 2 * dk)
    gq, gk, gv, gr, gg = _gla_proj_call(stream, mod1, gla_w_in[0].astype(BF16), wgd, wgu, bg, ctx_row)
    o_f, o_b = _gla_scan_call(gq, gk, gv, gg)
    x1, ri, rw, counts = _gla_out_call(o_f, o_b, gr, stream, mod1, gla_norm_g[0][None], gla_w_o[0].astype(BF16),
                                       ln_g[1, 0][None], ln_b[1, 0][None], rwt, rb)
    return _moe(x1, mod1, ri, rw, counts, ln_g[1, 1][None], ln_b[1, 1][None],
                moe_w_gate[1], moe_w_up[1], moe_w_down[1], n_lat, ctx_row, n_lat)
```

```python
import functools
import math

import numpy as np
import jax
import jax.numpy as jnp
from jax import lax
from jax.experimental import pallas as pl
from jax.experimental.pallas import tpu as pltpu

F32 = jnp.float32
BF16 = jnp.bfloat16
I32 = jnp.int32

DEPTH = 2
ALPHA = (2.0 * DEPTH) ** 0.25
LN_EPS = 1e-5
GRID_W = 64
ROPE_BASE = 10000.0
DIFF_HEADS = 8
DIFF_HEAD_DIM = 64
ROT_FREQS = DIFF_HEAD_DIM // 4
GLA_HEADS = 4
GLA_DK = 128
GLA_DV = 256
GLA_GATE_RANK = 16
GLA_TAU = 16.0
GLA_CHUNK = 64
N_EXPERTS = 16
N_GROUPS = 4
EXPERTS_PER_GROUP = 4
LOG2E = 1.4426950408889634

LANES = 128
TM = 256
TK = 256
TQ = 512
MOE_BLK = 256
ADA_TN = 1536
ROW_UNROLL = 8
VMEM_LIMIT = 48 * 1024 * 1024
NEG_BIG = -1e30


def _cparams(sem):
    return pltpu.CompilerParams(dimension_semantics=sem, vmem_limit_bytes=VMEM_LIMIT)


def _silu(x):
    return x * (1.0 / (1.0 + jnp.exp(-x)))


def _layer_norm(z, g, b):
    mu = jnp.mean(z, axis=-1, keepdims=True)
    zc = z - mu
    var = jnp.mean(zc * zc, axis=-1, keepdims=True)
    return zc * lax.rsqrt(var + LN_EPS) * g + b


def _dot(a, b):
    return jnp.dot(a, b, preferred_element_type=F32)


def _dot_nt(a, b, precision=None):
    return lax.dot_general(a, b, (((1,), (1,)), ((), ())), preferred_element_type=F32, precision=precision)


def _dot_tn(a, b):
    return lax.dot_general(a, b, (((0,), (0,)), ((), ())), preferred_element_type=F32)


def _ada_kernel(c_ref, w_ref, b_ref, o_ref):
    s = _silu(c_ref[...])
    o_ref[0] = jnp.dot(s, w_ref[0], preferred_element_type=F32, precision=lax.Precision.HIGHEST) + b_ref[0]


def _ada_call(cond, ada_w, ada_b):
    depth, d, n = ada_w.shape
    rows = cond.shape[0]
    return pl.pallas_call(
        _ada_kernel,
        out_shape=jax.ShapeDtypeStruct((depth, rows, n), F32),
        grid=(depth, n // ADA_TN),
        in_specs=[
            pl.BlockSpec((rows, d), lambda i, j: (0, 0)),
            pl.BlockSpec((1, d, ADA_TN), lambda i, j: (i, 0, j)),
            pl.BlockSpec((1, 1, ADA_TN), lambda i, j: (i, 0, j)),
        ],
        out_specs=pl.BlockSpec((1, rows, ADA_TN), lambda i, j: (i, 0, j)),
        compiler_params=_cparams(("parallel", "parallel")),
        name="ada_mod",
    )(cond, ada_w, ada_b.reshape(depth, 1, n))


def _qkv_kernel(x_ref, mod_ref, wqk_ref, wvt_ref, cos_ref, sa_ref, sb_ref, q_ref, k_ref, vt_ref, *, d):
    mod = mod_ref[0]
    h = (x_ref[0] * (1.0 + mod[:, d:2 * d]) + mod[:, 0:d]).astype(BF16)
    qk = _dot(h, wqk_ref[...])
    vt_ref[0, 0] = _dot_nt(wvt_ref[...], h).astype(BF16)
    cos, sa, sb = cos_ref[...], sa_ref[...], sb_ref[...]
    q_scale = DIFF_HEAD_DIM ** -0.5 * LOG2E
    for s in range(2 * d // LANES):
        z = qk[:, s * LANES:(s + 1) * LANES]
        zr = z * cos + pltpu.roll(z, LANES - ROT_FREQS, axis=1) * sa + pltpu.roll(z, ROT_FREQS, axis=1) * sb
        if s < d // LANES:
            q_ref[0, :, s * LANES:(s + 1) * LANES] = (zr * q_scale).astype(BF16)
        else:
            k_ref[0, :, (s * LANES - d):(s * LANES - d + LANES)] = zr.astype(BF16)


def _rope_tables(s_len, c_len):
    rows = s_len // GRID_W
    row = np.repeat(np.arange(rows, dtype=np.float32), GRID_W)
    col = np.tile(np.arange(GRID_W, dtype=np.float32), rows)
    inv_freq = np.float32(ROPE_BASE) ** (-np.arange(ROT_FREQS, dtype=np.float32) / np.float32(ROT_FREQS))
    ang_r = row[:, None] * inv_freq
    ang_c = col[:, None] * inv_freq
    ang = np.concatenate([ang_r, ang_r, ang_c, ang_c], axis=-1).astype(np.float64)
    cos, sin = np.cos(ang), np.sin(ang)
    first_half = (np.arange(DIFF_HEAD_DIM) // ROT_FREQS) % 2 == 0
    sa = np.where(first_half[None, :], -sin, 0.0)
    sb = np.where(first_half[None, :], 0.0, sin)
    pad = lambda t, fill: np.concatenate([t, np.full((c_len, DIFF_HEAD_DIM), fill)], axis=0)
    two = lambda t: np.concatenate([t, t], axis=-1).astype(np.float32)
    return two(pad(cos, 1.0)), two(pad(sa, 0.0)), two(pad(sb, 0.0))


def _qkv_call(xs, mod, wqk, wvt, tables, n_batch, ctx_row):
    b, t, d = xs.shape
    nt = t // TM
    n_lat = nt - 1
    mod_idx = lambda bi, j: (jnp.where(j < n_lat, bi, ctx_row), 0, 0)
    return pl.pallas_call(
        functools.partial(_qkv_kernel, d=d),
        out_shape=(jax.ShapeDtypeStruct((b, t, d), BF16), jax.ShapeDtypeStruct((b, t, d), BF16),
                   jax.ShapeDtypeStruct((b, nt, d, TM), BF16)),
        grid=(b, nt),
        in_specs=[
            pl.BlockSpec((1, TM, d), lambda bi, j: (bi, j, 0)),
            pl.BlockSpec((1, 1, 6 * d), mod_idx),
            pl.BlockSpec((d, 2 * d), lambda bi, j: (0, 0)),
            pl.BlockSpec((d, d), lambda bi, j: (0, 0)),
            pl.BlockSpec((TM, LANES), lambda bi, j: (j, 0)),
            pl.BlockSpec((TM, LANES), lambda bi, j: (j, 0)),
            pl.BlockSpec((TM, LANES), lambda bi, j: (j, 0)),
        ],
        out_specs=(pl.BlockSpec((1, TM, d), lambda bi, j: (bi, j, 0)),
                   pl.BlockSpec((1, TM, d), lambda bi, j: (bi, j, 0)),
                   pl.BlockSpec((1, 1, d, TM), lambda bi, j: (bi, j, 0, 0))),
        compiler_params=_cparams(("parallel", "parallel")),
        name="qkv_rope",
    )(xs, mod, wqk, wvt, *tables)


def _attn_kernel(q_ref, k_ref, vt_ref, lam_ref, g_ref, o_ref, *, chunks, lambda_init, tq):
    q = q_ref[0]
    lane = lax.broadcasted_iota(I32, q.shape, 1)
    zero = jnp.zeros_like(q)
    qq = jnp.concatenate([jnp.where(lane < DIFF_HEAD_DIM, q, zero),
                          jnp.where(lane >= DIFF_HEAD_DIM, q, zero)], axis=0)
    m = jnp.full((1, 2 * tq), NEG_BIG, F32)
    l = jnp.zeros((1, 2 * tq), F32)
    acc = jnp.zeros((LANES, 2 * tq), F32)
    scores = lambda c: _dot_nt(k_ref[0, c * TK:(c + 1) * TK, :], qq)
    st_next = scores(chunks[0])
    for i, c in enumerate(chunks):
        st = st_next
        if i + 1 < len(chunks):
            st_next = scores(chunks[i + 1])
        m_new = jnp.maximum(m, jnp.max(st, axis=0, keepdims=True))
        alpha = jnp.exp2(m - m_new)
        p = jnp.exp2(st - m_new)
        l = l * alpha + jnp.sum(p, axis=0, keepdims=True)
        acc = acc * alpha + _dot(vt_ref[0, c], p.astype(BF16))
        m = m_new
    acc = acc / l
    lv = lam_ref[...]
    lam = (jnp.exp(jnp.sum(lv[0:1] * lv[1:2], axis=1, keepdims=True))
           - jnp.exp(jnp.sum(lv[2:3] * lv[3:4], axis=1, keepdims=True)) + lambda_init)
    ot = acc[:, :tq] - lam * acc[:, tq:]
    ms = jnp.mean(ot * ot, axis=0, keepdims=True)
    ot = ot * lax.rsqrt(ms + LN_EPS) * g_ref[...] * (1.0 - lambda_init)
    o_ref[0] = ot.T.astype(BF16)


def _attn_call(q, k, vt, lam_vecs, subln_g, lambda_init, *, tq, q_row0, n_rows, chunks, name):
    b, t, d = q.shape
    nt = t // TM
    q0 = q_row0 // tq
    return pl.pallas_call(
        functools.partial(_attn_kernel, chunks=chunks, lambda_init=lambda_init, tq=tq),
        out_shape=jax.ShapeDtypeStruct((b, n_rows, d), BF16),
        grid=(b, DIFF_HEADS, n_rows // tq),
        in_specs=[
            pl.BlockSpec((1, tq, LANES), lambda bi, h, qi: (bi, q0 + qi, h)),
            pl.BlockSpec((1, t, LANES), lambda bi, h, qi: (bi, 0, h)),
            pl.BlockSpec((1, nt, LANES, TM), lambda bi, h, qi: (bi, 0, h, 0)),
            pl.BlockSpec((4, DIFF_HEAD_DIM), lambda bi, h, qi: (0, 0)),
            pl.BlockSpec((LANES, 1), lambda bi, h, qi: (0, 0)),
        ],
        out_specs=pl.BlockSpec((1, tq, LANES), lambda bi, h, qi: (bi, qi, h)),
        compiler_params=_cparams(("parallel", "parallel", "parallel")),
        name=name,
    )(q, k, vt, lam_vecs, subln_g.reshape(LANES, 1))


def _route(logits_t, rb, ri_ref, rw_ref, cnt_ref, first):
    tm = logits_t.shape[1]
    aff = 1.0 / (1.0 + jnp.exp(-logits_t))
    sel = aff + rb
    row = lambda a, e: a[e:e + 1, :]
    gscore = []
    for g in range(N_GROUPS):
        a, b, c, d = (row(sel, g * EXPERTS_PER_GROUP + j) for j in range(EXPERTS_PER_GROUP))
        hi1, lo1, hi2, lo2 = jnp.maximum(a, b), jnp.minimum(a, b), jnp.maximum(c, d), jnp.minimum(c, d)
        gscore.append(jnp.maximum(hi1, hi2) + jnp.maximum(jnp.minimum(hi1, hi2), jnp.maximum(lo1, lo2)))
    best, grp = gscore[0], jnp.zeros((1, tm), I32)
    for g in range(1, N_GROUPS):
        better = gscore[g] > best
        best = jnp.where(better, gscore[g], best)
        grp = jnp.where(better, g, grp)
    in_sel, in_aff = [], []
    for j in range(EXPERTS_PER_GROUP):
        s_j, a_j = row(sel, j), row(aff, j)
        for g in range(1, N_GROUPS):
            s_j = jnp.where(grp == g, row(sel, g * EXPERTS_PER_GROUP + j), s_j)
            a_j = jnp.where(grp == g, row(aff, g * EXPERTS_PER_GROUP + j), a_j)
        in_sel.append(s_j)
        in_aff.append(a_j)
    v0, i0, a0 = in_sel[0], jnp.zeros((1, tm), I32), in_aff[0]
    for j in range(1, EXPERTS_PER_GROUP):
        better = in_sel[j] > v0
        v0 = jnp.where(better, in_sel[j], v0)
        i0 = jnp.where(better, j, i0)
        a0 = jnp.where(better, in_aff[j], a0)
    v1, i1, a1 = jnp.full((1, tm), -jnp.inf, F32), jnp.zeros((1, tm), I32), jnp.zeros((1, tm), F32)
    for j in range(EXPERTS_PER_GROUP):
        better = (in_sel[j] > v1) & (i0 != j)
        v1 = jnp.where(better, in_sel[j], v1)
        i1 = jnp.where(better, j, i1)
        a1 = jnp.where(better, in_aff[j], a1)
    e0 = grp * EXPERTS_PER_GROUP + i0
    e1 = grp * EXPERTS_PER_GROUP + i1
    wsum = a0 + a1
    erow = lax.broadcasted_iota(I32, (N_EXPERTS, tm), 0)
    oh0 = (erow == e0).astype(F32)
    oh1 = (erow == e1).astype(F32)
    both = oh0 + oh1
    tri = (lax.broadcasted_iota(I32, (tm, tm), 0) < lax.broadcasted_iota(I32, (tm, tm), 1)).astype(BF16)
    before = _dot(both.astype(BF16), tri)

    @pl.when(first)
    def _():
        cnt_ref[...] = jnp.zeros_like(cnt_ref)

    base = cnt_ref[:, 0:1] + before
    r0 = jnp.sum(oh0 * base, axis=0, keepdims=True)
    r1 = jnp.sum(oh1 * base, axis=0, keepdims=True)
    cnt_ref[...] = cnt_ref[...] + jnp.sum(both, axis=1, keepdims=True)
    ri_ref[0, 0:1, :] = e0
    ri_ref[0, 1:2, :] = e1
    ri_ref[0, 2:3, :] = r0.astype(I32)
    ri_ref[0, 3:4, :] = r1.astype(I32)
    ri_ref[0, 4:8, :] = jnp.zeros((4, tm), I32)
    rw_ref[0, 0:1, :] = a0 / wsum
    rw_ref[0, 1:2, :] = a1 / wsum
    rw_ref[0, 2:8, :] = jnp.zeros((6, tm), F32)


def _post_mixer(y, x, mod, lng, lnb, rwt, rb, x1_ref, ri_ref, rw_ref, cnt_ref, first, d):
    x1 = _layer_norm(ALPHA * x + mod[:, 2 * d:3 * d] * y, lng, lnb)
    x1_ref[0] = x1
    tok = x1 * (1.0 + mod[:, 4 * d:5 * d]) + mod[:, 3 * d:4 * d]
    logits_t = _dot_nt(rwt, tok, precision=lax.Precision.HIGHEST)
    _route(logits_t, rb, ri_ref, rw_ref, cnt_ref, first)


def _attn_out_kernel(ol_ref, oc_ref, x_ref, mod_ref, wo_ref, lng_ref, lnb_ref, rwt_ref, rb_ref,
                     x1_ref, ri_ref, rw_ref, cnt_ref, *, d, n_lat):
    first = (pl.program_id(0) == 0) & (pl.program_id(1) == 0)
    o = jnp.where(pl.program_id(1) < n_lat, ol_ref[0], oc_ref[0])
    y = _dot(o, wo_ref[...])
    _post_mixer(y, x_ref[0], mod_ref[0], lng_ref[...], lnb_ref[...], rwt_ref[...], rb_ref[...],
                x1_ref, ri_ref, rw_ref, cnt_ref, first, d)


def _route_out_shapes(b, n_tiles, t_out, d):
    return (jax.ShapeDtypeStruct((b, t_out, d), F32),
            jax.ShapeDtypeStruct((b * n_tiles, 8, TM), I32),
            jax.ShapeDtypeStruct((b * n_tiles, 8, TM), F32),
            jax.ShapeDtypeStruct((N_EXPERTS, LANES), F32))


def _route_out_specs(n_tiles, d):
    return (pl.BlockSpec((1, TM, d), lambda bi, j: (bi, j, 0)),
            pl.BlockSpec((1, 8, TM), lambda bi, j: (bi * n_tiles + j, 0, 0)),
            pl.BlockSpec((1, 8, TM), lambda bi, j: (bi * n_tiles + j, 0, 0)),
            pl.BlockSpec((N_EXPERTS, LANES), lambda bi, j: (0, 0)))


def _attn_out_call(o_lat, o_ctx, xs, mod, wo, lng, lnb, rwt, rb, ctx_row):
    b, t, d = xs.shape
    nt = t // TM
    n_lat = nt - 1
    row = lambda a: pl.BlockSpec(a.shape, lambda bi, j: (0,) * a.ndim)
    return pl.pallas_call(
        functools.partial(_attn_out_kernel, d=d, n_lat=n_lat),
        out_shape=_route_out_shapes(b, nt, t, d),
        grid=(b, nt),
        in_specs=[
            pl.BlockSpec((1, TM, d), lambda bi, j: (bi, jnp.minimum(j, n_lat - 1), 0)),
            pl.BlockSpec((1, TM, d), lambda bi, j: (bi, 0, 0)),
            pl.BlockSpec((1, TM, d), lambda bi, j: (bi, j, 0)),
            pl.BlockSpec((1, 1, 6 * d), lambda bi, j: (jnp.where(j < n_lat, bi, ctx_row), 0, 0)),
            row(wo), row(lng), row(lnb), row(rwt), row(rb),
        ],
        out_specs=_route_out_specs(nt, d),
        compiler_params=_cparams(("arbitrary", "arbitrary")),
        name="attn_out_route",
    )(o_lat, o_ctx, xs, mod, wo, lng, lnb, rwt, rb)


def _row_copy(src, r_src, dst, r_dst, sem):
    return pltpu.make_async_copy(src.at[pl.ds(r_src, 1)], dst.at[pl.ds(r_dst, 1)], sem)


def _dispatch_kernel(pad_ref, cnt_ref, x_ref, mod_ref, ri_ref, xs_ref, tok_ref, zero_ref, ri_smem, sem, sem_s,
                     *, d, n_steps):
    step = pl.program_id(0) * pl.num_programs(1) + pl.program_id(1)
    idx_copy = pltpu.make_async_copy(ri_ref.at[0], ri_smem, sem_s)
    idx_copy.start()
    mod = mod_ref[0]
    tok_ref[...] = x_ref[0] * (1.0 + mod[:, 4 * d:5 * d]) + mod[:, 3 * d:4 * d]
    idx_copy.wait()

    def issue(r, carry):
        for k in range(2):
            dest = pad_ref[ri_smem[k, r]] + ri_smem[2 + k, r]
            _row_copy(tok_ref, r, xs_ref, dest, sem).start()
        return carry

    lax.fori_loop(0, TM, issue, 0, unroll=ROW_UNROLL)

    def drain(r, carry):
        for k in range(2):
            _row_copy(tok_ref, 0, xs_ref, 0, sem).wait()
        return carry

    lax.fori_loop(0, TM, drain, 0, unroll=ROW_UNROLL)

    @pl.when(step == n_steps - 1)
    def _():
        zero_ref[...] = jnp.zeros_like(zero_ref)
        for e in range(N_EXPERTS):
            lo = pad_ref[e] + cnt_ref[e]
            hi = pad_ref[e + 1]

            def fill(r, carry):
                _row_copy(zero_ref, 0, xs_ref, r, sem).start()
                return carry

            def fill_wait(r, carry):
                _row_copy(zero_ref, 0, xs_ref, 0, sem).wait()
                return carry

            lax.fori_loop(lo, hi, fill, 0)
            lax.fori_loop(lo, hi, fill_wait, 0)

        zrows = zero_ref.shape[0]

        def tail_copy(r):
            return pltpu.make_async_copy(zero_ref, xs_ref.at[pl.ds(pl.multiple_of(r * zrows, zrows), zrows)], sem)

        def tail(r, carry):
            tail_copy(r).start()
            return carry

        def tail_wait(r, carry):
            tail_copy(r).wait()
            return carry

        lo, hi = pad_ref[N_EXPERTS] // zrows, xs_ref.shape[0] // zrows
        lax.fori_loop(lo, hi, tail, 0)
        lax.fori_loop(lo, hi, tail_wait, 0)


def _dispatch_call(pad_start, counts, x1, mod, ri, n_tiles, n_pad, ctx_row, n_lat):
    b, t, d = x1.shape
    grid_spec = pltpu.PrefetchScalarGridSpec(
        num_scalar_prefetch=2,
        grid=(b, n_tiles),
        in_specs=[
            pl.BlockSpec((1, TM, d), lambda bi, j, *_: (bi, j, 0)),
            pl.BlockSpec((1, 1, 6 * d), lambda bi, j, *_: (jnp.where(j < n_lat, bi, ctx_row), 0, 0)),
            pl.BlockSpec((1, 8, TM), lambda bi, j, *_: (bi * n_tiles + j, 0, 0)),
        ],
        out_specs=pl.BlockSpec(memory_space=pl.ANY),
        scratch_shapes=[pltpu.VMEM((TM, d), F32), pltpu.VMEM((8, d), F32), pltpu.SMEM((8, TM), I32),
                        pltpu.SemaphoreType.DMA, pltpu.SemaphoreType.DMA],
    )
    return pl.pallas_call(
        functools.partial(_dispatch_kernel, d=d, n_steps=b * n_tiles),
        out_shape=jax.ShapeDtypeStruct((n_pad, d), F32),
        grid_spec=grid_spec,
        compiler_params=_cparams(("arbitrary", "arbitrary")),
        name="moe_dispatch",
    )(pad_start, counts, x1, mod, ri)


def _ffn_kernel(be_ref, nu_ref, xs_ref, wg_ref, wu_ref, wd_ref, ys_ref, wg_s, wu_s, wd_s):
    i = pl.program_id(0)

    @pl.when(i < nu_ref[0])
    def _():
        prev = be_ref[jnp.maximum(i - 1, 0)]

        @pl.when((i == 0) | (be_ref[i] != prev))
        def _():
            wg_s[...] = wg_ref[0].astype(BF16)
            wu_s[...] = wu_ref[0].astype(BF16)
            wd_s[...] = wd_ref[0].astype(BF16)

        x = xs_ref[...].astype(BF16)
        g = _dot(x, wg_s[...])
        u = _dot(x, wu_s[...])
        ys_ref[...] = _dot((_silu(g) * u).astype(BF16), wd_s[...])

    @pl.when(i >= nu_ref[0])
    def _():
        ys_ref[...] = jnp.zeros_like(ys_ref)


def _ffn_call(block_expert, n_used, xs, w_gate, w_up, w_down):
    n_pad, d = xs.shape
    de = w_gate.shape[-1]
    n_blocks = n_pad // MOE_BLK
    blk = lambda i, be, nu: (jnp.minimum(i, nu[0] - 1), 0)
    exp = lambda i, be, nu: (be[jnp.minimum(i, nu[0] - 1)], 0, 0)
    grid_spec = pltpu.PrefetchScalarGridSpec(
        num_scalar_prefetch=2,
        grid=(n_blocks,),
        in_specs=[
            pl.BlockSpec((MOE_BLK, d), blk),
            pl.BlockSpec((1, d, de), exp),
            pl.BlockSpec((1, d, de), exp),
            pl.BlockSpec((1, de, d), exp),
        ],
        out_specs=pl.BlockSpec((MOE_BLK, d), lambda i, be, nu: (i, 0)),
        scratch_shapes=[pltpu.VMEM((d, de), BF16), pltpu.VMEM((d, de), BF16), pltpu.VMEM((de, d), BF16)],
    )
    return pl.pallas_call(
        _ffn_kernel,
        out_shape=jax.ShapeDtypeStruct((n_pad, d), F32),
        grid_spec=grid_spec,
        compiler_params=_cparams(("arbitrary",)),
        name="moe_ffn",
    )(block_expert, n_used, xs, w_gate, w_up, w_down)


def _combine_kernel(pad_ref, x_ref, mod_ref, ri_ref, rw_ref, lng_ref, lnb_ref, ys_ref, o_ref,
                    y0_ref, y1_ref, ri_smem, sem, sem_s, *, d):
    idx_copy = pltpu.make_async_copy(ri_ref.at[0], ri_smem, sem_s)
    idx_copy.start()
    idx_copy.wait()
    bufs = (y0_ref, y1_ref)

    def issue(r, carry):
        for k in range(2):
            src = pad_ref[ri_smem[k, r]] + ri_smem[2 + k, r]
            _row_copy(ys_ref, src, bufs[k], r, sem).start()
        return carry

    lax.fori_loop(0, TM, issue, 0, unroll=ROW_UNROLL)

    def drain(r, carry):
        for k in range(2):
            _row_copy(ys_ref, 0, bufs[k], 0, sem).wait()
        return carry

    lax.fori_loop(0, TM, drain, 0, unroll=ROW_UNROLL)
    w = rw_ref[0]
    moe = w[:, 0:1] * y0_ref[...] + w[:, 1:2] * y1_ref[...]
    mod = mod_ref[0]
    o_ref[0] = _layer_norm(ALPHA * x_ref[0] + mod[:, 5 * d:6 * d] * moe, lng_ref[...], lnb_ref[...])


def _combine_call(pad_start, x1, mod, ri, rw_t, lng, lnb, ys, n_tiles, ctx_row, n_lat):
    b, _, d = x1.shape
    row = lambda a: pl.BlockSpec(a.shape, lambda bi, j, *_: (0,) * a.ndim)
    grid_spec = pltpu.PrefetchScalarGridSpec(
        num_scalar_prefetch=1,
        grid=(b, n_tiles),
        in_specs=[
            pl.BlockSpec((1, TM, d), lambda bi, j, *_: (bi, j, 0)),
            pl.BlockSpec((1, 1, 6 * d), lambda bi, j, *_: (jnp.where(j < n_lat, bi, ctx_row), 0, 0)),
            pl.BlockSpec((1, 8, TM), lambda bi, j, *_: (bi * n_tiles + j, 0, 0)),
            pl.BlockSpec((1, TM, 8), lambda bi, j, *_: (bi * n_tiles + j, 0, 0)),
            row(lng), row(lnb),
            pl.BlockSpec(memory_space=pl.ANY),
        ],
        out_specs=pl.BlockSpec((1, TM, d), lambda bi, j, *_: (bi, j, 0)),
        scratch_shapes=[pltpu.VMEM((TM, d), F32), pltpu.VMEM((TM, d), F32), pltpu.SMEM((8, TM), I32),
                        pltpu.SemaphoreType.DMA, pltpu.SemaphoreType.DMA],
    )
    return pl.pallas_call(
        functools.partial(_combine_kernel, d=d),
        out_shape=jax.ShapeDtypeStruct((b, n_tiles * TM, d), F32),
        grid_spec=grid_spec,
        compiler_params=_cparams(("arbitrary", "arbitrary")),
        name="moe_combine",
    )(pad_start, x1, mod, ri, rw_t, lng, lnb, ys)


def _moe(x1, mod, ri, rw, counts, lng, lnb, w_gate, w_up, w_down, n_tiles, ctx_row, n_lat):
    b, _, d = x1.shape
    n_assign = 2 * b * n_tiles * TM
    n_pad = n_assign + N_EXPERTS * MOE_BLK
    n_blocks = n_pad // MOE_BLK
    cnt = counts[:, 0].astype(I32)
    padded = (cnt + MOE_BLK - 1) // MOE_BLK * MOE_BLK
    pad_end = jnp.cumsum(padded)
    pad_start = jnp.concatenate([jnp.zeros((1,), I32), pad_end]).astype(I32)
    block_row0 = jnp.arange(n_blocks, dtype=I32) * MOE_BLK
    block_expert = jnp.minimum(
        jnp.sum((pad_end[None, :] <= block_row0[:, None]).astype(I32), axis=1), N_EXPERTS - 1).astype(I32)
    n_used = (pad_end[-1:] // MOE_BLK).astype(I32)
    xs = _dispatch_call(pad_start, cnt, x1, mod, ri, n_tiles, n_pad, ctx_row, n_lat)
    ys = _ffn_call(block_expert, n_used, xs, w_gate, w_up, w_down)
    rw_t = jnp.swapaxes(rw, 1, 2)
    return _combine_call(pad_start, x1, mod, ri, rw_t, lng, lnb, ys, n_tiles, ctx_row, n_lat)


def _gla_proj_kernel(x_ref, mod_ref, win_ref, wgd_ref, wgu_ref, bg_ref, q_ref, k_ref, v_ref, r_ref, g_ref, *, d):
    mod = mod_ref[0]
    h = (x_ref[0] * (1.0 + mod[:, d:2 * d]) + mod[:, 0:d]).astype(BF16)
    dk = GLA_HEADS * GLA_DK
    p = _dot(h, win_ref[...])
    q_ref[0] = (p[:, 0:dk] * GLA_DK ** -0.5).astype(BF16)
    k_ref[0] = p[:, dk:2 * dk].astype(BF16)
    v_ref[0] = p[:, 2 * dk:2 * dk + d].astype(BF16)
    r_ref[0] = p[:, 2 * dk + d:].astype(BF16)
    low = _dot(h, wgd_ref[...]).astype(BF16)
    pre = _dot(low, wgu_ref[...]) + bg_ref[...]
    g_ref[0] = (jnp.minimum(pre, 0.0) - jnp.log1p(jnp.exp(-jnp.abs(pre)))) * (1.0 / GLA_TAU)


def _gla_proj_call(xs, mod, win, wgd, wgu, bg, ctx_row):
    b, t, d = xs.shape
    nt = t // TM
    n_lat = nt - 1
    dk = GLA_HEADS * GLA_DK
    row = lambda a: pl.BlockSpec(a.shape, lambda bi, j: (0,) * a.ndim)
    tile = lambda n: pl.BlockSpec((1, TM, n), lambda bi, j: (bi, j, 0))
    return pl.pallas_call(
        functools.partial(_gla_proj_kernel, d=d),
        out_shape=(jax.ShapeDtypeStruct((b, t, dk), BF16), jax.ShapeDtypeStruct((b, t, dk), BF16),
                   jax.ShapeDtypeStruct((b, t, d), BF16), jax.ShapeDtypeStruct((b, t, d), BF16),
                   jax.ShapeDtypeStruct((b, t, 2 * dk), F32)),
        grid=(b, nt),
        in_specs=[tile(d), pl.BlockSpec((1, 1, 6 * d), lambda bi, j: (jnp.where(j < n_lat, bi, ctx_row), 0, 0)),
                  row(win), row(wgd), row(wgu), row(bg)],
        out_specs=(tile(dk), tile(dk), tile(d), tile(d), tile(2 * dk)),
        compiler_params=_cparams(("parallel", "parallel")),
        name="gla_proj",
    )(xs, mod, win, wgd, wgu, bg)


def _split3(x):
    hi = x.astype(BF16)
    r1 = x - hi.astype(F32)
    mid = r1.astype(BF16)
    lo = (r1 - mid.astype(F32)).astype(BF16)
    return hi, mid, lo


def _gla_direction(q_ref, k_ref, v_ref, g_ref, o_ref, state_ref, reverse):
    L = GLA_CHUNK
    ri = lax.broadcasted_iota(I32, (L, L), 0)
    ci = lax.broadcasted_iota(I32, (L, L), 1)
    keep = (ci >= ri) if reverse else (ci <= ri)
    tri = keep.astype(BF16)
    chunks = range(TM // L)
    for c in (reversed(chunks) if reverse else chunks):
        rows = slice(c * L, (c + 1) * L)
        g = g_ref[0, rows, :]
        b_all = sum(_dot(tri, part) for part in _split3(g))
        edge = 0 if reverse else L - 1
        for h in range(GLA_HEADS):
            kcols = slice(h * GLA_DK, (h + 1) * GLA_DK)
            vcols = slice(h * GLA_DV, (h + 1) * GLA_DV)
            bh = b_all[:, kcols]
            b_last = bh[edge:edge + 1, :]
            q = q_ref[0, rows, kcols].astype(F32)
            k = k_ref[0, rows, kcols].astype(F32)
            v = v_ref[0, rows, vcols]
            qd = (q * jnp.exp(bh)).astype(BF16)
            kd = (k * jnp.exp(-bh)).astype(BF16)
            a = jnp.where(keep, _dot_nt(qd, kd), 0.0).astype(BF16)
            state_t = state_ref[h]
            o_ref[0, rows, vcols] = _dot(a, v) + _dot_nt(qd, state_t.astype(BF16))
            kl = (k * jnp.exp(b_last - bh)).astype(BF16)
            state_ref[h] = jnp.exp(b_last) * state_t + _dot_tn(v, kl)


def _gla_scan_kernel(qf, kf, vf, gf, qb, kb, vb, gb, of_ref, ob_ref, sf_ref, sb_ref):
    @pl.when(pl.program_id(1) == 0)
    def _():
        sf_ref[...] = jnp.zeros_like(sf_ref)
        sb_ref[...] = jnp.zeros_like(sb_ref)

    _gla_direction(qf, kf, vf, gf, of_ref, sf_ref, False)
    _gla_direction(qb, kb, vb, gb, ob_ref, sb_ref, True)


def _gla_scan_call(q, k, v, g):
    b, t, d = v.shape
    nt = t // TM
    n_lat = nt - 1
    dk = GLA_HEADS * GLA_DK
    fwd = lambda j: jnp.where(j == 0, n_lat, j - 1)
    bwd = lambda j: jnp.where(j == 0, n_lat, n_lat - j)
    spec = lambda n, idx, col: pl.BlockSpec((1, TM, n), lambda bi, j: (bi, idx(j), col))
    return pl.pallas_call(
        _gla_scan_kernel,
        out_shape=(jax.ShapeDtypeStruct((b, n_lat * TM, d), F32), jax.ShapeDtypeStruct((b, n_lat * TM, d), F32)),
        grid=(b, nt),
        in_specs=[spec(dk, fwd, 0), spec(dk, fwd, 0), spec(d, fwd, 0), spec(dk, fwd, 0),
                  spec(dk, bwd, 0), spec(dk, bwd, 0), spec(d, bwd, 0), spec(dk, bwd, 1)],
        out_specs=(pl.BlockSpec((1, TM, d), lambda bi, j: (bi, jnp.maximum(j - 1, 0), 0)),
                   pl.BlockSpec((1, TM, d), lambda bi, j: (bi, n_lat - jnp.maximum(j, 1), 0))),
        scratch_shapes=[pltpu.VMEM((GLA_HEADS, GLA_DV, GLA_DK), F32), pltpu.VMEM((GLA_HEADS, GLA_DV, GLA_DK), F32)],
        compiler_params=_cparams(("parallel", "arbitrary")),
        name="gla_scan",
    )(q, k, v, g, q, k, v, g)


def _gla_out_kernel(of_ref, ob_ref, r_ref, x_ref, mod_ref, ng_ref, wo_ref, lng_ref, lnb_ref, rwt_ref, rb_ref,
                    x1_ref, ri_ref, rw_ref, cnt_ref, *, d):
    first = (pl.program_id(0) == 0) & (pl.program_id(1) == 0)
    o = of_ref[0] + ob_ref[0]
    r = r_ref[0].astype(F32)
    gate = _silu(r)
    ng = ng_ref[...]
    parts = []
    for h in range(GLA_HEADS):
        cols = slice(h * GLA_DV, (h + 1) * GLA_DV)
        oh = o[:, cols]
        ms = jnp.mean(oh * oh, axis=-1, keepdims=True)
        parts.append((oh * lax.rsqrt(ms + LN_EPS) * ng * gate[:, cols]).astype(BF16))
    y = _dot(jnp.concatenate(parts, axis=-1), wo_ref[...])
    _post_mixer(y, x_ref[0], mod_ref[0], lng_ref[...], lnb_ref[...], rwt_ref[...], rb_ref[...],
                x1_ref, ri_ref, rw_ref, cnt_ref, first, d)


def _gla_out_call(o_f, o_b, r, xs, mod, ng, wo, lng, lnb, rwt, rb):
    b, s_len, d = o_f.shape
    n_lat = s_len // TM
    row = lambda a: pl.BlockSpec(a.shape, lambda bi, j: (0,) * a.ndim)
    tile = pl.BlockSpec((1, TM, d), lambda bi, j: (bi, j, 0))
    return pl.pallas_call(
        functools.partial(_gla_out_kernel, d=d),
        out_shape=_route_out_shapes(b, n_lat, s_len, d),
        grid=(b, n_lat),
        in_specs=[tile, tile, tile, tile, pl.BlockSpec((1, 1, 6 * d), lambda bi, j: (bi, 0, 0)),
                  row(ng), row(wo), row(lng), row(lnb), row(rwt), row(rb)],
        out_specs=_route_out_specs(n_lat, d),
        compiler_params=_cparams(("arbitrary", "arbitrary")),
        name="gla_out_route",
    )(o_f, o_b, r, xs, mod, ng, wo, lng, lnb, rwt, rb)


def kernel(x, c, ctx, c_ctx, ada_w, ada_b, ln_g, ln_b, attn_w_qkv, attn_w_o, attn_lambda, attn_subln_g, gla_w_in, gla_wg_down, gla_wg_up, gla_bg, gla_norm_g, gla_w_o, router_w, router_b, moe_w_gate, moe_w_up, moe_w_down):
    b, s_len, d = x.shape
    c_len = ctx.shape[1]
    assert ada_w.shape[0] == DEPTH and c_len == TM and s_len % TQ == 0 and TQ % TM == 0 and s_len % GRID_W == 0
    assert d == DIFF_HEADS * 2 * DIFF_HEAD_DIM == GLA_HEADS * GLA_DV
    n_lat = s_len // TM
    nt = n_lat + 1
    ctx_row = b

    cond_rows = -(-(b + 1) // 8) * 8
    cond = jnp.concatenate([c, c_ctx[None, :], jnp.zeros((cond_rows - b - 1, d), F32)], axis=0)
    mods = _ada_call(cond, ada_w, ada_b)
    rwt = router_w.T
    rb = router_b.reshape(N_EXPERTS, 1)
    stream = jnp.concatenate([x, ctx], axis=1)

    mod0 = mods[0].reshape(cond_rows, 1, 6 * d)
    wqkv = attn_w_qkv[0]
    wqk = wqkv[:, :2 * d].astype(BF16)
    wvt = wqkv[:, 2 * d:].T.astype(BF16)
    tables = tuple(jnp.asarray(t) for t in _rope_tables(s_len, c_len))
    q, k, vt = _qkv_call(stream, mod0, wqk, wvt, tables, b, ctx_row)
    lambda_init = 0.8 - 0.6 * math.exp(-0.3 * 0)
    attn = functools.partial(_attn_call, q, k, vt, attn_lambda[0], attn_subln_g[0], lambda_init)
    o_lat = attn(tq=TQ, q_row0=0, n_rows=s_len, chunks=tuple(range(nt)), name="diff_attn_lat")
    o_ctx = attn(tq=c_len, q_row0=s_len, n_rows=c_len, chunks=(n_lat,), name="diff_attn_ctx")
    x1, ri, rw, counts = _attn_out_call(o_lat, o_ctx, stream, mod0, attn_w_o[0].astype(BF16), ln_g[0, 0][None], ln_b[0, 0][None],
                                        rwt, rb, ctx_row)
    stream = _moe(x1, mod0, ri, rw, counts, ln_g[0, 1][None], ln_b[0, 1][None],
                  moe_w_gate[0], moe_w_up[0], moe_w_down[0], nt, ctx_row, n_lat)

    mod1 = mods[1].reshape(cond_rows, 1, 6 * d)
    dk = GLA_HEADS * GLA_DK
    wgd = jnp.concatenate([gla_wg_down[0, 0], gla_wg_down[0, 1],
                           jnp.zeros((d, LANES - 2 * GLA_GATE_RANK), F32)], axis=1).astype(BF16)
    wgu = jnp.zeros((LANES, 2 * dk), F32)
    wgu = wgu.at[0:GLA_GATE_RANK, 0:dk].set(gla_wg_up[0, 0])
    wgu = wgu.at[GLA_GATE_RANK:2 * GLA_GATE_RANK, dk:].set(gla_wg_up[0, 1]).astype(BF16)
    bg = gla_bg[0].reshape(1, 2 * dk)
    gq, gk, gv, gr, gg = _gla_proj_call(stream, mod1, gla_w_in[0].astype(BF16), wgd, wgu, bg, ctx_row)
    o_f, o_b = _gla_scan_call(gq, gk, gv, gg)
    x1, ri, rw, counts = _gla_out_call(o_f, o_b, gr, stream, mod1, gla_norm_g[0][None], gla_w_o[0].astype(BF16),
                                       ln_g[1, 0][None], ln_b[1, 0][None], rwt, rb)
    return _moe(x1, mod1, ri, rw, counts, ln_g[1, 1][None], ln_b[1, 1][None],
                moe_w_gate[1], moe_w_up[1], moe_w_down[1], n_lat, ctx_row, n_lat)
```

```python
import functools
import math

import numpy as np
import jax
import jax.numpy as jnp
from jax import lax
from jax.experimental import pallas as pl
from jax.experimental.pallas import tpu as pltpu

F32 = jnp.float32
BF16 = jnp.bfloat16
I32 = jnp.int32

DEPTH = 2
ALPHA = (2.0 * DEPTH) ** 0.25
LN_EPS = 1e-5
GRID_W = 64
ROPE_BASE = 10000.0
DIFF_HEADS = 8
DIFF_HEAD_DIM = 64
ROT_FREQS = DIFF_HEAD_DIM // 4
GLA_HEADS = 4
GLA_DK = 128
GLA_DV = 256
GLA_GATE_RANK = 16
GLA_TAU = 16.0
GLA_CHUNK = 64
N_EXPERTS = 16
N_GROUPS = 4
EXPERTS_PER_GROUP = 4
LOG2E = 1.4426950408889634

LANES = 128
TM = 256
TK = 256
TQ = 512
MOE_BLK = 256
ADA_TN = 1536
ROW_GROUP = 8
STAGE_ROWS = 640
TAB_ROWS, TAB_LANES = 3, 32
VMEM_LIMIT = 48 * 1024 * 1024
NEG_BIG = -1e30


def _cparams(sem):
    return pltpu.CompilerParams(dimension_semantics=sem, vmem_limit_bytes=VMEM_LIMIT)


def _silu(x):
    return x * (1.0 / (1.0 + jnp.exp(-x)))


def _layer_norm(z, g, b):
    mu = jnp.mean(z, axis=-1, keepdims=True)
    zc = z - mu
    var = jnp.mean(zc * zc, axis=-1, keepdims=True)
    return zc * lax.rsqrt(var + LN_EPS) * g + b


def _dot(a, b):
    return jnp.dot(a, b, preferred_element_type=F32)


def _dot_nt(a, b, precision=None):
    return lax.dot_general(a, b, (((1,), (1,)), ((), ())), preferred_element_type=F32, precision=precision)


def _dot_tn(a, b):
    return lax.dot_general(a, b, (((0,), (0,)), ((), ())), preferred_element_type=F32)


def _ada_kernel(c_ref, w_ref, b_ref, o_ref):
    s = _silu(c_ref[...])
    o_ref[0] = jnp.dot(s, w_ref[0], preferred_element_type=F32, precision=lax.Precision.HIGHEST) + b_ref[0]


def _ada_call(cond, ada_w, ada_b):
    depth, d, n = ada_w.shape
    rows = cond.shape[0]
    return pl.pallas_call(
        _ada_kernel,
        out_shape=jax.ShapeDtypeStruct((depth, rows, n), F32),
        grid=(depth, n // ADA_TN),
        in_specs=[
            pl.BlockSpec((rows, d), lambda i, j: (0, 0)),
            pl.BlockSpec((1, d, ADA_TN), lambda i, j: (i, 0, j)),
            pl.BlockSpec((1, 1, ADA_TN), lambda i, j: (i, 0, j)),
        ],
        out_specs=pl.BlockSpec((1, rows, ADA_TN), lambda i, j: (i, 0, j)),
        compiler_params=_cparams(("parallel", "parallel")),
        name="ada_mod",
    )(cond, ada_w, ada_b.reshape(depth, 1, n))


def _qkv_kernel(x_ref, mod_ref, wqk_ref, wvt_ref, cos_ref, sa_ref, sb_ref, q_ref, k_ref, vt_ref, *, d):
    mod = mod_ref[0]
    h = (x_ref[0] * (1.0 + mod[:, d:2 * d]) + mod[:, 0:d]).astype(BF16)
    qk = _dot(h, wqk_ref[...])
    vt_ref[0, 0] = _dot_nt(wvt_ref[...], h).astype(BF16)
    cos, sa, sb = cos_ref[...], sa_ref[...], sb_ref[...]
    q_scale = DIFF_HEAD_DIM ** -0.5 * LOG2E
    for s in range(2 * d // LANES):
        z = qk[:, s * LANES:(s + 1) * LANES]
        zr = z * cos + pltpu.roll(z, LANES - ROT_FREQS, axis=1) * sa + pltpu.roll(z, ROT_FREQS, axis=1) * sb
        if s < d // LANES:
            q_ref[0, :, s * LANES:(s + 1) * LANES] = (zr * q_scale).astype(BF16)
        else:
            k_ref[0, :, (s * LANES - d):(s * LANES - d + LANES)] = zr.astype(BF16)


def _rope_tables(s_len, c_len):
    rows = s_len // GRID_W
    row = np.repeat(np.arange(rows, dtype=np.float32), GRID_W)
    col = np.tile(np.arange(GRID_W, dtype=np.float32), rows)
    inv_freq = np.float32(ROPE_BASE) ** (-np.arange(ROT_FREQS, dtype=np.float32) / np.float32(ROT_FREQS))
    ang_r = row[:, None] * inv_freq
    ang_c = col[:, None] * inv_freq
    ang = np.concatenate([ang_r, ang_r, ang_c, ang_c], axis=-1).astype(np.float64)
    cos, sin = np.cos(ang), np.sin(ang)
    first_half = (np.arange(DIFF_HEAD_DIM) // ROT_FREQS) % 2 == 0
    sa = np.where(first_half[None, :], -sin, 0.0)
    sb = np.where(first_half[None, :], 0.0, sin)
    pad = lambda t, fill: np.concatenate([t, np.full((c_len, DIFF_HEAD_DIM), fill)], axis=0)
    two = lambda t: np.concatenate([t, t], axis=-1).astype(np.float32)
    return two(pad(cos, 1.0)), two(pad(sa, 0.0)), two(pad(sb, 0.0))


def _qkv_call(xs, mod, wqk, wvt, tables, n_batch, ctx_row):
    b, t, d = xs.shape
    nt = t // TM
    n_lat = nt - 1
    mod_idx = lambda bi, j: (jnp.where(j < n_lat, bi, ctx_row), 0, 0)
    return pl.pallas_call(
        functools.partial(_qkv_kernel, d=d),
        out_shape=(jax.ShapeDtypeStruct((b, t, d), BF16), jax.ShapeDtypeStruct((b, t, d), BF16),
                   jax.ShapeDtypeStruct((b, nt, d, TM), BF16)),
        grid=(b, nt),
        in_specs=[
            pl.BlockSpec((1, TM, d), lambda bi, j: (bi, j, 0)),
            pl.BlockSpec((1, 1, 6 * d), mod_idx),
            pl.BlockSpec((d, 2 * d), lambda bi, j: (0, 0)),
            pl.BlockSpec((d, d), lambda bi, j: (0, 0)),
            pl.BlockSpec((TM, LANES), lambda bi, j: (j, 0)),
            pl.BlockSpec((TM, LANES), lambda bi, j: (j, 0)),
            pl.BlockSpec((TM, LANES), lambda bi, j: (j, 0)),
        ],
        out_specs=(pl.BlockSpec((1, TM, d), lambda bi, j: (bi, j, 0)),
                   pl.BlockSpec((1, TM, d), lambda bi, j: (bi, j, 0)),
                   pl.BlockSpec((1, 1, d, TM), lambda bi, j: (bi, j, 0, 0))),
        compiler_params=_cparams(("parallel", "parallel")),
        name="qkv_rope",
    )(xs, mod, wqk, wvt, *tables)


def _attn_kernel(q_ref, k_ref, vt_ref, lam_ref, g_ref, o_ref, *, chunks, lambda_init, tq):
    q = q_ref[0]
    lane = lax.broadcasted_iota(I32, q.shape, 1)
    zero = jnp.zeros_like(q)
    qq = jnp.concatenate([jnp.where(lane < DIFF_HEAD_DIM, q, zero),
                          jnp.where(lane >= DIFF_HEAD_DIM, q, zero)], axis=0)
    m = jnp.full((1, 2 * tq), NEG_BIG, F32)
    l = jnp.zeros((1, 2 * tq), F32)
    acc = jnp.zeros((LANES, 2 * tq), F32)
    scores = lambda c: _dot_nt(k_ref[0, c * TK:(c + 1) * TK, :], qq)
    st_next = scores(chunks[0])
    for i, c in enumerate(chunks):
        st = st_next
        if i + 1 < len(chunks):
            st_next = scores(chunks[i + 1])
        m_new = jnp.maximum(m, jnp.max(st, axis=0, keepdims=True))
        alpha = jnp.exp2(m - m_new)
        p = jnp.exp2(st - m_new)
        l = l * alpha + jnp.sum(p, axis=0, keepdims=True)
        acc = acc * alpha + _dot(vt_ref[0, c], p.astype(BF16))
        m = m_new
    acc = acc / l
    lv = lam_ref[...]
    lam = (jnp.exp(jnp.sum(lv[0:1] * lv[1:2], axis=1, keepdims=True))
           - jnp.exp(jnp.sum(lv[2:3] * lv[3:4], axis=1, keepdims=True)) + lambda_init)
    ot = acc[:, :tq] - lam * acc[:, tq:]
    ms = jnp.mean(ot * ot, axis=0, keepdims=True)
    ot = ot * lax.rsqrt(ms + LN_EPS) * g_ref[...] * (1.0 - lambda_init)
    o_ref[0] = ot.T.astype(BF16)


def _attn_call(q, k, vt, lam_vecs, subln_g, lambda_init, *, tq, q_row0, n_rows, chunks, name):
    b, t, d = q.shape
    nt = t // TM
    q0 = q_row0 // tq
    return pl.pallas_call(
        functools.partial(_attn_kernel, chunks=chunks, lambda_init=lambda_init, tq=tq),
        out_shape=jax.ShapeDtypeStruct((b, n_rows, d), BF16),
        grid=(b, DIFF_HEADS, n_rows // tq),
        in_specs=[
            pl.BlockSpec((1, tq, LANES), lambda bi, h, qi: (bi, q0 + qi, h)),
            pl.BlockSpec((1, t, LANES), lambda bi, h, qi: (bi, 0, h)),
            pl.BlockSpec((1, nt, LANES, TM), lambda bi, h, qi: (bi, 0, h, 0)),
            pl.BlockSpec((4, DIFF_HEAD_DIM), lambda bi, h, qi: (0, 0)),
            pl.BlockSpec((LANES, 1), lambda bi, h, qi: (0, 0)),
        ],
        out_specs=pl.BlockSpec((1, tq, LANES), lambda bi, h, qi: (bi, qi, h)),
        compiler_params=_cparams(("parallel", "parallel", "parallel")),
        name=name,
    )(q, k, vt, lam_vecs, subln_g.reshape(LANES, 1))


def _route(logits_t, rb, ri_ref, rw_ref, cnt_ref, first):
    tm = logits_t.shape[1]
    aff = 1.0 / (1.0 + jnp.exp(-logits_t))
    sel = aff + rb
    row = lambda a, e: a[e:e + 1, :]
    gscore = []
    for g in range(N_GROUPS):
        a, b, c, d = (row(sel, g * EXPERTS_PER_GROUP + j) for j in range(EXPERTS_PER_GROUP))
        hi1, lo1, hi2, lo2 = jnp.maximum(a, b), jnp.minimum(a, b), jnp.maximum(c, d), jnp.minimum(c, d)
        gscore.append(jnp.maximum(hi1, hi2) + jnp.maximum(jnp.minimum(hi1, hi2), jnp.maximum(lo1, lo2)))
    best, grp = gscore[0], jnp.zeros((1, tm), I32)
    for g in range(1, N_GROUPS):
        better = gscore[g] > best
        best = jnp.where(better, gscore[g], best)
        grp = jnp.where(better, g, grp)
    in_sel, in_aff = [], []
    for j in range(EXPERTS_PER_GROUP):
        s_j, a_j = row(sel, j), row(aff, j)
        for g in range(1, N_GROUPS):
            s_j = jnp.where(grp == g, row(sel, g * EXPERTS_PER_GROUP + j), s_j)
            a_j = jnp.where(grp == g, row(aff, g * EXPERTS_PER_GROUP + j), a_j)
        in_sel.append(s_j)
        in_aff.append(a_j)
    v0, i0, a0 = in_sel[0], jnp.zeros((1, tm), I32), in_aff[0]
    for j in range(1, EXPERTS_PER_GROUP):
        better = in_sel[j] > v0
        v0 = jnp.where(better, in_sel[j], v0)
        i0 = jnp.where(better, j, i0)
        a0 = jnp.where(better, in_aff[j], a0)
    v1, i1, a1 = jnp.full((1, tm), -jnp.inf, F32), jnp.zeros((1, tm), I32), jnp.zeros((1, tm), F32)
    for j in range(EXPERTS_PER_GROUP):
        better = (in_sel[j] > v1) & (i0 != j)
        v1 = jnp.where(better, in_sel[j], v1)
        i1 = jnp.where(better, j, i1)
        a1 = jnp.where(better, in_aff[j], a1)
    e0 = grp * EXPERTS_PER_GROUP + i0
    e1 = grp * EXPERTS_PER_GROUP + i1
    wsum = a0 + a1
    erow = lax.broadcasted_iota(I32, (N_EXPERTS, tm), 0)
    oh0 = (erow == e0).astype(F32)
    oh1 = (erow == e1).astype(F32)
    both = oh0 + oh1
    both_bf = both.astype(BF16)
    tri = (lax.broadcasted_iota(I32, (tm, tm), 0) < lax.broadcasted_iota(I32, (tm, tm), 1)).astype(BF16)
    before = _dot(both_bf, tri)
    round_up = lambda n: jnp.floor((n + (ROW_GROUP - 1.0)) * (1.0 / ROW_GROUP)) * ROW_GROUP
    run_col = round_up(jnp.sum(both, axis=1, keepdims=True))
    e_r = lax.broadcasted_iota(I32, (N_EXPERTS, N_EXPERTS), 0)
    e_c = lax.broadcasted_iota(I32, (N_EXPERTS, N_EXPERTS), 1)
    off_col = _dot((e_c < e_r).astype(BF16), jnp.broadcast_to(run_col, (N_EXPERTS, LANES)).astype(BF16))[:, 0:1]
    base = off_col + before
    s0 = jnp.sum(oh0 * base, axis=0, keepdims=True)
    s1 = jnp.sum(oh1 * base, axis=0, keepdims=True)
    both_pad = jnp.concatenate([both_bf, jnp.zeros((LANES - N_EXPERTS, tm), BF16)], axis=0)
    run_row = round_up(_dot_nt(jnp.ones((8, tm), BF16), both_pad))
    l_r = lax.broadcasted_iota(I32, (LANES, LANES), 0)
    l_c = lax.broadcasted_iota(I32, (LANES, LANES), 1)
    off_row = _dot(run_row.astype(BF16), (l_r < l_c).astype(BF16))
    lane = lax.broadcasted_iota(I32, (1, LANES), 1)
    groups = run_row[0:1] * (1.0 / ROW_GROUP)
    groups = jnp.where(lane == N_EXPERTS, jnp.sum(groups, axis=1, keepdims=True), groups)

    @pl.when(first)
    def _():
        cnt_ref[...] = jnp.zeros_like(cnt_ref)

    zero_half = jnp.zeros((1, tm - LANES), I32)
    ri_ref[0, 0:1, :] = s0.astype(I32)
    ri_ref[0, 1:2, :] = s1.astype(I32)
    ri_ref[0, 2:3, :] = e0
    ri_ref[0, 3:4, :] = e1
    for r, v in ((4, groups), (5, cnt_ref[0:1, :]), (6, off_row[0:1])):
        ri_ref[0, r:r + 1, 0:LANES] = v.astype(I32)
        ri_ref[0, r:r + 1, LANES:] = zero_half
    ri_ref[0, 7:8, :] = jnp.zeros((1, tm), I32)
    cnt_ref[...] = cnt_ref[...] + run_row
    rw_ref[0, 0:1, :] = a0 / wsum
    rw_ref[0, 1:2, :] = a1 / wsum
    rw_ref[0, 2:8, :] = jnp.zeros((6, tm), F32)


def _post_mixer(y, x, mod, lng, lnb, rwt, rb, x1_ref, ri_ref, rw_ref, cnt_ref, first, d):
    x1 = _layer_norm(ALPHA * x + mod[:, 2 * d:3 * d] * y, lng, lnb)
    x1_ref[0] = x1
    tok = x1 * (1.0 + mod[:, 4 * d:5 * d]) + mod[:, 3 * d:4 * d]
    logits_t = _dot_nt(rwt, tok, precision=lax.Precision.HIGHEST)
    _route(logits_t, rb, ri_ref, rw_ref, cnt_ref, first)


def _attn_out_kernel(ol_ref, oc_ref, x_ref, mod_ref, wo_ref, lng_ref, lnb_ref, rwt_ref, rb_ref,
                     x1_ref, ri_ref, rw_ref, cnt_ref, *, d, n_lat):
    first = (pl.program_id(0) == 0) & (pl.program_id(1) == 0)
    o = jnp.where(pl.program_id(1) < n_lat, ol_ref[0], oc_ref[0])
    y = _dot(o, wo_ref[...])
    _post_mixer(y, x_ref[0], mod_ref[0], lng_ref[...], lnb_ref[...], rwt_ref[...], rb_ref[...],
                x1_ref, ri_ref, rw_ref, cnt_ref, first, d)


def _route_out_shapes(b, n_tiles, t_out, d):
    return (jax.ShapeDtypeStruct((b, t_out, d), F32),
            jax.ShapeDtypeStruct((b * n_tiles, 8, TM), I32),
            jax.ShapeDtypeStruct((b * n_tiles, 8, TM), F32),
            jax.ShapeDtypeStruct((8, LANES), F32))


def _route_out_specs(n_tiles, d):
    return (pl.BlockSpec((1, TM, d), lambda bi, j: (bi, j, 0)),
            pl.BlockSpec((1, 8, TM), lambda bi, j: (bi * n_tiles + j, 0, 0)),
            pl.BlockSpec((1, 8, TM), lambda bi, j: (bi * n_tiles + j, 0, 0)),
            pl.BlockSpec((8, LANES), lambda bi, j: (0, 0)))


def _attn_out_call(o_lat, o_ctx, xs, mod, wo, lng, lnb, rwt, rb, ctx_row):
    b, t, d = xs.shape
    nt = t // TM
    n_lat = nt - 1
    row = lambda a: pl.BlockSpec(a.shape, lambda bi, j: (0,) * a.ndim)
    return pl.pallas_call(
        functools.partial(_attn_out_kernel, d=d, n_lat=n_lat),
        out_shape=_route_out_shapes(b, nt, t, d),
        grid=(b, nt),
        in_specs=[
            pl.BlockSpec((1, TM, d), lambda bi, j: (bi, jnp.minimum(j, n_lat - 1), 0)),
            pl.BlockSpec((1, TM, d), lambda bi, j: (bi, 0, 0)),
            pl.BlockSpec((1, TM, d), lambda bi, j: (bi, j, 0)),
            pl.BlockSpec((1, 1, 6 * d), lambda bi, j: (jnp.where(j < n_lat, bi, ctx_row), 0, 0)),
            row(wo), row(lng), row(lnb), row(rwt), row(rb),
        ],
        out_specs=_route_out_specs(nt, d),
        compiler_params=_cparams(("arbitrary", "arbitrary")),
        name="attn_out_route",
    )(o_lat, o_ctx, xs, mod, wo, lng, lnb, rwt, rb)


def _group_copy(src, r_src, dst, r_dst, sem):
    return pltpu.make_async_copy(src.at[pl.ds(pl.multiple_of(r_src, ROW_GROUP), ROW_GROUP)],
                                 dst.at[pl.ds(pl.multiple_of(r_dst, ROW_GROUP), ROW_GROUP)], sem)


def _tab(tab_ref, tile, row, e):
    return tab_ref[(tile * TAB_ROWS + row) * TAB_LANES + e]


def _run_copies(tab_ref, pad_ref, tile, stage, hbm, sem, to_hbm):
    for e in range(N_EXPERTS):
        s0 = _tab(tab_ref, tile, 2, e)
        h0 = pad_ref[e] + _tab(tab_ref, tile, 1, e)

        def issue(g, carry):
            s, h = s0 + g * ROW_GROUP, h0 + g * ROW_GROUP
            (_group_copy(stage, s, hbm, h, sem) if to_hbm else _group_copy(hbm, h, stage, s, sem)).start()
            return carry

        lax.fori_loop(0, _tab(tab_ref, tile, 0, e), issue, 0)


def _run_waits(tab_ref, tile, stage, hbm, sem, to_hbm):
    def drain(g, carry):
        (_group_copy(stage, 0, hbm, 0, sem) if to_hbm else _group_copy(hbm, 0, stage, 0, sem)).wait()
        return carry

    lax.fori_loop(0, _tab(tab_ref, tile, 0, N_EXPERTS), drain, 0)


def _dispatch_kernel(tab_ref, pad_ref, fill_ref, x_ref, mod_ref, ri_ref, xs_ref, stage_ref, zero_ref, sems, sem_z,
                     *, d, n_steps):
    step = pl.program_id(0) * pl.num_programs(1) + pl.program_id(1)
    buf = step % 2
    mod = mod_ref[0]
    tok = (x_ref[0] * (1.0 + mod[:, 4 * d:5 * d]) + mod[:, 3 * d:4 * d]).astype(BF16)
    slot = lax.broadcasted_iota(I32, (STAGE_ROWS, TM), 0)
    ri = ri_ref[0]
    pick = jnp.where((slot == ri[0:1, :]) | (slot == ri[1:2, :]), 1.0, 0.0).astype(BF16)
    for bsel in range(2):
        @pl.when(buf == bsel)
        def _():
            stage_ref[bsel] = _dot(pick, tok)
            _run_copies(tab_ref, pad_ref, step, stage_ref.at[bsel], xs_ref, sems.at[bsel], True)

    for bsel in range(2):
        @pl.when((buf != bsel) & (step > 0))
        def _():
            _run_waits(tab_ref, step - 1, stage_ref.at[bsel], xs_ref, sems.at[bsel], True)

    @pl.when(step == n_steps - 1)
    def _():
        for bsel in range(2):
            @pl.when(buf == bsel)
            def _():
                _run_waits(tab_ref, step, stage_ref.at[bsel], xs_ref, sems.at[bsel], True)

        zero_ref[...] = jnp.zeros_like(zero_ref)
        for e in range(N_EXPERTS + 1):
            lo, hi = fill_ref[2 * e], fill_ref[2 * e + 1]

            def fill(g, carry):
                _group_copy(zero_ref, 0, xs_ref, g * ROW_GROUP, sem_z).start()
                return carry

            def fill_wait(g, carry):
                _group_copy(zero_ref, 0, xs_ref, 0, sem_z).wait()
                return carry

            lax.fori_loop(lo, hi, fill, 0)
            lax.fori_loop(lo, hi, fill_wait, 0)


def _dispatch_call(tab, pad_start, fill, x1, mod, ri, n_tiles, n_pad, ctx_row, n_lat):
    b, t, d = x1.shape
    grid_spec = pltpu.PrefetchScalarGridSpec(
        num_scalar_prefetch=3,
        grid=(b, n_tiles),
        in_specs=[
            pl.BlockSpec((1, TM, d), lambda bi, j, *_: (bi, j, 0)),
            pl.BlockSpec((1, 1, 6 * d), lambda bi, j, *_: (jnp.where(j < n_lat, bi, ctx_row), 0, 0)),
            pl.BlockSpec((1, 8, TM), lambda bi, j, *_: (bi * n_tiles + j, 0, 0)),
        ],
        out_specs=pl.BlockSpec(memory_space=pl.ANY),
        scratch_shapes=[pltpu.VMEM((2, STAGE_ROWS, d), F32), pltpu.VMEM((ROW_GROUP, d), F32),
                        pltpu.SemaphoreType.DMA((2,)), pltpu.SemaphoreType.DMA],
    )
    return pl.pallas_call(
        functools.partial(_dispatch_kernel, d=d, n_steps=b * n_tiles),
        out_shape=jax.ShapeDtypeStruct((n_pad, d), F32),
        grid_spec=grid_spec,
        compiler_params=_cparams(("arbitrary", "arbitrary")),
        name="moe_dispatch",
    )(tab, pad_start, fill, x1, mod, ri)


def _ffn_kernel(be_ref, nu_ref, xs_ref, wg_ref, wu_ref, wd_ref, ys_ref, wg_s, wu_s, wd_s):
    i = pl.program_id(0)

    @pl.when(i < nu_ref[0])
    def _():
        prev = be_ref[jnp.maximum(i - 1, 0)]

        @pl.when((i == 0) | (be_ref[i] != prev))
        def _():
            wg_s[...] = wg_ref[0].astype(BF16)
            wu_s[...] = wu_ref[0].astype(BF16)
            wd_s[...] = wd_ref[0].astype(BF16)

        x = xs_ref[...].astype(BF16)
        g = _dot(x, wg_s[...])
        u = _dot(x, wu_s[...])
        ys_ref[...] = _dot((_silu(g) * u).astype(BF16), wd_s[...])

    @pl.when(i >= nu_ref[0])
    def _():
        ys_ref[...] = jnp.zeros_like(ys_ref)


def _ffn_call(block_expert, n_used, xs, w_gate, w_up, w_down):
    n_pad, d = xs.shape
    de = w_gate.shape[-1]
    n_blocks = n_pad // MOE_BLK
    blk = lambda i, be, nu: (jnp.minimum(i, nu[0] - 1), 0)
    exp = lambda i, be, nu: (be[jnp.minimum(i, nu[0] - 1)], 0, 0)
    grid_spec = pltpu.PrefetchScalarGridSpec(
        num_scalar_prefetch=2,
        grid=(n_blocks,),
        in_specs=[
            pl.BlockSpec((MOE_BLK, d), blk),
            pl.BlockSpec((1, d, de), exp),
            pl.BlockSpec((1, d, de), exp),
            pl.BlockSpec((1, de, d), exp),
        ],
        out_specs=pl.BlockSpec((MOE_BLK, d), lambda i, be, nu: (i, 0)),
        scratch_shapes=[pltpu.VMEM((d, de), BF16), pltpu.VMEM((d, de), BF16), pltpu.VMEM((de, d), BF16)],
    )
    return pl.pallas_call(
        _ffn_kernel,
        out_shape=jax.ShapeDtypeStruct((n_pad, d), F32),
        grid_spec=grid_spec,
        compiler_params=_cparams(("arbitrary",)),
        name="moe_ffn",
    )(block_expert, n_used, xs, w_gate, w_up, w_down)


def _combine_kernel(tab_ref, pad_ref, x_ref, mod_ref, rit_ref, rw_ref, lng_ref, lnb_ref, ys_ref, o_ref,
                    stage_ref, sems, *, d, n_steps):
    step = pl.program_id(0) * pl.num_programs(1) + pl.program_id(1)
    buf = step % 2

    @pl.when(step == 0)
    def _():
        stage_ref[...] = jnp.zeros_like(stage_ref)
        _run_copies(tab_ref, pad_ref, step, stage_ref.at[0], ys_ref, sems.at[0], False)

    for bsel in range(2):
        @pl.when((buf != bsel) & (step + 1 < n_steps))
        def _():
            _run_copies(tab_ref, pad_ref, step + 1, stage_ref.at[bsel], ys_ref, sems.at[bsel], False)

    rit = rit_ref[0]
    lane = lax.broadcasted_iota(I32, (TM, STAGE_ROWS), 1)
    pick0 = jnp.where(lane == rit[:, 0:1], 1.0, 0.0).astype(BF16)
    pick1 = jnp.where(lane == rit[:, 1:2], 1.0, 0.0).astype(BF16)
    w = rw_ref[0]
    mod = mod_ref[0]
    for bsel in range(2):
        @pl.when(buf == bsel)
        def _():
            _run_waits(tab_ref, step, stage_ref.at[bsel], ys_ref, sems.at[bsel], False)
            stage = stage_ref[bsel].astype(BF16)
            moe = w[:, 0:1] * _dot(pick0, stage) + w[:, 1:2] * _dot(pick1, stage)
            o_ref[0] = _layer_norm(ALPHA * x_ref[0] + mod[:, 5 * d:6 * d] * moe, lng_ref[...], lnb_ref[...])


def _combine_call(tab, pad_start, x1, mod, ri_t, rw_t, lng, lnb, ys, n_tiles, ctx_row, n_lat):
    b, _, d = x1.shape
    row = lambda a: pl.BlockSpec(a.shape, lambda bi, j, *_: (0,) * a.ndim)
    grid_spec = pltpu.PrefetchScalarGridSpec(
        num_scalar_prefetch=2,
        grid=(b, n_tiles),
        in_specs=[
            pl.BlockSpec((1, TM, d), lambda bi, j, *_: (bi, j, 0)),
            pl.BlockSpec((1, 1, 6 * d), lambda bi, j, *_: (jnp.where(j < n_lat, bi, ctx_row), 0, 0)),
            pl.BlockSpec((1, TM, 8), lambda bi, j, *_: (bi * n_tiles + j, 0, 0)),
            pl.BlockSpec((1, TM, 8), lambda bi, j, *_: (bi * n_tiles + j, 0, 0)),
            row(lng), row(lnb),
            pl.BlockSpec(memory_space=pl.ANY),
        ],
        out_specs=pl.BlockSpec((1, TM, d), lambda bi, j, *_: (bi, j, 0)),
        scratch_shapes=[pltpu.VMEM((2, STAGE_ROWS, d), F32), pltpu.SemaphoreType.DMA((2,))],
    )
    return pl.pallas_call(
        functools.partial(_combine_kernel, d=d, n_steps=b * n_tiles),
        out_shape=jax.ShapeDtypeStruct((b, n_tiles * TM, d), F32),
        grid_spec=grid_spec,
        compiler_params=_cparams(("arbitrary", "arbitrary")),
        name="moe_combine",
    )(tab, pad_start, x1, mod, ri_t, rw_t, lng, lnb, ys)


def _moe(x1, mod, ri, rw, counts, lng, lnb, w_gate, w_up, w_down, n_tiles, ctx_row, n_lat):
    b, _, d = x1.shape
    n_steps = b * n_tiles
    n_rows_max = 2 * n_steps * TM + (ROW_GROUP - 1) * N_EXPERTS * n_steps
    n_pad = -(-n_rows_max // MOE_BLK) * MOE_BLK + N_EXPERTS * MOE_BLK
    n_blocks = n_pad // MOE_BLK
    cnt = counts[0, :N_EXPERTS].astype(I32)
    padded = (cnt + MOE_BLK - 1) // MOE_BLK * MOE_BLK
    pad_end = jnp.cumsum(padded)
    pad_start = jnp.concatenate([jnp.zeros((1,), I32), pad_end]).astype(I32)
    block_row0 = jnp.arange(n_blocks, dtype=I32) * MOE_BLK
    block_expert = jnp.minimum(
        jnp.sum((pad_end[None, :] <= block_row0[:, None]).astype(I32), axis=1), N_EXPERTS - 1).astype(I32)
    n_used = (pad_end[-1:] // MOE_BLK).astype(I32)
    fill_lo = jnp.concatenate([pad_start[:-1] + cnt, pad_end[-1:]]) // ROW_GROUP
    fill_hi = jnp.concatenate([pad_end, jnp.full((1,), n_pad, I32)]) // ROW_GROUP
    fill = jnp.stack([fill_lo, fill_hi], axis=1).reshape(-1).astype(I32)
    tab = ri[:, 4:4 + TAB_ROWS, :TAB_LANES].reshape(-1)
    xs = _dispatch_call(tab, pad_start, fill, x1, mod, ri, n_tiles, n_pad, ctx_row, n_lat)
    ys = _ffn_call(block_expert, n_used, xs, w_gate, w_up, w_down)
    ri_t, rw_t = jnp.swapaxes(ri, 1, 2), jnp.swapaxes(rw, 1, 2)
    return _combine_call(tab, pad_start, x1, mod, ri_t, rw_t, lng, lnb, ys, n_tiles, ctx_row, n_lat)


def _gla_proj_kernel(x_ref, mod_ref, win_ref, wgd_ref, wgu_ref, bg_ref, q_ref, k_ref, v_ref, r_ref, g_ref, *, d):
    mod = mod_ref[0]
    h = (x_ref[0] * (1.0 + mod[:, d:2 * d]) + mod[:, 0:d]).astype(BF16)
    dk = GLA_HEADS * GLA_DK
    p = _dot(h, win_ref[...])
    q_ref[0] = (p[:, 0:dk] * GLA_DK ** -0.5).astype(BF16)
    k_ref[0] = p[:, dk:2 * dk].astype(BF16)
    v_ref[0] = p[:, 2 * dk:2 * dk + d].astype(BF16)
    r_ref[0] = p[:, 2 * dk + d:].astype(BF16)
    low = _dot(h, wgd_ref[...]).astype(BF16)
    pre = _dot(low, wgu_ref[...]) + bg_ref[...]
    g_ref[0] = (jnp.minimum(pre, 0.0) - jnp.log1p(jnp.exp(-jnp.abs(pre)))) * (1.0 / GLA_TAU)


def _gla_proj_call(xs, mod, win, wgd, wgu, bg, ctx_row):
    b, t, d = xs.shape
    nt = t // TM
    n_lat = nt - 1
    dk = GLA_HEADS * GLA_DK
    row = lambda a: pl.BlockSpec(a.shape, lambda bi, j: (0,) * a.ndim)
    tile = lambda n: pl.BlockSpec((1, TM, n), lambda bi, j: (bi, j, 0))
    return pl.pallas_call(
        functools.partial(_gla_proj_kernel, d=d),
        out_shape=(jax.ShapeDtypeStruct((b, t, dk), BF16), jax.ShapeDtypeStruct((b, t, dk), BF16),
                   jax.ShapeDtypeStruct((b, t, d), BF16), jax.ShapeDtypeStruct((b, t, d), BF16),
                   jax.ShapeDtypeStruct((b, t, 2 * dk), F32)),
        grid=(b, nt),
        in_specs=[tile(d), pl.BlockSpec((1, 1, 6 * d), lambda bi, j: (jnp.where(j < n_lat, bi, ctx_row), 0, 0)),
                  row(win), row(wgd), row(wgu), row(bg)],
        out_specs=(tile(dk), tile(dk), tile(d), tile(d), tile(2 * dk)),
        compiler_params=_cparams(("parallel", "parallel")),
        name="gla_proj",
    )(xs, mod, win, wgd, wgu, bg)


def _split3(x):
    hi = x.astype(BF16)
    r1 = x - hi.astype(F32)
    mid = r1.astype(BF16)
    lo = (r1 - mid.astype(F32)).astype(BF16)
    return hi, mid, lo


def _gla_direction(q_ref, k_ref, v_ref, g_ref, o_ref, state_ref, reverse):
    L = GLA_CHUNK
    ri = lax.broadcasted_iota(I32, (L, L), 0)
    ci = lax.broadcasted_iota(I32, (L, L), 1)
    keep = (ci >= ri) if reverse else (ci <= ri)
    tri = keep.astype(BF16)
    chunks = range(TM // L)
    for c in (reversed(chunks) if reverse else chunks):
        rows = slice(c * L, (c + 1) * L)
        g = g_ref[0, rows, :]
        b_all = sum(_dot(tri, part) for part in _split3(g))
        edge = 0 if reverse else L - 1
        for h in range(GLA_HEADS):
            kcols = slice(h * GLA_DK, (h + 1) * GLA_DK)
            vcols = slice(h * GLA_DV, (h + 1) * GLA_DV)
            bh = b_all[:, kcols]
            b_last = bh[edge:edge + 1, :]
            q = q_ref[0, rows, kcols].astype(F32)
            k = k_ref[0, rows, kcols].astype(F32)
            v = v_ref[0, rows, vcols]
            qd = (q * jnp.exp(bh)).astype(BF16)
            kd = (k * jnp.exp(-bh)).astype(BF16)
            a = jnp.where(keep, _dot_nt(qd, kd), 0.0).astype(BF16)
            state_t = state_ref[h]
            o_ref[0, rows, vcols] = _dot(a, v) + _dot_nt(qd, state_t.astype(BF16))
            kl = (k * jnp.exp(b_last - bh)).astype(BF16)
            state_ref[h] = jnp.exp(b_last) * state_t + _dot_tn(v, kl)


def _gla_scan_kernel(qf, kf, vf, gf, qb, kb, vb, gb, of_ref, ob_ref, sf_ref, sb_ref):
    @pl.when(pl.program_id(1) == 0)
    def _():
        sf_ref[...] = jnp.zeros_like(sf_ref)
        sb_ref[...] = jnp.zeros_like(sb_ref)

    _gla_direction(qf, kf, vf, gf, of_ref, sf_ref, False)
    _gla_direction(qb, kb, vb, gb, ob_ref, sb_ref, True)


def _gla_scan_call(q, k, v, g):
    b, t, d = v.shape
    nt = t // TM
    n_lat = nt - 1
    dk = GLA_HEADS * GLA_DK
    fwd = lambda j: jnp.where(j == 0, n_lat, j - 1)
    bwd = lambda j: jnp.where(j == 0, n_lat, n_lat - j)
    spec = lambda n, idx, col: pl.BlockSpec((1, TM, n), lambda bi, j: (bi, idx(j), col))
    return pl.pallas_call(
        _gla_scan_kernel,
        out_shape=(jax.ShapeDtypeStruct((b, n_lat * TM, d), F32), jax.ShapeDtypeStruct((b, n_lat * TM, d), F32)),
        grid=(b, nt),
        in_specs=[spec(dk, fwd, 0), spec(dk, fwd, 0), spec(d, fwd, 0), spec(dk, fwd, 0),
                  spec(dk, bwd, 0), spec(dk, bwd, 0), spec(d, bwd, 0), spec(dk, bwd, 1)],
        out_specs=(pl.BlockSpec((1, TM, d), lambda bi, j: (bi, jnp.maximum(j - 1, 0), 0)),
                   pl.BlockSpec((1, TM, d), lambda bi, j: (bi, n_lat - jnp.maximum(j, 1), 0))),
        scratch_shapes=[pltpu.VMEM((GLA_HEADS, GLA_DV, GLA_DK), F32), pltpu.VMEM((GLA_HEADS, GLA_DV, GLA_DK), F32)],
        compiler_params=_cparams(("parallel", "arbitrary")),
        name="gla_scan",
    )(q, k, v, g, q, k, v, g)


def _gla_out_kernel(of_ref, ob_ref, r_ref, x_ref, mod_ref, ng_ref, wo_ref, lng_ref, lnb_ref, rwt_ref, rb_ref,
                    x1_ref, ri_ref, rw_ref, cnt_ref, *, d):
    first = (pl.program_id(0) == 0) & (pl.program_id(1) == 0)
    o = of_ref[0] + ob_ref[0]
    r = r_ref[0].astype(F32)
    gate = _silu(r)
    ng = ng_ref[...]
    parts = []
    for h in range(GLA_HEADS):
        cols = slice(h * GLA_DV, (h + 1) * GLA_DV)
        oh = o[:, cols]
        ms = jnp.mean(oh * oh, axis=-1, keepdims=True)
        parts.append((oh * lax.rsqrt(ms + LN_EPS) * ng * gate[:, cols]).astype(BF16))
    y = _dot(jnp.concatenate(parts, axis=-1), wo_ref[...])
    _post_mixer(y, x_ref[0], mod_ref[0], lng_ref[...], lnb_ref[...], rwt_ref[...], rb_ref[...],
                x1_ref, ri_ref, rw_ref, cnt_ref, first, d)


def _gla_out_call(o_f, o_b, r, xs, mod, ng, wo, lng, lnb, rwt, rb):
    b, s_len, d = o_f.shape
    n_lat = s_len // TM
    row = lambda a: pl.BlockSpec(a.shape, lambda bi, j: (0,) * a.ndim)
    tile = pl.BlockSpec((1, TM, d), lambda bi, j: (bi, j, 0))
    return pl.pallas_call(
        functools.partial(_gla_out_kernel, d=d),
        out_shape=_route_out_shapes(b, n_lat, s_len, d),
        grid=(b, n_lat),
        in_specs=[tile, tile, tile, tile, pl.BlockSpec((1, 1, 6 * d), lambda bi, j: (bi, 0, 0)),
                  row(ng), row(wo), row(lng), row(lnb), row(rwt), row(rb)],
        out_specs=_route_out_specs(n_lat, d),
        compiler_params=_cparams(("arbitrary", "arbitrary")),
        name="gla_out_route",
    )(o_f, o_b, r, xs, mod, ng, wo, lng, lnb, rwt, rb)


def kernel(x, c, ctx, c_ctx, ada_w, ada_b, ln_g, ln_b, attn_w_qkv, attn_w_o, attn_lambda, attn_subln_g, gla_w_in, gla_wg_down, gla_wg_up, gla_bg, gla_norm_g, gla_w_o, router_w, router_b, moe_w_gate, moe_w_up, moe_w_down):
    b, s_len, d = x.shape
    c_len = ctx.shape[1]
    assert ada_w.shape[0] == DEPTH and c_len == TM and s_len % TQ == 0 and TQ % TM == 0 and s_len % GRID_W == 0
    assert d == DIFF_HEADS * 2 * DIFF_HEAD_DIM == GLA_HEADS * GLA_DV
    assert STAGE_ROWS % LANES == 0 and STAGE_ROWS >= 2 * TM + (ROW_GROUP - 1) * N_EXPERTS
    n_lat = s_len // TM
    nt = n_lat + 1
    ctx_row = b

    cond_rows = -(-(b + 1) // 8) * 8
    cond = jnp.concatenate([c, c_ctx[None, :], jnp.zeros((cond_rows - b - 1, d), F32)], axis=0)
    mods = _ada_call(cond, ada_w, ada_b)
    rwt = router_w.T
    rb = router_b.reshape(N_EXPERTS, 1)
    stream = jnp.concatenate([x, ctx], axis=1)

    mod0 = mods[0].reshape(cond_rows, 1, 6 * d)
    wqkv = attn_w_qkv[0]
    wqk = wqkv[:, :2 * d].astype(BF16)
    wvt = wqkv[:, 2 * d:].T.astype(BF16)
    tables = tuple(jnp.asarray(t) for t in _rope_tables(s_len, c_len))
    q, k, vt = _qkv_call(stream, mod0, wqk, wvt, tables, b, ctx_row)
    lambda_init = 0.8 - 0.6 * math.exp(-0.3 * 0)
    attn = functools.partial(_attn_call, q, k, vt, attn_lambda[0], attn_subln_g[0], lambda_init)
    o_lat = attn(tq=TQ, q_row0=0, n_rows=s_len, chunks=tuple(range(nt)), name="diff_attn_lat")
    o_ctx = attn(tq=c_len, q_row0=s_len, n_rows=c_len, chunks=(n_lat,), name="diff_attn_ctx")
    x1, ri, rw, counts = _attn_out_call(o_lat, o_ctx, stream, mod0, attn_w_o[0].astype(BF16), ln_g[0, 0][None], ln_b[0, 0][None],
                                        rwt, rb, ctx_row)
    stream = _moe(x1, mod0, ri, rw, counts, ln_g[0, 1][None], ln_b[0, 1][None],
                  moe_w_gate[0], moe_w_up[0], moe_w_down[0], nt, ctx_row, n_lat)

    mod1 = mods[1].reshape(cond_rows, 1, 6 * d)
    dk = GLA_HEADS * GLA_DK
    wgd = jnp.concatenate([gla_wg_down[0, 0], gla_wg_down[0, 1],
                           jnp.zeros((d, LANES - 2 * GLA_GATE_RANK), F32)], axis=1).astype(BF16)
    wgu = jnp.zeros((LANES, 2 * dk), F32)
    wgu = wgu.at[0:GLA_GATE_RANK, 0:dk].set(gla_wg_up[0, 0])
    wgu = wgu.at[GLA_GATE_RANK:2 * GLA_GATE_RANK, dk:].set(gla_wg_up[0, 1]).astype(BF16)
    bg = gla_bg[0].reshape(1, 2 * dk)
    gq, gk, gv, gr, gg = _gla_proj_call(stream, mod1, gla_w_in[0].astype(BF16), wgd, wgu, bg, ctx_row)
    o_f, o_b = _gla_scan_call(gq, gk, gv, gg)
    x1, ri, rw, counts = _gla_out_call(o_f, o_b, gr, stream, mod1, gla_norm_g[0][None], gla_w_o[0].astype(BF16),
                                       ln_g[1, 0][None], ln_b[1, 0][None], rwt, rb)
    return _moe(x1, mod1, ri, rw, counts, ln_g[1, 1][None], ln_b[1, 1][None],
                moe_w_gate[1], moe_w_up[1], moe_w_down[1], n_lat, ctx_row, n_lat)
```

```python
import functools
import math

import numpy as np
import jax
import jax.numpy as jnp
from jax import lax
from jax.experimental import pallas as pl
from jax.experimental.pallas import tpu as pltpu

F32 = jnp.float32
BF16 = jnp.bfloat16
I32 = jnp.int32

DEPTH = 2
ALPHA = (2.0 * DEPTH) ** 0.25
LN_EPS = 1e-5
GRID_W = 64
ROPE_BASE = 10000.0
DIFF_HEADS = 8
DIFF_HEAD_DIM = 64
ROT_FREQS = DIFF_HEAD_DIM // 4
GLA_HEADS = 4
GLA_DK = 128
GLA_DV = 256
GLA_GATE_RANK = 16
GLA_TAU = 16.0
GLA_CHUNK = 64
GLA_SUB = 16
N_EXPERTS = 16
N_GROUPS = 4
EXPERTS_PER_GROUP = 4
LOG2E = 1.4426950408889634

LANES = 128
TM = 256
TK = 256
TQ = 512
MOE_BLK = 256
ADA_TN = 1536
ROW_GROUP = 8
STAGE_ROWS = 640
TAB_ROWS, TAB_LANES = 3, 32
VMEM_LIMIT = 48 * 1024 * 1024
NEG_BIG = -1e30


def _cparams(sem):
    return pltpu.CompilerParams(dimension_semantics=sem, vmem_limit_bytes=VMEM_LIMIT)


def _silu(x):
    return x * (1.0 / (1.0 + jnp.exp(-x)))


def _layer_norm(z, g, b):
    mu = jnp.mean(z, axis=-1, keepdims=True)
    zc = z - mu
    var = jnp.mean(zc * zc, axis=-1, keepdims=True)
    return zc * lax.rsqrt(var + LN_EPS) * g + b


def _dot(a, b):
    return jnp.dot(a, b, preferred_element_type=F32)


def _dot_nt(a, b, precision=None):
    return lax.dot_general(a, b, (((1,), (1,)), ((), ())), preferred_element_type=F32, precision=precision)


def _dot_tn(a, b):
    return lax.dot_general(a, b, (((0,), (0,)), ((), ())), preferred_element_type=F32)


def _ada_kernel(c_ref, w_ref, b_ref, o_ref):
    s = _silu(c_ref[...])
    o_ref[0] = jnp.dot(s, w_ref[0], preferred_element_type=F32, precision=lax.Precision.HIGHEST) + b_ref[0]


def _ada_call(cond, ada_w, ada_b):
    depth, d, n = ada_w.shape
    rows = cond.shape[0]
    return pl.pallas_call(
        _ada_kernel,
        out_shape=jax.ShapeDtypeStruct((depth, rows, n), F32),
        grid=(depth, n // ADA_TN),
        in_specs=[
            pl.BlockSpec((rows, d), lambda i, j: (0, 0)),
            pl.BlockSpec((1, d, ADA_TN), lambda i, j: (i, 0, j)),
            pl.BlockSpec((1, 1, ADA_TN), lambda i, j: (i, 0, j)),
        ],
        out_specs=pl.BlockSpec((1, rows, ADA_TN), lambda i, j: (i, 0, j)),
        compiler_params=_cparams(("parallel", "parallel")),
        name="ada_mod",
    )(cond, ada_w, ada_b.reshape(depth, 1, n))


def _qkv_kernel(xl_ref, xc_ref, mod_ref, wqk_ref, wvt_ref, cos_ref, sa_ref, sb_ref, q_ref, k_ref, vt_ref, *, d, n_lat):
    mod = mod_ref[0]
    x = jnp.where(pl.program_id(1) < n_lat, xl_ref[0], xc_ref[0])
    h = (x * (1.0 + mod[:, d:2 * d]) + mod[:, 0:d]).astype(BF16)
    qk = _dot(h, wqk_ref[...])
    vt_ref[0, 0] = _dot_nt(wvt_ref[...], h).astype(BF16)
    cos, sa, sb = cos_ref[...], sa_ref[...], sb_ref[...]
    q_scale = DIFF_HEAD_DIM ** -0.5 * LOG2E
    for s in range(2 * d // LANES):
        z = qk[:, s * LANES:(s + 1) * LANES]
        zr = z * cos + pltpu.roll(z, LANES - ROT_FREQS, axis=1) * sa + pltpu.roll(z, ROT_FREQS, axis=1) * sb
        if s < d // LANES:
            q_ref[0, :, s * LANES:(s + 1) * LANES] = (zr * q_scale).astype(BF16)
        else:
            k_ref[0, :, (s * LANES - d):(s * LANES - d + LANES)] = zr.astype(BF16)


def _rope_tables(s_len, c_len):
    rows = s_len // GRID_W
    row = np.repeat(np.arange(rows, dtype=np.float32), GRID_W)
    col = np.tile(np.arange(GRID_W, dtype=np.float32), rows)
    inv_freq = np.float32(ROPE_BASE) ** (-np.arange(ROT_FREQS, dtype=np.float32) / np.float32(ROT_FREQS))
    ang_r = row[:, None] * inv_freq
    ang_c = col[:, None] * inv_freq
    ang = np.concatenate([ang_r, ang_r, ang_c, ang_c], axis=-1).astype(np.float64)
    cos, sin = np.cos(ang), np.sin(ang)
    first_half = (np.arange(DIFF_HEAD_DIM) // ROT_FREQS) % 2 == 0
    sa = np.where(first_half[None, :], -sin, 0.0)
    sb = np.where(first_half[None, :], 0.0, sin)
    pad = lambda t, fill: np.concatenate([t, np.full((c_len, DIFF_HEAD_DIM), fill)], axis=0)
    two = lambda t: np.concatenate([t, t], axis=-1).astype(np.float32)
    return two(pad(cos, 1.0)), two(pad(sa, 0.0)), two(pad(sb, 0.0))


def _qkv_call(x, ctx, mod, wqk, wvt, tables, ctx_row):
    b, s_len, d = x.shape
    n_lat = s_len // TM
    nt = n_lat + 1
    t = nt * TM
    mod_idx = lambda bi, j: (jnp.where(j < n_lat, bi, ctx_row), 0, 0)
    return pl.pallas_call(
        functools.partial(_qkv_kernel, d=d, n_lat=n_lat),
        out_shape=(jax.ShapeDtypeStruct((b, t, d), BF16), jax.ShapeDtypeStruct((b, t, d), BF16),
                   jax.ShapeDtypeStruct((b, nt, d, TM), BF16)),
        grid=(b, nt),
        in_specs=[
            pl.BlockSpec((1, TM, d), lambda bi, j: (bi, jnp.minimum(j, n_lat - 1), 0)),
            pl.BlockSpec((1, TM, d), lambda bi, j: (bi, 0, 0)),
            pl.BlockSpec((1, 1, 6 * d), mod_idx),
            pl.BlockSpec((d, 2 * d), lambda bi, j: (0, 0)),
            pl.BlockSpec((d, d), lambda bi, j: (0, 0)),
            pl.BlockSpec((TM, LANES), lambda bi, j: (j, 0)),
            pl.BlockSpec((TM, LANES), lambda bi, j: (j, 0)),
            pl.BlockSpec((TM, LANES), lambda bi, j: (j, 0)),
        ],
        out_specs=(pl.BlockSpec((1, TM, d), lambda bi, j: (bi, j, 0)),
                   pl.BlockSpec((1, TM, d), lambda bi, j: (bi, j, 0)),
                   pl.BlockSpec((1, 1, d, TM), lambda bi, j: (bi, j, 0, 0))),
        compiler_params=_cparams(("parallel", "parallel")),
        name="qkv_rope",
    )(x, ctx, mod, wqk, wvt, *tables)


def _attn_kernel(q_ref, k_ref, vt_ref, lam_ref, g_ref, o_ref, *, chunks, lambda_init, tq):
    q = q_ref[0]
    lane = lax.broadcasted_iota(I32, q.shape, 1)
    zero = jnp.zeros_like(q)
    qq = jnp.concatenate([jnp.where(lane < DIFF_HEAD_DIM, q, zero),
                          jnp.where(lane >= DIFF_HEAD_DIM, q, zero)], axis=0)
    m = jnp.full((1, 2 * tq), NEG_BIG, F32)
    l = jnp.zeros((1, 2 * tq), F32)
    acc = jnp.zeros((LANES, 2 * tq), F32)
    scores = lambda c: _dot_nt(k_ref[0, c * TK:(c + 1) * TK, :], qq)
    st_next = scores(chunks[0])
    for i, c in enumerate(chunks):
        st = st_next
        if i + 1 < len(chunks):
            st_next = scores(chunks[i + 1])
        m_new = jnp.maximum(m, jnp.max(st, axis=0, keepdims=True))
        alpha = jnp.exp2(m - m_new)
        p = jnp.exp2(st - m_new)
        l = l * alpha + jnp.sum(p, axis=0, keepdims=True)
        acc = acc * alpha + _dot(vt_ref[0, c], p.astype(BF16))
        m = m_new
    acc = acc / l
    lv = lam_ref[...]
    lam = (jnp.exp(jnp.sum(lv[0:1] * lv[1:2], axis=1, keepdims=True))
           - jnp.exp(jnp.sum(lv[2:3] * lv[3:4], axis=1, keepdims=True)) + lambda_init)
    ot = acc[:, :tq] - lam * acc[:, tq:]
    ms = jnp.mean(ot * ot, axis=0, keepdims=True)
    ot = ot * lax.rsqrt(ms + LN_EPS) * g_ref[...] * (1.0 - lambda_init)
    o_ref[0] = ot.T.astype(BF16)


def _attn_call(q, k, vt, lam_vecs, subln_g, lambda_init, *, tq, q_row0, n_rows, chunks, name):
    b, t, d = q.shape
    nt = t // TM
    q0 = q_row0 // tq
    return pl.pallas_call(
        functools.partial(_attn_kernel, chunks=chunks, lambda_init=lambda_init, tq=tq),
        out_shape=jax.ShapeDtypeStruct((b, n_rows, d), BF16),
        grid=(b, DIFF_HEADS, n_rows // tq),
        in_specs=[
            pl.BlockSpec((1, tq, LANES), lambda bi, h, qi: (bi, q0 + qi, h)),
            pl.BlockSpec((1, t, LANES), lambda bi, h, qi: (bi, 0, h)),
            pl.BlockSpec((1, nt, LANES, TM), lambda bi, h, qi: (bi, 0, h, 0)),
            pl.BlockSpec((4, DIFF_HEAD_DIM), lambda bi, h, qi: (0, 0)),
            pl.BlockSpec((LANES, 1), lambda bi, h, qi: (0, 0)),
        ],
        out_specs=pl.BlockSpec((1, tq, LANES), lambda bi, h, qi: (bi, qi, h)),
        compiler_params=_cparams(("parallel", "parallel", "parallel")),
        name=name,
    )(q, k, vt, lam_vecs, subln_g.reshape(LANES, 1))


def _route(logits_t, rb, ri_ref, rw_ref, cnt_ref, first):
    tm = logits_t.shape[1]
    aff = 1.0 / (1.0 + jnp.exp(-logits_t))
    sel = aff + rb
    row = lambda a, e: a[e:e + 1, :]
    gscore = []
    for g in range(N_GROUPS):
        a, b, c, d = (row(sel, g * EXPERTS_PER_GROUP + j) for j in range(EXPERTS_PER_GROUP))
        hi1, lo1, hi2, lo2 = jnp.maximum(a, b), jnp.minimum(a, b), jnp.maximum(c, d), jnp.minimum(c, d)
        gscore.append(jnp.maximum(hi1, hi2) + jnp.maximum(jnp.minimum(hi1, hi2), jnp.maximum(lo1, lo2)))
    best, grp = gscore[0], jnp.zeros((1, tm), I32)
    for g in range(1, N_GROUPS):
        better = gscore[g] > best
        best = jnp.where(better, gscore[g], best)
        grp = jnp.where(better, g, grp)
    in_sel, in_aff = [], []
    for j in range(EXPERTS_PER_GROUP):
        s_j, a_j = row(sel, j), row(aff, j)
        for g in range(1, N_GROUPS):
            s_j = jnp.where(grp == g, row(sel, g * EXPERTS_PER_GROUP + j), s_j)
            a_j = jnp.where(grp == g, row(aff, g * EXPERTS_PER_GROUP + j), a_j)
        in_sel.append(s_j)
        in_aff.append(a_j)
    v0, i0, a0 = in_sel[0], jnp.zeros((1, tm), I32), in_aff[0]
    for j in range(1, EXPERTS_PER_GROUP):
        better = in_sel[j] > v0
        v0 = jnp.where(better, in_sel[j], v0)
        i0 = jnp.where(better, j, i0)
        a0 = jnp.where(better, in_aff[j], a0)
    v1, i1, a1 = jnp.full((1, tm), -jnp.inf, F32), jnp.zeros((1, tm), I32), jnp.zeros((1, tm), F32)
    for j in range(EXPERTS_PER_GROUP):
        better = (in_sel[j] > v1) & (i0 != j)
        v1 = jnp.where(better, in_sel[j], v1)
        i1 = jnp.where(better, j, i1)
        a1 = jnp.where(better, in_aff[j], a1)
    e0 = grp * EXPERTS_PER_GROUP + i0
    e1 = grp * EXPERTS_PER_GROUP + i1
    wsum = a0 + a1
    erow = lax.broadcasted_iota(I32, (N_EXPERTS, tm), 0)
    oh0 = (erow == e0).astype(F32)
    oh1 = (erow == e1).astype(F32)
    both = oh0 + oh1
    both_bf = both.astype(BF16)
    tri = (lax.broadcasted_iota(I32, (tm, tm), 0) < lax.broadcasted_iota(I32, (tm, tm), 1)).astype(BF16)
    before = _dot(both_bf, tri)
    round_up = lambda n: jnp.floor((n + (ROW_GROUP - 1.0)) * (1.0 / ROW_GROUP)) * ROW_GROUP
    run_col = round_up(jnp.sum(both, axis=1, keepdims=True))
    e_r = lax.broadcasted_iota(I32, (N_EXPERTS, N_EXPERTS), 0)
    e_c = lax.broadcasted_iota(I32, (N_EXPERTS, N_EXPERTS), 1)
    off_col = _dot((e_c < e_r).astype(BF16), jnp.broadcast_to(run_col, (N_EXPERTS, LANES)).astype(BF16))[:, 0:1]
    base = off_col + before
    s0 = jnp.sum(oh0 * base, axis=0, keepdims=True)
    s1 = jnp.sum(oh1 * base, axis=0, keepdims=True)
    both_pad = jnp.concatenate([both_bf, jnp.zeros((LANES - N_EXPERTS, tm), BF16)], axis=0)
    run_row = round_up(_dot_nt(jnp.ones((8, tm), BF16), both_pad))
    l_r = lax.broadcasted_iota(I32, (LANES, LANES), 0)
    l_c = lax.broadcasted_iota(I32, (LANES, LANES), 1)
    off_row = _dot(run_row.astype(BF16), (l_r < l_c).astype(BF16))
    lane = lax.broadcasted_iota(I32, (1, LANES), 1)
    groups = run_row[0:1] * (1.0 / ROW_GROUP)
    groups = jnp.where(lane == N_EXPERTS, jnp.sum(groups, axis=1, keepdims=True), groups)

    @pl.when(first)
    def _():
        cnt_ref[...] = jnp.zeros_like(cnt_ref)

    zero_half = jnp.zeros((1, tm - LANES), I32)
    ri_ref[0, 0:1, :] = s0.astype(I32)
    ri_ref[0, 1:2, :] = s1.astype(I32)
    ri_ref[0, 2:3, :] = e0
    ri_ref[0, 3:4, :] = e1
    for r, v in ((4, groups), (5, cnt_ref[0:1, :]), (6, off_row[0:1])):
        ri_ref[0, r:r + 1, 0:LANES] = v.astype(I32)
        ri_ref[0, r:r + 1, LANES:] = zero_half
    ri_ref[0, 7:8, :] = jnp.zeros((1, tm), I32)
    cnt_ref[...] = cnt_ref[...] + run_row
    rw_ref[0, 0:1, :] = a0 / wsum
    rw_ref[0, 1:2, :] = a1 / wsum
    rw_ref[0, 2:8, :] = jnp.zeros((6, tm), F32)


def _post_mixer(y, x, mod, lng, lnb, rwt, rb, x1_ref, ri_ref, rw_ref, cnt_ref, first, d):
    x1 = _layer_norm(ALPHA * x + mod[:, 2 * d:3 * d] * y, lng, lnb)
    x1_ref[0] = x1
    tok = x1 * (1.0 + mod[:, 4 * d:5 * d]) + mod[:, 3 * d:4 * d]
    t_hi = tok.astype(BF16)
    t_mid = (tok - t_hi.astype(F32)).astype(BF16)
    w_hi, w_mid = rwt[0], rwt[1]
    logits = _dot(t_hi, w_hi) + (_dot(t_hi, w_mid) + _dot(t_mid, w_hi))
    _route(logits.T[0:N_EXPERTS, :], rb, ri_ref, rw_ref, cnt_ref, first)


def _attn_out_kernel(ol_ref, oc_ref, xl_ref, xc_ref, mod_ref, wo_ref, lng_ref, lnb_ref, rwt_ref, rb_ref,
                     x1_ref, ri_ref, rw_ref, cnt_ref, *, d, n_lat):
    first = (pl.program_id(0) == 0) & (pl.program_id(1) == 0)
    latent = pl.program_id(1) < n_lat
    o = jnp.where(latent, ol_ref[0], oc_ref[0])
    x = jnp.where(latent, xl_ref[0], xc_ref[0])
    y = _dot(o, wo_ref[...])
    _post_mixer(y, x, mod_ref[0], lng_ref[...], lnb_ref[...], rwt_ref[...], rb_ref[...],
                x1_ref, ri_ref, rw_ref, cnt_ref, first, d)


def _route_out_shapes(b, n_tiles, t_out, d):
    return (jax.ShapeDtypeStruct((b, t_out, d), F32),
            jax.ShapeDtypeStruct((b * n_tiles, 8, TM), I32),
            jax.ShapeDtypeStruct((b * n_tiles, 8, TM), F32),
            jax.ShapeDtypeStruct((8, LANES), F32))


def _route_out_specs(n_tiles, d):
    return (pl.BlockSpec((1, TM, d), lambda bi, j: (bi, j, 0)),
            pl.BlockSpec((1, 8, TM), lambda bi, j: (bi * n_tiles + j, 0, 0)),
            pl.BlockSpec((1, 8, TM), lambda bi, j: (bi * n_tiles + j, 0, 0)),
            pl.BlockSpec((8, LANES), lambda bi, j: (0, 0)))


def _attn_out_call(o_lat, o_ctx, x, ctx, mod, wo, lng, lnb, rwt, rb, ctx_row):
    b, s_len, d = x.shape
    n_lat = s_len // TM
    nt = n_lat + 1
    row = lambda a: pl.BlockSpec(a.shape, lambda bi, j: (0,) * a.ndim)
    lat = pl.BlockSpec((1, TM, d), lambda bi, j: (bi, jnp.minimum(j, n_lat - 1), 0))
    cxt = pl.BlockSpec((1, TM, d), lambda bi, j: (bi, 0, 0))
    return pl.pallas_call(
        functools.partial(_attn_out_kernel, d=d, n_lat=n_lat),
        out_shape=_route_out_shapes(b, nt, nt * TM, d),
        grid=(b, nt),
        in_specs=[
            lat, cxt, lat, cxt,
            pl.BlockSpec((1, 1, 6 * d), lambda bi, j: (jnp.where(j < n_lat, bi, ctx_row), 0, 0)),
            row(wo), row(lng), row(lnb), row(rwt), row(rb),
        ],
        out_specs=_route_out_specs(nt, d),
        compiler_params=_cparams(("arbitrary", "arbitrary")),
        name="attn_out_route",
    )(o_lat, o_ctx, x, ctx, mod, wo, lng, lnb, rwt, rb)


def _group_copy(src, r_src, dst, r_dst, sem):
    return pltpu.make_async_copy(src.at[pl.ds(pl.multiple_of(r_src, ROW_GROUP), ROW_GROUP)],
                                 dst.at[pl.ds(pl.multiple_of(r_dst, ROW_GROUP), ROW_GROUP)], sem)


def _tab(tab_ref, tile, row, e):
    return tab_ref[(tile * TAB_ROWS + row) * TAB_LANES + e]


def _run_copies(tab_ref, pad_ref, tile, stage, hbm, sem, to_hbm):
    for e in range(N_EXPERTS):
        s0 = _tab(tab_ref, tile, 2, e)
        h0 = pad_ref[e] + _tab(tab_ref, tile, 1, e)

        def issue(g, carry):
            s, h = s0 + g * ROW_GROUP, h0 + g * ROW_GROUP
            (_group_copy(stage, s, hbm, h, sem) if to_hbm else _group_copy(hbm, h, stage, s, sem)).start()
            return carry

        lax.fori_loop(0, _tab(tab_ref, tile, 0, e), issue, 0)


def _run_waits(tab_ref, tile, stage, hbm, sem, to_hbm):
    def drain(g, carry):
        (_group_copy(stage, 0, hbm, 0, sem) if to_hbm else _group_copy(hbm, 0, stage, 0, sem)).wait()
        return carry

    lax.fori_loop(0, _tab(tab_ref, tile, 0, N_EXPERTS), drain, 0)


def _dispatch_kernel(tab_ref, pad_ref, fill_ref, x_ref, mod_ref, ri_ref, xs_ref, stage_ref, zero_ref, sems, sem_z,
                     *, d, n_steps):
    step = pl.program_id(0) * pl.num_programs(1) + pl.program_id(1)
    buf = step % 2
    mod = mod_ref[0]
    tok = (x_ref[0] * (1.0 + mod[:, 4 * d:5 * d]) + mod[:, 3 * d:4 * d]).astype(BF16)
    slot = lax.broadcasted_iota(I32, (STAGE_ROWS, TM), 0)
    ri = ri_ref[0]
    pick = jnp.where((slot == ri[0:1, :]) | (slot == ri[1:2, :]), 1.0, 0.0).astype(BF16)
    for bsel in range(2):
        @pl.when(buf == bsel)
        def _():
            stage_ref[bsel] = _dot(pick, tok)
            _run_copies(tab_ref, pad_ref, step, stage_ref.at[bsel], xs_ref, sems.at[bsel], True)

    for bsel in range(2):
        @pl.when((buf != bsel) & (step > 0))
        def _():
            _run_waits(tab_ref, step - 1, stage_ref.at[bsel], xs_ref, sems.at[bsel], True)

    @pl.when(step == n_steps - 1)
    def _():
        for bsel in range(2):
            @pl.when(buf == bsel)
            def _():
                _run_waits(tab_ref, step, stage_ref.at[bsel], xs_ref, sems.at[bsel], True)

        zero_ref[...] = jnp.zeros_like(zero_ref)
        for e in range(N_EXPERTS + 1):
            lo, hi = fill_ref[2 * e], fill_ref[2 * e + 1]

            def fill(g, carry):
                _group_copy(zero_ref, 0, xs_ref, g * ROW_GROUP, sem_z).start()
                return carry

            def fill_wait(g, carry):
                _group_copy(zero_ref, 0, xs_ref, 0, sem_z).wait()
                return carry

            lax.fori_loop(lo, hi, fill, 0)
            lax.fori_loop(lo, hi, fill_wait, 0)


def _dispatch_call(tab, pad_start, fill, x1, mod, ri, n_tiles, n_pad, ctx_row, n_lat):
    b, t, d = x1.shape
    grid_spec = pltpu.PrefetchScalarGridSpec(
        num_scalar_prefetch=3,
        grid=(b, n_tiles),
        in_specs=[
            pl.BlockSpec((1, TM, d), lambda bi, j, *_: (bi, j, 0)),
            pl.BlockSpec((1, 1, 6 * d), lambda bi, j, *_: (jnp.where(j < n_lat, bi, ctx_row), 0, 0)),
            pl.BlockSpec((1, 8, TM), lambda bi, j, *_: (bi * n_tiles + j, 0, 0)),
        ],
        out_specs=pl.BlockSpec(memory_space=pl.ANY),
        scratch_shapes=[pltpu.VMEM((2, STAGE_ROWS, d), F32), pltpu.VMEM((ROW_GROUP, d), F32),
                        pltpu.SemaphoreType.DMA((2,)), pltpu.SemaphoreType.DMA],
    )
    return pl.pallas_call(
        functools.partial(_dispatch_kernel, d=d, n_steps=b * n_tiles),
        out_shape=jax.ShapeDtypeStruct((n_pad, d), F32),
        grid_spec=grid_spec,
        compiler_params=_cparams(("arbitrary", "arbitrary")),
        name="moe_dispatch",
    )(tab, pad_start, fill, x1, mod, ri)


def _ffn_kernel(be_ref, nu_ref, xs_ref, wg_ref, wu_ref, wd_ref, ys_ref, wg_s, wu_s, wd_s):
    i = pl.program_id(0)

    @pl.when(i < nu_ref[0])
    def _():
        prev = be_ref[jnp.maximum(i - 1, 0)]

        @pl.when((i == 0) | (be_ref[i] != prev))
        def _():
            wg_s[...] = wg_ref[0, 0].astype(BF16)
            wu_s[...] = wu_ref[0, 0].astype(BF16)
            wd_s[...] = wd_ref[0, 0].astype(BF16)

        x = xs_ref[...].astype(BF16)
        g = _dot(x, wg_s[...])
        u = _dot(x, wu_s[...])
        ys_ref[...] = _dot((_silu(g) * u).astype(BF16), wd_s[...])

    @pl.when(i >= nu_ref[0])
    def _():
        ys_ref[...] = jnp.zeros_like(ys_ref)


def _ffn_call(block_expert, n_used, xs, w_gate, w_up, w_down, layer):
    n_pad, d = xs.shape
    de = w_gate.shape[-1]
    n_blocks = n_pad // MOE_BLK
    blk = lambda i, be, nu: (jnp.minimum(i, nu[0] - 1), 0)
    exp = lambda i, be, nu: (layer, be[jnp.minimum(i, nu[0] - 1)], 0, 0)
    grid_spec = pltpu.PrefetchScalarGridSpec(
        num_scalar_prefetch=2,
        grid=(n_blocks,),
        in_specs=[
            pl.BlockSpec((MOE_BLK, d), blk),
            pl.BlockSpec((1, 1, d, de), exp),
            pl.BlockSpec((1, 1, d, de), exp),
            pl.BlockSpec((1, 1, de, d), exp),
        ],
        out_specs=pl.BlockSpec((MOE_BLK, d), lambda i, be, nu: (i, 0)),
        scratch_shapes=[pltpu.VMEM((d, de), BF16), pltpu.VMEM((d, de), BF16), pltpu.VMEM((de, d), BF16)],
    )
    return pl.pallas_call(
        _ffn_kernel,
        out_shape=jax.ShapeDtypeStruct((n_pad, d), F32),
        grid_spec=grid_spec,
        compiler_params=_cparams(("arbitrary",)),
        name="moe_ffn",
    )(block_expert, n_used, xs, w_gate, w_up, w_down)


def _combine_kernel(tab_ref, pad_ref, x_ref, mod_ref, rit_ref, rw_ref, lng_ref, lnb_ref, ys_ref, o_ref,
                    stage_ref, sems, *, d, n_steps):
    step = pl.program_id(0) * pl.num_programs(1) + pl.program_id(1)
    buf = step % 2

    @pl.when(step == 0)
    def _():
        stage_ref[...] = jnp.zeros_like(stage_ref)
        _run_copies(tab_ref, pad_ref, step, stage_ref.at[0], ys_ref, sems.at[0], False)

    for bsel in range(2):
        @pl.when((buf != bsel) & (step + 1 < n_steps))
        def _():
            _run_copies(tab_ref, pad_ref, step + 1, stage_ref.at[bsel], ys_ref, sems.at[bsel], False)

    rit = rit_ref[0]
    lane = lax.broadcasted_iota(I32, (TM, STAGE_ROWS), 1)
    pick0 = jnp.where(lane == rit[:, 0:1], 1.0, 0.0).astype(BF16)
    pick1 = jnp.where(lane == rit[:, 1:2], 1.0, 0.0).astype(BF16)
    w = rw_ref[0]
    mod = mod_ref[0]
    for bsel in range(2):
        @pl.when(buf == bsel)
        def _():
            _run_waits(tab_ref, step, stage_ref.at[bsel], ys_ref, sems.at[bsel], False)
            stage = stage_ref[bsel].astype(BF16)
            moe = w[:, 0:1] * _dot(pick0, stage) + w[:, 1:2] * _dot(pick1, stage)
            o_ref[0] = _layer_norm(ALPHA * x_ref[0] + mod[:, 5 * d:6 * d] * moe, lng_ref[...], lnb_ref[...])


def _combine_call(tab, pad_start, x1, mod, ri_t, rw_t, lng, lnb, ys, n_tiles, ctx_row, n_lat):
    b, _, d = x1.shape
    row = lambda a: pl.BlockSpec(a.shape, lambda bi, j, *_: (0,) * a.ndim)
    grid_spec = pltpu.PrefetchScalarGridSpec(
        num_scalar_prefetch=2,
        grid=(b, n_tiles),
        in_specs=[
            pl.BlockSpec((1, TM, d), lambda bi, j, *_: (bi, j, 0)),
            pl.BlockSpec((1, 1, 6 * d), lambda bi, j, *_: (jnp.where(j < n_lat, bi, ctx_row), 0, 0)),
            pl.BlockSpec((1, TM, 8), lambda bi, j, *_: (bi * n_tiles + j, 0, 0)),
            pl.BlockSpec((1, TM, 8), lambda bi, j, *_: (bi * n_tiles + j, 0, 0)),
            row(lng), row(lnb),
            pl.BlockSpec(memory_space=pl.ANY),
        ],
        out_specs=pl.BlockSpec((1, TM, d), lambda bi, j, *_: (bi, j, 0)),
        scratch_shapes=[pltpu.VMEM((2, STAGE_ROWS, d), F32), pltpu.SemaphoreType.DMA((2,))],
    )
    return pl.pallas_call(
        functools.partial(_combine_kernel, d=d, n_steps=b * n_tiles),
        out_shape=jax.ShapeDtypeStruct((b, n_tiles * TM, d), F32),
        grid_spec=grid_spec,
        compiler_params=_cparams(("arbitrary", "arbitrary")),
        name="moe_combine",
    )(tab, pad_start, x1, mod, ri_t, rw_t, lng, lnb, ys)


def _moe(x1, mod, ri, rw, counts, lng, lnb, w_gate, w_up, w_down, layer, n_tiles, ctx_row, n_lat):
    b, _, d = x1.shape
    n_steps = b * n_tiles
    n_rows_max = 2 * n_steps * TM + (ROW_GROUP - 1) * N_EXPERTS * n_steps
    n_pad = -(-n_rows_max // MOE_BLK) * MOE_BLK + N_EXPERTS * MOE_BLK
    n_blocks = n_pad // MOE_BLK
    cnt = counts[0, :N_EXPERTS].astype(I32)
    padded = (cnt + MOE_BLK - 1) // MOE_BLK * MOE_BLK
    pad_end = jnp.cumsum(padded)
    pad_start = jnp.concatenate([jnp.zeros((1,), I32), pad_end]).astype(I32)
    block_row0 = jnp.arange(n_blocks, dtype=I32) * MOE_BLK
    block_expert = jnp.minimum(
        jnp.sum((pad_end[None, :] <= block_row0[:, None]).astype(I32), axis=1), N_EXPERTS - 1).astype(I32)
    n_used = (pad_end[-1:] // MOE_BLK).astype(I32)
    fill_lo = jnp.concatenate([pad_start[:-1] + cnt, pad_end[-1:]]) // ROW_GROUP
    fill_hi = jnp.concatenate([pad_end, jnp.full((1,), n_pad, I32)]) // ROW_GROUP
    fill = jnp.stack([fill_lo, fill_hi], axis=1).reshape(-1).astype(I32)
    tab = ri[:, 4:4 + TAB_ROWS, :TAB_LANES].reshape(-1)
    xs = _dispatch_call(tab, pad_start, fill, x1, mod, ri, n_tiles, n_pad, ctx_row, n_lat)
    ys = _ffn_call(block_expert, n_used, xs, w_gate, w_up, w_down, layer)
    ri_t, rw_t = jnp.swapaxes(ri, 1, 2), jnp.swapaxes(rw, 1, 2)
    return _combine_call(tab, pad_start, x1, mod, ri_t, rw_t, lng, lnb, ys, n_tiles, ctx_row, n_lat)


def _gla_proj_kernel(x_ref, mod_ref, win_ref, wgd_ref, wgu_ref, bg_ref, q_ref, k_ref, v_ref, r_ref, g_ref, *, d):
    mod = mod_ref[0]
    h = (x_ref[0] * (1.0 + mod[:, d:2 * d]) + mod[:, 0:d]).astype(BF16)
    dk = GLA_HEADS * GLA_DK
    p = _dot(h, win_ref[...])
    q_ref[0] = (p[:, 0:dk] * GLA_DK ** -0.5).astype(BF16)
    k_ref[0] = p[:, dk:2 * dk].astype(BF16)
    v_ref[0] = p[:, 2 * dk:2 * dk + d].astype(BF16)
    r_ref[0] = p[:, 2 * dk + d:].astype(BF16)
    low = _dot(h, wgd_ref[...]).astype(BF16)
    pre = _dot(low, wgu_ref[...]) + bg_ref[...]
    g_ref[0] = (jnp.minimum(pre, 0.0) - jnp.log1p(jnp.exp(-jnp.abs(pre)))) * (1.0 / GLA_TAU)


def _gla_proj_call(xs, mod, win, wgd, wgu, bg, ctx_row):
    b, t, d = xs.shape
    nt = t // TM
    n_lat = nt - 1
    dk = GLA_HEADS * GLA_DK
    row = lambda a: pl.BlockSpec(a.shape, lambda bi, j: (0,) * a.ndim)
    tile = lambda n: pl.BlockSpec((1, TM, n), lambda bi, j: (bi, j, 0))
    return pl.pallas_call(
        functools.partial(_gla_proj_kernel, d=d),
        out_shape=(jax.ShapeDtypeStruct((b, t, dk), BF16), jax.ShapeDtypeStruct((b, t, dk), BF16),
                   jax.ShapeDtypeStruct((b, t, d), BF16), jax.ShapeDtypeStruct((b, t, d), BF16),
                   jax.ShapeDtypeStruct((b, t, 2 * dk), F32)),
        grid=(b, nt),
        in_specs=[tile(d), pl.BlockSpec((1, 1, 6 * d), lambda bi, j: (jnp.where(j < n_lat, bi, ctx_row), 0, 0)),
                  row(win), row(wgd), row(wgu), row(bg)],
        out_specs=(tile(dk), tile(dk), tile(d), tile(d), tile(2 * dk)),
        compiler_params=_cparams(("parallel", "parallel")),
        name="gla_proj",
    )(xs, mod, win, wgd, wgu, bg)


def _split3(x):
    hi = x.astype(BF16)
    r1 = x - hi.astype(F32)
    mid = r1.astype(BF16)
    lo = (r1 - mid.astype(F32)).astype(BF16)
    return hi, mid, lo


class _GlaMasks:
    def __init__(self, reverse):
        L = GLA_CHUNK
        t = lax.broadcasted_iota(I32, (L, L), 0)
        s = lax.broadcasted_iota(I32, (L, L), 1)
        seen = (s >= t) if reverse else (s <= t)
        self.tri = seen.astype(BF16)
        blk = lambda r: jnp.right_shift(r, GLA_SUB.bit_length() - 1)
        dist = (blk(s) - blk(t)) if reverse else (blk(t) - blk(s))
        self.pairs = [seen & (dist == 0)] + [dist == n for n in range(1, L // GLA_SUB)]


def _gla_chunk_matmuls(q_ref, k_ref, v_ref, b_all, rows, h, state_ref, reverse):
    L, SB, DK = GLA_CHUNK, GLA_SUB, GLA_DK
    nb = L // SB
    kcols = slice(h * DK, (h + 1) * DK)
    vcols = slice(h * GLA_DV, (h + 1) * GLA_DV)
    bh = b_all[:, kcols]
    zero = jnp.zeros((1, DK), F32)
    earlier = lambda i, n: i + n if reverse else i - n

    def start_ref(i):
        if not 0 <= i < nb or i == (nb - 1 if reverse else 0):
            return zero
        r = (i + 1) * SB if reverse else i * SB - 1
        return bh[r:r + 1, :]

    def end_ref(i):
        r = i * SB if reverse else i * SB + SB - 1
        return bh[r:r + 1, :]

    per_block = lambda f: jnp.concatenate([jnp.broadcast_to(f(i), (SB, DK)) for i in range(nb)], axis=0)
    bs, be = per_block(start_ref), per_block(end_ref)
    b_last = end_ref(0 if reverse else nb - 1)
    q = q_ref[0, rows, kcols].astype(F32)
    k = k_ref[0, rows, kcols].astype(F32)
    v = v_ref[0, rows, vcols]
    scaled = lambda x, e: (x * jnp.exp(e)).astype(BF16)
    q_own = scaled(q, bh - bs)
    k_own, k_end = scaled(k, bs - bh), scaled(k, be - bh)
    pieces = [_dot_nt(q_own, k_own), _dot_nt(q_own, k_end)]
    for n in range(2, nb):
        pieces.append(_dot_nt(scaled(q, bh - per_block(lambda i: start_ref(earlier(i, n - 1)))), k_end))
    state_t = state_ref[h]
    o_inter = _dot_nt(scaled(q, bh), state_t.astype(BF16))
    update = _dot_tn(v, scaled(k, b_last - bh))
    return pieces, o_inter, update, b_last, state_t, v, vcols


def _gla_scan_kernel(qf, kf, vf, gf, qb, kb, vb, gb, of_ref, ob_ref, sf_ref, sb_ref):
    @pl.when(pl.program_id(1) == 0)
    def _():
        sf_ref[...] = jnp.zeros_like(sf_ref)
        sb_ref[...] = jnp.zeros_like(sb_ref)

    L = GLA_CHUNK
    n_chunks = TM // L
    dirs = ((qf, kf, vf, gf, of_ref, sf_ref, False, _GlaMasks(False)),
            (qb, kb, vb, gb, ob_ref, sb_ref, True, _GlaMasks(True)))
    for step in range(n_chunks):
        work = []
        for q_ref, k_ref, v_ref, g_ref, o_ref, state_ref, reverse, masks in dirs:
            c = n_chunks - 1 - step if reverse else step
            rows = slice(c * L, (c + 1) * L)
            b_all = sum(_dot(masks.tri, part) for part in _split3(g_ref[0, rows, :]))
            for h in range(GLA_HEADS):
                work.append((o_ref, state_ref, rows, h, masks,
                             _gla_chunk_matmuls(q_ref, k_ref, v_ref, b_all, rows, h, state_ref, reverse)))
        for o_ref, state_ref, rows, h, masks, (pieces, o_inter, update, b_last, state_t, v, vcols) in work:
            a = sum(jnp.where(m, p, 0.0) for m, p in zip(masks.pairs, pieces)).astype(BF16)
            o_ref[0, rows, vcols] = _dot(a, v) + o_inter
            state_ref[h] = jnp.exp(b_last) * state_t + update


def _gla_scan_call(q, k, v, g):
    b, t, d = v.shape
    nt = t // TM
    n_lat = nt - 1
    dk = GLA_HEADS * GLA_DK
    fwd = lambda j: jnp.where(j == 0, n_lat, j - 1)
    bwd = lambda j: jnp.where(j == 0, n_lat, n_lat - j)
    spec = lambda n, idx, col: pl.BlockSpec((1, TM, n), lambda bi, j: (bi, idx(j), col))
    return pl.pallas_call(
        _gla_scan_kernel,
        out_shape=(jax.ShapeDtypeStruct((b, n_lat * TM, d), F32), jax.ShapeDtypeStruct((b, n_lat * TM, d), F32)),
        grid=(b, nt),
        in_specs=[spec(dk, fwd, 0), spec(dk, fwd, 0), spec(d, fwd, 0), spec(dk, fwd, 0),
                  spec(dk, bwd, 0), spec(dk, bwd, 0), spec(d, bwd, 0), spec(dk, bwd, 1)],
        out_specs=(pl.BlockSpec((1, TM, d), lambda bi, j: (bi, jnp.maximum(j - 1, 0), 0)),
                   pl.BlockSpec((1, TM, d), lambda bi, j: (bi, n_lat - jnp.maximum(j, 1), 0))),
        scratch_shapes=[pltpu.VMEM((GLA_HEADS, GLA_DV, GLA_DK), F32), pltpu.VMEM((GLA_HEADS, GLA_DV, GLA_DK), F32)],
        compiler_params=_cparams(("parallel", "arbitrary")),
        name="gla_scan",
    )(q, k, v, g, q, k, v, g)


def _gla_out_kernel(of_ref, ob_ref, r_ref, x_ref, mod_ref, ng_ref, wo_ref, lng_ref, lnb_ref, rwt_ref, rb_ref,
                    x1_ref, ri_ref, rw_ref, cnt_ref, *, d):
    first = (pl.program_id(0) == 0) & (pl.program_id(1) == 0)
    o = of_ref[0] + ob_ref[0]
    r = r_ref[0].astype(F32)
    gate = _silu(r)
    ng = ng_ref[...]
    parts = []
    for h in range(GLA_HEADS):
        cols = slice(h * GLA_DV, (h + 1) * GLA_DV)
        oh = o[:, cols]
        ms = jnp.mean(oh * oh, axis=-1, keepdims=True)
        parts.append((oh * lax.rsqrt(ms + LN_EPS) * ng * gate[:, cols]).astype(BF16))
    y = _dot(jnp.concatenate(parts, axis=-1), wo_ref[...])
    _post_mixer(y, x_ref[0], mod_ref[0], lng_ref[...], lnb_ref[...], rwt_ref[...], rb_ref[...],
                x1_ref, ri_ref, rw_ref, cnt_ref, first, d)


def _gla_out_call(o_f, o_b, r, xs, mod, ng, wo, lng, lnb, rwt, rb):
    b, s_len, d = o_f.shape
    n_lat = s_len // TM
    row = lambda a: pl.BlockSpec(a.shape, lambda bi, j: (0,) * a.ndim)
    tile = pl.BlockSpec((1, TM, d), lambda bi, j: (bi, j, 0))
    return pl.pallas_call(
        functools.partial(_gla_out_kernel, d=d),
        out_shape=_route_out_shapes(b, n_lat, s_len, d),
        grid=(b, n_lat),
        in_specs=[tile, tile, tile, tile, pl.BlockSpec((1, 1, 6 * d), lambda bi, j: (bi, 0, 0)),
                  row(ng), row(wo), row(lng), row(lnb), row(rwt), row(rb)],
        out_specs=_route_out_specs(n_lat, d),
        compiler_params=_cparams(("arbitrary", "arbitrary")),
        name="gla_out_route",
    )(o_f, o_b, r, xs, mod, ng, wo, lng, lnb, rwt, rb)


def kernel(x, c, ctx, c_ctx, ada_w, ada_b, ln_g, ln_b, attn_w_qkv, attn_w_o, attn_lambda, attn_subln_g, gla_w_in, gla_wg_down, gla_wg_up, gla_bg, gla_norm_g, gla_w_o, router_w, router_b, moe_w_gate, moe_w_up, moe_w_down):
    b, s_len, d = x.shape
    c_len = ctx.shape[1]
    assert ada_w.shape[0] == DEPTH and c_len == TM and s_len % TQ == 0 and TQ % TM == 0 and s_len % GRID_W == 0
    assert d == DIFF_HEADS * 2 * DIFF_HEAD_DIM == GLA_HEADS * GLA_DV
    assert STAGE_ROWS % LANES == 0 and STAGE_ROWS >= 2 * TM + (ROW_GROUP - 1) * N_EXPERTS
    n_lat = s_len // TM
    nt = n_lat + 1
    ctx_row = b

    cond_rows = -(-(b + 1) // 8) * 8
    cond = jnp.concatenate([c, c_ctx[None, :], jnp.zeros((cond_rows - b - 1, d), F32)], axis=0)
    mods = _ada_call(cond, ada_w, ada_b)
    rw_pad = jnp.concatenate([router_w, jnp.zeros((d, LANES - N_EXPERTS), F32)], axis=1)
    rw_hi = rw_pad.astype(BF16)
    rwt = jnp.stack([rw_hi, (rw_pad - rw_hi.astype(F32)).astype(BF16)])
    rb = router_b.reshape(N_EXPERTS, 1)

    mod0 = mods[0].reshape(cond_rows, 1, 6 * d)
    wqkv = attn_w_qkv[0]
    wqk = wqkv[:, :2 * d].astype(BF16)
    wvt = wqkv[:, 2 * d:].T.astype(BF16)
    tables = tuple(jnp.asarray(t) for t in _rope_tables(s_len, c_len))
    q, k, vt = _qkv_call(x, ctx, mod0, wqk, wvt, tables, ctx_row)
    lambda_init = 0.8 - 0.6 * math.exp(-0.3 * 0)
    attn = functools.partial(_attn_call, q, k, vt, attn_lambda[0], attn_subln_g[0], lambda_init)
    o_lat = attn(tq=TQ, q_row0=0, n_rows=s_len, chunks=tuple(range(nt)), name="diff_attn_lat")
    o_ctx = attn(tq=c_len, q_row0=s_len, n_rows=c_len, chunks=(n_lat,), name="diff_attn_ctx")
    x1, ri, rw, counts = _attn_out_call(o_lat, o_ctx, x, ctx, mod0, attn_w_o[0].astype(BF16),
                                        ln_g[0, 0][None], ln_b[0, 0][None], rwt, rb, ctx_row)
    stream = _moe(x1, mod0, ri, rw, counts, ln_g[0, 1][None], ln_b[0, 1][None],
                  moe_w_gate, moe_w_up, moe_w_down, 0, nt, ctx_row, n_lat)

    mod1 = mods[1].reshape(cond_rows, 1, 6 * d)
    dk = GLA_HEADS * GLA_DK
    wgd = jnp.concatenate([gla_wg_down[0, 0], gla_wg_down[0, 1],
                           jnp.zeros((d, LANES - 2 * GLA_GATE_RANK), F32)], axis=1).astype(BF16)
    wgu = jnp.zeros((LANES, 2 * dk), F32)
    wgu = wgu.at[0:GLA_GATE_RANK, 0:dk].set(gla_wg_up[0, 0])
    wgu = wgu.at[GLA_GATE_RANK:2 * GLA_GATE_RANK, dk:].set(gla_wg_up[0, 1]).astype(BF16)
    bg = gla_bg[0].reshape(1, 2 * dk)
    gq, gk, gv, gr, gg = _gla_proj_call(stream, mod1, gla_w_in[0].astype(BF16), wgd, wgu, bg, ctx_row)
    o_f, o_b = _gla_scan_call(gq, gk, gv, gg)
    x1, ri, rw, counts = _gla_out_call(o_f, o_b, gr, stream, mod1, gla_norm_g[0][None], gla_w_o[0].astype(BF16),
                                       ln_g[1, 0][None], ln_b[1, 0][None], rwt, rb)
    return _moe(x1, mod1, ri, rw, counts, ln_g[1, 1][None], ln_b[1, 1][None],
                moe_w_gate, moe_w_up, moe_w_down, 1, n_lat, ctx_row, n_lat)
```

```python
import functools
import math

import numpy as np
import jax
import jax.numpy as jnp
from jax import lax
from jax.experimental import pallas as pl
from jax.experimental.pallas import tpu as pltpu

F32 = jnp.float32
BF16 = jnp.bfloat16
I32 = jnp.int32

DEPTH = 2
ALPHA = (2.0 * DEPTH) ** 0.25
LN_EPS = 1e-5
GRID_W = 64
ROPE_BASE = 10000.0
DIFF_HEADS = 8
DIFF_HEAD_DIM = 64
ROT_FREQS = DIFF_HEAD_DIM // 4
GLA_HEADS = 4
GLA_DK = 128
GLA_DV = 256
GLA_GATE_RANK = 16
GLA_TAU = 16.0
GLA_CHUNK = 64
GLA_SUB = 16
N_EXPERTS = 16
N_GROUPS = 4
EXPERTS_PER_GROUP = 4
LOG2E = 1.4426950408889634

LANES = 128
TM = 256
TK = 256
TQ = 512
MOE_BLK = 256
ADA_TN = 1536
ROW_GROUP = 8
STAGE_ROWS = 640
TAB_ROWS, TAB_LANES = 3, 32
VMEM_LIMIT = 48 * 1024 * 1024
NEG_BIG = -1e30
SOFTMAX_HEADROOM = 64.0
BOUND_SLACK = 1.001
SOFTMAX_MIN_SUM = 2.0 ** -40


def _cparams(sem):
    return pltpu.CompilerParams(dimension_semantics=sem, vmem_limit_bytes=VMEM_LIMIT)


def _silu(x):
    return x * (1.0 / (1.0 + jnp.exp(-x)))


def _layer_norm(z, g, b):
    mu = jnp.mean(z, axis=-1, keepdims=True)
    zc = z - mu
    var = jnp.mean(zc * zc, axis=-1, keepdims=True)
    return zc * lax.rsqrt(var + LN_EPS) * g + b


def _dot(a, b):
    return jnp.dot(a, b, preferred_element_type=F32)


def _dot_nt(a, b, precision=None):
    return lax.dot_general(a, b, (((1,), (1,)), ((), ())), preferred_element_type=F32, precision=precision)


def _dot_tn(a, b):
    return lax.dot_general(a, b, (((0,), (0,)), ((), ())), preferred_element_type=F32)


def _ada_kernel(c_ref, w_ref, b_ref, o_ref):
    s = _silu(c_ref[...])
    o_ref[0] = jnp.dot(s, w_ref[0], preferred_element_type=F32, precision=lax.Precision.HIGHEST) + b_ref[0]


def _ada_call(cond, ada_w, ada_b):
    depth, d, n = ada_w.shape
    rows = cond.shape[0]
    return pl.pallas_call(
        _ada_kernel,
        out_shape=jax.ShapeDtypeStruct((depth, rows, n), F32),
        grid=(depth, n // ADA_TN),
        in_specs=[
            pl.BlockSpec((rows, d), lambda i, j: (0, 0)),
            pl.BlockSpec((1, d, ADA_TN), lambda i, j: (i, 0, j)),
            pl.BlockSpec((1, 1, ADA_TN), lambda i, j: (i, 0, j)),
        ],
        out_specs=pl.BlockSpec((1, rows, ADA_TN), lambda i, j: (i, 0, j)),
        compiler_params=_cparams(("parallel", "parallel")),
        name="ada_mod",
    )(cond, ada_w, ada_b.reshape(depth, 1, n))


def _qkv_kernel(xl_ref, xc_ref, mod_ref, wqk_ref, wvt_ref, cos_ref, sa_ref, sb_ref, q_ref, k_ref, vt_ref, *, d, n_lat):
    mod = mod_ref[0]
    x = jnp.where(pl.program_id(1) < n_lat, xl_ref[0], xc_ref[0])
    h = (x * (1.0 + mod[:, d:2 * d]) + mod[:, 0:d]).astype(BF16)
    qk = _dot(h, wqk_ref[...])
    vt_ref[0, 0] = _dot_nt(wvt_ref[...], h).astype(BF16)
    cos, sa, sb = cos_ref[...], sa_ref[...], sb_ref[...]
    q_scale = DIFF_HEAD_DIM ** -0.5 * LOG2E
    for s in range(2 * d // LANES):
        z = qk[:, s * LANES:(s + 1) * LANES]
        zr = z * cos + pltpu.roll(z, LANES - ROT_FREQS, axis=1) * sa + pltpu.roll(z, ROT_FREQS, axis=1) * sb
        if s < d // LANES:
            q_ref[0, :, s * LANES:(s + 1) * LANES] = (zr * q_scale).astype(BF16)
        else:
            k_ref[0, :, (s * LANES - d):(s * LANES - d + LANES)] = zr.astype(BF16)


def _rope_tables(s_len, c_len):
    rows = s_len // GRID_W
    row = np.repeat(np.arange(rows, dtype=np.float32), GRID_W)
    col = np.tile(np.arange(GRID_W, dtype=np.float32), rows)
    inv_freq = np.float32(ROPE_BASE) ** (-np.arange(ROT_FREQS, dtype=np.float32) / np.float32(ROT_FREQS))
    ang_r = row[:, None] * inv_freq
    ang_c = col[:, None] * inv_freq
    ang = np.concatenate([ang_r, ang_r, ang_c, ang_c], axis=-1).astype(np.float64)
    cos, sin = np.cos(ang), np.sin(ang)
    first_half = (np.arange(DIFF_HEAD_DIM) // ROT_FREQS) % 2 == 0
    sa = np.where(first_half[None, :], -sin, 0.0)
    sb = np.where(first_half[None, :], 0.0, sin)
    pad = lambda t, fill: np.concatenate([t, np.full((c_len, DIFF_HEAD_DIM), fill)], axis=0)
    two = lambda t: np.concatenate([t, t], axis=-1).astype(np.float32)
    return two(pad(cos, 1.0)), two(pad(sa, 0.0)), two(pad(sb, 0.0))


def _qkv_call(x, ctx, mod, wqk, wvt, tables, ctx_row):
    b, s_len, d = x.shape
    n_lat = s_len // TM
    nt = n_lat + 1
    t = nt * TM
    mod_idx = lambda bi, j: (jnp.where(j < n_lat, bi, ctx_row), 0, 0)
    return pl.pallas_call(
        functools.partial(_qkv_kernel, d=d, n_lat=n_lat),
        out_shape=(jax.ShapeDtypeStruct((b, t, d), BF16), jax.ShapeDtypeStruct((b, t, d), BF16),
                   jax.ShapeDtypeStruct((b, nt, d, TM), BF16)),
        grid=(b, nt),
        in_specs=[
            pl.BlockSpec((1, TM, d), lambda bi, j: (bi, jnp.minimum(j, n_lat - 1), 0)),
            pl.BlockSpec((1, TM, d), lambda bi, j: (bi, 0, 0)),
            pl.BlockSpec((1, 1, 6 * d), mod_idx),
            pl.BlockSpec((d, 2 * d), lambda bi, j: (0, 0)),
            pl.BlockSpec((d, d), lambda bi, j: (0, 0)),
            pl.BlockSpec((TM, LANES), lambda bi, j: (j, 0)),
            pl.BlockSpec((TM, LANES), lambda bi, j: (j, 0)),
            pl.BlockSpec((TM, LANES), lambda bi, j: (j, 0)),
        ],
        out_specs=(pl.BlockSpec((1, TM, d), lambda bi, j: (bi, j, 0)),
                   pl.BlockSpec((1, TM, d), lambda bi, j: (bi, j, 0)),
                   pl.BlockSpec((1, 1, d, TM), lambda bi, j: (bi, j, 0, 0))),
        compiler_params=_cparams(("parallel", "parallel")),
        name="qkv_rope",
    )(x, ctx, mod, wqk, wvt, *tables)


def _attn_kernel(q_ref, k_ref, vt_ref, lam_ref, g_ref, o_ref, kn_ref, *, chunks, lambda_init, tq):
    q = q_ref[0]
    lane = lax.broadcasted_iota(I32, q.shape, 1)
    zero = jnp.zeros_like(q)
    qq = jnp.concatenate([jnp.where(lane < DIFF_HEAD_DIM, q, zero),
                          jnp.where(lane >= DIFF_HEAD_DIM, q, zero)], axis=0)
    k_chunk = lambda c: k_ref[0, c * TK:(c + 1) * TK, :]

    def finish(acc, l):
        acc = acc / l
        lv = lam_ref[...]
        lam = (jnp.exp(jnp.sum(lv[0:1] * lv[1:2], axis=1, keepdims=True))
               - jnp.exp(jnp.sum(lv[2:3] * lv[3:4], axis=1, keepdims=True)) + lambda_init)
        ot = acc[:, :tq] - lam * acc[:, tq:]
        ms = jnp.mean(ot * ot, axis=0, keepdims=True)
        ot = ot * lax.rsqrt(ms + LN_EPS) * g_ref[...] * (1.0 - lambda_init)
        o_ref[0] = ot.T.astype(BF16)

    @pl.when(pl.program_id(2) == 0)
    def _():
        klane = lax.broadcasted_iota(I32, (TK, LANES), 1)
        best1 = jnp.zeros((1, 1), F32)
        best2 = jnp.zeros((1, 1), F32)
        for c in chunks:
            kf = k_chunk(c).astype(F32)
            ksq = kf * kf
            n1 = jnp.sum(jnp.where(klane < DIFF_HEAD_DIM, ksq, 0.0), axis=1, keepdims=True)
            n2 = jnp.sum(jnp.where(klane >= DIFF_HEAD_DIM, ksq, 0.0), axis=1, keepdims=True)
            best1 = jnp.maximum(best1, jnp.max(n1, axis=0, keepdims=True))
            best2 = jnp.maximum(best2, jnp.max(n2, axis=0, keepdims=True))
        kn_ref[...] = jnp.where(lax.broadcasted_iota(I32, kn_ref.shape, 1) < DIFF_HEAD_DIM, best1, best2)

    kn = kn_ref[0:1, :]
    qf = qq.astype(F32)
    qsq = jnp.sum(qf * qf, axis=1, keepdims=True)
    half = lax.broadcasted_iota(I32, (2 * tq, 1), 0) < tq
    bound = jnp.sqrt(qsq * jnp.where(half, kn[:, 0:1], kn[:, DIFF_HEAD_DIM:DIFF_HEAD_DIM + 1])) * BOUND_SLACK
    shift = SOFTMAX_HEADROOM - bound
    lane2 = lax.broadcasted_iota(I32, (2 * tq, LANES), 1)
    q_aug = jnp.concatenate([qq, jnp.where(lane2 == 0, shift, 0.0).astype(BF16)], axis=1)
    one_lane = jnp.where(lax.broadcasted_iota(I32, (TK, LANES), 1) == 0, 1.0, 0.0).astype(BF16)
    scores = lambda c: _dot_nt(jnp.concatenate([k_chunk(c), one_lane], axis=1), q_aug)
    acc = jnp.zeros((LANES, 2 * tq), F32)
    l = jnp.zeros((1, 2 * tq), F32)
    st_next = scores(chunks[0])
    for i, c in enumerate(chunks):
        st = st_next
        if i + 1 < len(chunks):
            st_next = scores(chunks[i + 1])
        p = jnp.exp2(st)
        l = l + jnp.sum(p, axis=0, keepdims=True)
        acc = acc + _dot(vt_ref[0, c], p.astype(BF16))
    finish(acc, l)
    underflowed = jnp.sum(jnp.where(l > SOFTMAX_MIN_SUM, 0.0, 1.0)) > 0.0

    @pl.when(underflowed)
    def _():
        m = jnp.full((1, 2 * tq), NEG_BIG, F32)
        l = jnp.zeros((1, 2 * tq), F32)
        acc = jnp.zeros((LANES, 2 * tq), F32)
        for c in chunks:
            st = _dot_nt(k_chunk(c), qq)
            m_new = jnp.maximum(m, jnp.max(st, axis=0, keepdims=True))
            alpha = jnp.exp2(m - m_new)
            p = jnp.exp2(st - m_new)
            l = l * alpha + jnp.sum(p, axis=0, keepdims=True)
            acc = acc * alpha + _dot(vt_ref[0, c], p.astype(BF16))
            m = m_new
        finish(acc, l)


def _attn_call(q, k, vt, lam_vecs, subln_g, lambda_init, *, tq, q_row0, n_rows, chunks, name):
    b, t, d = q.shape
    nt = t // TM
    q0 = q_row0 // tq
    return pl.pallas_call(
        functools.partial(_attn_kernel, chunks=chunks, lambda_init=lambda_init, tq=tq),
        out_shape=jax.ShapeDtypeStruct((b, n_rows, d), BF16),
        grid=(b, DIFF_HEADS, n_rows // tq),
        in_specs=[
            pl.BlockSpec((1, tq, LANES), lambda bi, h, qi: (bi, q0 + qi, h)),
            pl.BlockSpec((1, t, LANES), lambda bi, h, qi: (bi, 0, h)),
            pl.BlockSpec((1, nt, LANES, TM), lambda bi, h, qi: (bi, 0, h, 0)),
            pl.BlockSpec((4, DIFF_HEAD_DIM), lambda bi, h, qi: (0, 0)),
            pl.BlockSpec((LANES, 1), lambda bi, h, qi: (0, 0)),
        ],
        out_specs=pl.BlockSpec((1, tq, LANES), lambda bi, h, qi: (bi, qi, h)),
        scratch_shapes=[pltpu.VMEM((8, LANES), F32)],
        compiler_params=_cparams(("parallel", "parallel", "arbitrary")),
        name=name,
    )(q, k, vt, lam_vecs, subln_g.reshape(LANES, 1))


def _route(logits_t, rb, ri_ref, rw_ref, cnt_ref, first):
    tm = logits_t.shape[1]
    aff = 1.0 / (1.0 + jnp.exp(-logits_t))
    sel = aff + rb
    row = lambda a, e: a[e:e + 1, :]
    gscore = []
    for g in range(N_GROUPS):
        a, b, c, d = (row(sel, g * EXPERTS_PER_GROUP + j) for j in range(EXPERTS_PER_GROUP))
        hi1, lo1, hi2, lo2 = jnp.maximum(a, b), jnp.minimum(a, b), jnp.maximum(c, d), jnp.minimum(c, d)
        gscore.append(jnp.maximum(hi1, hi2) + jnp.maximum(jnp.minimum(hi1, hi2), jnp.maximum(lo1, lo2)))
    best, grp = gscore[0], jnp.zeros((1, tm), I32)
    for g in range(1, N_GROUPS):
        better = gscore[g] > best
        best = jnp.where(better, gscore[g], best)
        grp = jnp.where(better, g, grp)
    in_sel, in_aff = [], []
    for j in range(EXPERTS_PER_GROUP):
        s_j, a_j = row(sel, j), row(aff, j)
        for g in range(1, N_GROUPS):
            s_j = jnp.where(grp == g, row(sel, g * EXPERTS_PER_GROUP + j), s_j)
            a_j = jnp.where(grp == g, row(aff, g * EXPERTS_PER_GROUP + j), a_j)
        in_sel.append(s_j)
        in_aff.append(a_j)
    v0, i0, a0 = in_sel[0], jnp.zeros((1, tm), I32), in_aff[0]
    for j in range(1, EXPERTS_PER_GROUP):
        better = in_sel[j] > v0
        v0 = jnp.where(better, in_sel[j], v0)
        i0 = jnp.where(better, j, i0)
        a0 = jnp.where(better, in_aff[j], a0)
    v1, i1, a1 = jnp.full((1, tm), -jnp.inf, F32), jnp.zeros((1, tm), I32), jnp.zeros((1, tm), F32)
    for j in range(EXPERTS_PER_GROUP):
        better = (in_sel[j] > v1) & (i0 != j)
        v1 = jnp.where(better, in_sel[j], v1)
        i1 = jnp.where(better, j, i1)
        a1 = jnp.where(better, in_aff[j], a1)
    e0 = grp * EXPERTS_PER_GROUP + i0
    e1 = grp * EXPERTS_PER_GROUP + i1
    wsum = a0 + a1
    erow = lax.broadcasted_iota(I32, (N_EXPERTS, tm), 0)
    oh0 = (erow == e0).astype(F32)
    oh1 = (erow == e1).astype(F32)
    both = oh0 + oh1
    both_bf = both.astype(BF16)
    tri = (lax.broadcasted_iota(I32, (tm, tm), 0) < lax.broadcasted_iota(I32, (tm, tm), 1)).astype(BF16)
    before = _dot(both_bf, tri)
    round_up = lambda n: jnp.floor((n + (ROW_GROUP - 1.0)) * (1.0 / ROW_GROUP)) * ROW_GROUP
    run_col = round_up(jnp.sum(both, axis=1, keepdims=True))
    e_r = lax.broadcasted_iota(I32, (N_EXPERTS, N_EXPERTS), 0)
    e_c = lax.broadcasted_iota(I32, (N_EXPERTS, N_EXPERTS), 1)
    off_col = _dot((e_c < e_r).astype(BF16), jnp.broadcast_to(run_col, (N_EXPERTS, LANES)).astype(BF16))[:, 0:1]
    base = off_col + before
    s0 = jnp.sum(oh0 * base, axis=0, keepdims=True)
    s1 = jnp.sum(oh1 * base, axis=0, keepdims=True)
    both_pad = jnp.concatenate([both_bf, jnp.zeros((LANES - N_EXPERTS, tm), BF16)], axis=0)
    run_row = round_up(_dot_nt(jnp.ones((8, tm), BF16), both_pad))
    l_r = lax.broadcasted_iota(I32, (LANES, LANES), 0)
    l_c = lax.broadcasted_iota(I32, (LANES, LANES), 1)
    off_row = _dot(run_row.astype(BF16), (l_r < l_c).astype(BF16))
    lane = lax.broadcasted_iota(I32, (1, LANES), 1)
    groups = run_row[0:1] * (1.0 / ROW_GROUP)
    groups = jnp.where(lane == N_EXPERTS, jnp.sum(groups, axis=1, keepdims=True), groups)

    @pl.when(first)
    def _():
        cnt_ref[...] = jnp.zeros_like(cnt_ref)

    zero_half = jnp.zeros((1, tm - LANES), I32)
    ri_ref[0, 0:1, :] = s0.astype(I32)
    ri_ref[0, 1:2, :] = s1.astype(I32)
    ri_ref[0, 2:3, :] = e0
    ri_ref[0, 3:4, :] = e1
    for r, v in ((4, groups), (5, cnt_ref[0:1, :]), (6, off_row[0:1])):
        ri_ref[0, r:r + 1, 0:LANES] = v.astype(I32)
        ri_ref[0, r:r + 1, LANES:] = zero_half
    ri_ref[0, 7:8, :] = jnp.zeros((1, tm), I32)
    cnt_ref[...] = cnt_ref[...] + run_row
    rw_ref[0, 0:1, :] = a0 / wsum
    rw_ref[0, 1:2, :] = a1 / wsum
    rw_ref[0, 2:8, :] = jnp.zeros((6, tm), F32)


def _post_mixer(y, x, mod, lng, lnb, rwt, rb, x1_ref, ri_ref, rw_ref, cnt_ref, first, d):
    x1 = _layer_norm(ALPHA * x + mod[:, 2 * d:3 * d] * y, lng, lnb)
    x1_ref[0] = x1
    tok = x1 * (1.0 + mod[:, 4 * d:5 * d]) + mod[:, 3 * d:4 * d]
    t_hi = tok.astype(BF16)
    t_mid = (tok - t_hi.astype(F32)).astype(BF16)
    w_hi, w_mid = rwt[0], rwt[1]
    logits = _dot(t_hi, w_hi) + (_dot(t_hi, w_mid) + _dot(t_mid, w_hi))
    _route(logits.T[0:N_EXPERTS, :], rb, ri_ref, rw_ref, cnt_ref, first)


def _attn_out_kernel(ol_ref, oc_ref, xl_ref, xc_ref, mod_ref, wo_ref, lng_ref, lnb_ref, rwt_ref, rb_ref,
                     x1_ref, ri_ref, rw_ref, cnt_ref, *, d, n_lat):
    first = (pl.program_id(0) == 0) & (pl.program_id(1) == 0)
    latent = pl.program_id(1) < n_lat
    o = jnp.where(latent, ol_ref[0], oc_ref[0])
    x = jnp.where(latent, xl_ref[0], xc_ref[0])
    y = _dot(o, wo_ref[...])
    _post_mixer(y, x, mod_ref[0], lng_ref[...], lnb_ref[...], rwt_ref[...], rb_ref[...],
                x1_ref, ri_ref, rw_ref, cnt_ref, first, d)


def _route_out_shapes(b, n_tiles, t_out, d):
    return (jax.ShapeDtypeStruct((b, t_out, d), F32),
            jax.ShapeDtypeStruct((b * n_tiles, 8, TM), I32),
            jax.ShapeDtypeStruct((b * n_tiles, 8, TM), F32),
            jax.ShapeDtypeStruct((8, LANES), F32))


def _route_out_specs(n_tiles, d):
    return (pl.BlockSpec((1, TM, d), lambda bi, j: (bi, j, 0)),
            pl.BlockSpec((1, 8, TM), lambda bi, j: (bi * n_tiles + j, 0, 0)),
            pl.BlockSpec((1, 8, TM), lambda bi, j: (bi * n_tiles + j, 0, 0)),
            pl.BlockSpec((8, LANES), lambda bi, j: (0, 0)))


def _attn_out_call(o_lat, o_ctx, x, ctx, mod, wo, lng, lnb, rwt, rb, ctx_row):
    b, s_len, d = x.shape
    n_lat = s_len // TM
    nt = n_lat + 1
    row = lambda a: pl.BlockSpec(a.shape, lambda bi, j: (0,) * a.ndim)
    lat = pl.BlockSpec((1, TM, d), lambda bi, j: (bi, jnp.minimum(j, n_lat - 1), 0))
    cxt = pl.BlockSpec((1, TM, d), lambda bi, j: (bi, 0, 0))
    return pl.pallas_call(
        functools.partial(_attn_out_kernel, d=d, n_lat=n_lat),
        out_shape=_route_out_shapes(b, nt, nt * TM, d),
        grid=(b, nt),
        in_specs=[
            lat, cxt, lat, cxt,
            pl.BlockSpec((1, 1, 6 * d), lambda bi, j: (jnp.where(j < n_lat, bi, ctx_row), 0, 0)),
            row(wo), row(lng), row(lnb), row(rwt), row(rb),
        ],
        out_specs=_route_out_specs(nt, d),
        compiler_params=_cparams(("arbitrary", "arbitrary")),
        name="attn_out_route",
    )(o_lat, o_ctx, x, ctx, mod, wo, lng, lnb, rwt, rb)


def _group_copy(src, r_src, dst, r_dst, sem):
    return pltpu.make_async_copy(src.at[pl.ds(pl.multiple_of(r_src, ROW_GROUP), ROW_GROUP)],
                                 dst.at[pl.ds(pl.multiple_of(r_dst, ROW_GROUP), ROW_GROUP)], sem)


def _tab(tab_ref, tile, row, e):
    return tab_ref[(tile * TAB_ROWS + row) * TAB_LANES + e]


def _run_copies(tab_ref, pad_ref, tile, stage, hbm, sem, to_hbm):
    for e in range(N_EXPERTS):
        s0 = _tab(tab_ref, tile, 2, e)
        h0 = pad_ref[e] + _tab(tab_ref, tile, 1, e)

        def issue(g, carry):
            s, h = s0 + g * ROW_GROUP, h0 + g * ROW_GROUP
            (_group_copy(stage, s, hbm, h, sem) if to_hbm else _group_copy(hbm, h, stage, s, sem)).start()
            return carry

        lax.fori_loop(0, _tab(tab_ref, tile, 0, e), issue, 0)


def _run_waits(tab_ref, tile, stage, hbm, sem, to_hbm):
    def drain(g, carry):
        (_group_copy(stage, 0, hbm, 0, sem) if to_hbm else _group_copy(hbm, 0, stage, 0, sem)).wait()
        return carry

    lax.fori_loop(0, _tab(tab_ref, tile, 0, N_EXPERTS), drain, 0)


def _dispatch_kernel(tab_ref, pad_ref, fill_ref, x_ref, mod_ref, ri_ref, xs_ref, stage_ref, zero_ref, sems, sem_z,
                     *, d, n_steps):
    step = pl.program_id(0) * pl.num_programs(1) + pl.program_id(1)
    buf = step % 2
    mod = mod_ref[0]
    tok = (x_ref[0] * (1.0 + mod[:, 4 * d:5 * d]) + mod[:, 3 * d:4 * d]).astype(BF16)
    slot = lax.broadcasted_iota(I32, (STAGE_ROWS, TM), 0)
    ri = ri_ref[0]
    pick = jnp.where((slot == ri[0:1, :]) | (slot == ri[1:2, :]), 1.0, 0.0).astype(BF16)
    for bsel in range(2):
        @pl.when(buf == bsel)
        def _():
            stage_ref[bsel] = _dot(pick, tok)
            _run_copies(tab_ref, pad_ref, step, stage_ref.at[bsel], xs_ref, sems.at[bsel], True)

    for bsel in range(2):
        @pl.when((buf != bsel) & (step > 0))
        def _():
            _run_waits(tab_ref, step - 1, stage_ref.at[bsel], xs_ref, sems.at[bsel], True)

    @pl.when(step == n_steps - 1)
    def _():
        for bsel in range(2):
            @pl.when(buf == bsel)
            def _():
                _run_waits(tab_ref, step, stage_ref.at[bsel], xs_ref, sems.at[bsel], True)

        zero_ref[...] = jnp.zeros_like(zero_ref)
        for e in range(N_EXPERTS + 1):
            lo, hi = fill_ref[2 * e], fill_ref[2 * e + 1]

            def fill(g, carry):
                _group_copy(zero_ref, 0, xs_ref, g * ROW_GROUP, sem_z).start()
                return carry

            def fill_wait(g, carry):
                _group_copy(zero_ref, 0, xs_ref, 0, sem_z).wait()
                return carry

            lax.fori_loop(lo, hi, fill, 0)
            lax.fori_loop(lo, hi, fill_wait, 0)


def _dispatch_call(tab, pad_start, fill, x1, mod, ri, n_tiles, n_pad, ctx_row, n_lat):
    b, t, d = x1.shape
    grid_spec = pltpu.PrefetchScalarGridSpec(
        num_scalar_prefetch=3,
        grid=(b, n_tiles),
        in_specs=[
            pl.BlockSpec((1, TM, d), lambda bi, j, *_: (bi, j, 0)),
            pl.BlockSpec((1, 1, 6 * d), lambda bi, j, *_: (jnp.where(j < n_lat, bi, ctx_row), 0, 0)),
            pl.BlockSpec((1, 8, TM), lambda bi, j, *_: (bi * n_tiles + j, 0, 0)),
        ],
        out_specs=pl.BlockSpec(memory_space=pl.ANY),
        scratch_shapes=[pltpu.VMEM((2, STAGE_ROWS, d), F32), pltpu.VMEM((ROW_GROUP, d), F32),
                        pltpu.SemaphoreType.DMA((2,)), pltpu.SemaphoreType.DMA],
    )
    return pl.pallas_call(
        functools.partial(_dispatch_kernel, d=d, n_steps=b * n_tiles),
        out_shape=jax.ShapeDtypeStruct((n_pad, d), F32),
        grid_spec=grid_spec,
        compiler_params=_cparams(("arbitrary", "arbitrary")),
        name="moe_dispatch",
    )(tab, pad_start, fill, x1, mod, ri)


def _ffn_kernel(be_ref, nu_ref, xs_ref, wg_ref, wu_ref, wd_ref, ys_ref, wg_s, wu_s, wd_s):
    i = pl.program_id(0)

    @pl.when(i < nu_ref[0])
    def _():
        prev = be_ref[jnp.maximum(i - 1, 0)]

        @pl.when((i == 0) | (be_ref[i] != prev))
        def _():
            wg_s[...] = wg_ref[0, 0].astype(BF16)
            wu_s[...] = wu_ref[0, 0].astype(BF16)
            wd_s[...] = wd_ref[0, 0].astype(BF16)

        x = xs_ref[...].astype(BF16)
        g = _dot(x, wg_s[...])
        u = _dot(x, wu_s[...])
        ys_ref[...] = _dot((_silu(g) * u).astype(BF16), wd_s[...])

    @pl.when(i >= nu_ref[0])
    def _():
        ys_ref[...] = jnp.zeros_like(ys_ref)


def _ffn_call(block_expert, n_used, xs, w_gate, w_up, w_down, layer):
    n_pad, d = xs.shape
    de = w_gate.shape[-1]
    n_blocks = n_pad // MOE_BLK
    blk = lambda i, be, nu: (jnp.minimum(i, nu[0] - 1), 0)
    exp = lambda i, be, nu: (layer, be[jnp.minimum(i, nu[0] - 1)], 0, 0)
    grid_spec = pltpu.PrefetchScalarGridSpec(
        num_scalar_prefetch=2,
        grid=(n_blocks,),
        in_specs=[
            pl.BlockSpec((MOE_BLK, d), blk),
            pl.BlockSpec((1, 1, d, de), exp),
            pl.BlockSpec((1, 1, d, de), exp),
            pl.BlockSpec((1, 1, de, d), exp),
        ],
        out_specs=pl.BlockSpec((MOE_BLK, d), lambda i, be, nu: (i, 0)),
        scratch_shapes=[pltpu.VMEM((d, de), BF16), pltpu.VMEM((d, de), BF16), pltpu.VMEM((de, d), BF16)],
    )
    return pl.pallas_call(
        _ffn_kernel,
        out_shape=jax.ShapeDtypeStruct((n_pad, d), F32),
        grid_spec=grid_spec,
        compiler_params=_cparams(("arbitrary",)),
        name="moe_ffn",
    )(block_expert, n_used, xs, w_gate, w_up, w_down)


def _combine_kernel(tab_ref, pad_ref, x_ref, mod_ref, rit_ref, rw_ref, lng_ref, lnb_ref, ys_ref, o_ref,
                    stage_ref, sems, *, d, n_steps):
    step = pl.program_id(0) * pl.num_programs(1) + pl.program_id(1)
    buf = step % 2

    @pl.when(step == 0)
    def _():
        stage_ref[...] = jnp.zeros_like(stage_ref)
        _run_copies(tab_ref, pad_ref, step, stage_ref.at[0], ys_ref, sems.at[0], False)

    for bsel in range(2):
        @pl.when((buf != bsel) & (step + 1 < n_steps))
        def _():
            _run_copies(tab_ref, pad_ref, step + 1, stage_ref.at[bsel], ys_ref, sems.at[bsel], False)

    rit = rit_ref[0]
    lane = lax.broadcasted_iota(I32, (TM, STAGE_ROWS), 1)
    pick0 = jnp.where(lane == rit[:, 0:1], 1.0, 0.0).astype(BF16)
    pick1 = jnp.where(lane == rit[:, 1:2], 1.0, 0.0).astype(BF16)
    w = rw_ref[0]
    mod = mod_ref[0]
    for bsel in range(2):
        @pl.when(buf == bsel)
        def _():
            _run_waits(tab_ref, step, stage_ref.at[bsel], ys_ref, sems.at[bsel], False)
            stage = stage_ref[bsel].astype(BF16)
            moe = w[:, 0:1] * _dot(pick0, stage) + w[:, 1:2] * _dot(pick1, stage)
            o_ref[0] = _layer_norm(ALPHA * x_ref[0] + mod[:, 5 * d:6 * d] * moe, lng_ref[...], lnb_ref[...])


def _combine_call(tab, pad_start, x1, mod, ri_t, rw_t, lng, lnb, ys, n_tiles, ctx_row, n_lat):
    b, _, d = x1.shape
    row = lambda a: pl.BlockSpec(a.shape, lambda bi, j, *_: (0,) * a.ndim)
    grid_spec = pltpu.PrefetchScalarGridSpec(
        num_scalar_prefetch=2,
        grid=(b, n_tiles),
        in_specs=[
            pl.BlockSpec((1, TM, d), lambda bi, j, *_: (bi, j, 0)),
            pl.BlockSpec((1, 1, 6 * d), lambda bi, j, *_: (jnp.where(j < n_lat, bi, ctx_row), 0, 0)),
            pl.BlockSpec((1, TM, 8), lambda bi, j, *_: (bi * n_tiles + j, 0, 0)),
            pl.BlockSpec((1, TM, 8), lambda bi, j, *_: (bi * n_tiles + j, 0, 0)),
            row(lng), row(lnb),
            pl.BlockSpec(memory_space=pl.ANY),
        ],
        out_specs=pl.BlockSpec((1, TM, d), lambda bi, j, *_: (bi, j, 0)),
        scratch_shapes=[pltpu.VMEM((2, STAGE_ROWS, d), F32), pltpu.SemaphoreType.DMA((2,))],
    )
    return pl.pallas_call(
        functools.partial(_combine_kernel, d=d, n_steps=b * n_tiles),
        out_shape=jax.ShapeDtypeStruct((b, n_tiles * TM, d), F32),
        grid_spec=grid_spec,
        compiler_params=_cparams(("arbitrary", "arbitrary")),
        name="moe_combine",
    )(tab, pad_start, x1, mod, ri_t, rw_t, lng, lnb, ys)


def _moe(x1, mod, ri, rw, counts, lng, lnb, w_gate, w_up, w_down, layer, n_tiles, ctx_row, n_lat):
    b, _, d = x1.shape
    n_steps = b * n_tiles
    n_rows_max = 2 * n_steps * TM + (ROW_GROUP - 1) * N_EXPERTS * n_steps
    n_pad = -(-n_rows_max // MOE_BLK) * MOE_BLK + N_EXPERTS * MOE_BLK
    n_blocks = n_pad // MOE_BLK
    cnt = counts[0, :N_EXPERTS].astype(I32)
    padded = (cnt + MOE_BLK - 1) // MOE_BLK * MOE_BLK
    pad_end = jnp.cumsum(padded)
    pad_start = jnp.concatenate([jnp.zeros((1,), I32), pad_end]).astype(I32)
    block_row0 = jnp.arange(n_blocks, dtype=I32) * MOE_BLK
    block_expert = jnp.minimum(
        jnp.sum((pad_end[None, :] <= block_row0[:, None]).astype(I32), axis=1), N_EXPERTS - 1).astype(I32)
    n_used = (pad_end[-1:] // MOE_BLK).astype(I32)
    fill_lo = jnp.concatenate([pad_start[:-1] + cnt, pad_end[-1:]]) // ROW_GROUP
    fill_hi = jnp.concatenate([pad_end, jnp.full((1,), n_pad, I32)]) // ROW_GROUP
    fill = jnp.stack([fill_lo, fill_hi], axis=1).reshape(-1).astype(I32)
    tab = ri[:, 4:4 + TAB_ROWS, :TAB_LANES].reshape(-1)
    xs = _dispatch_call(tab, pad_start, fill, x1, mod, ri, n_tiles, n_pad, ctx_row, n_lat)
    ys = _ffn_call(block_expert, n_used, xs, w_gate, w_up, w_down, layer)
    ri_t, rw_t = jnp.swapaxes(ri, 1, 2), jnp.swapaxes(rw, 1, 2)
    return _combine_call(tab, pad_start, x1, mod, ri_t, rw_t, lng, lnb, ys, n_tiles, ctx_row, n_lat)


def _gla_proj_kernel(x_ref, mod_ref, win_ref, wgd_ref, wgu_ref, bg_ref, q_ref, k_ref, v_ref, r_ref, g_ref, *, d):
    mod = mod_ref[0]
    h = (x_ref[0] * (1.0 + mod[:, d:2 * d]) + mod[:, 0:d]).astype(BF16)
    dk = GLA_HEADS * GLA_DK
    p = _dot(h, win_ref[...])
    q_ref[0] = (p[:, 0:dk] * GLA_DK ** -0.5).astype(BF16)
    k_ref[0] = p[:, dk:2 * dk].astype(BF16)
    v_ref[0] = p[:, 2 * dk:2 * dk + d].astype(BF16)
    r_ref[0] = p[:, 2 * dk + d:].astype(BF16)
    low = _dot(h, wgd_ref[...]).astype(BF16)
    pre = _dot(low, wgu_ref[...]) + bg_ref[...]
    g_ref[0] = (jnp.minimum(pre, 0.0) - jnp.log1p(jnp.exp(-jnp.abs(pre)))) * (1.0 / GLA_TAU)


def _gla_proj_call(xs, mod, win, wgd, wgu, bg, ctx_row):
    b, t, d = xs.shape
    nt = t // TM
    n_lat = nt - 1
    dk = GLA_HEADS * GLA_DK
    row = lambda a: pl.BlockSpec(a.shape, lambda bi, j: (0,) * a.ndim)
    tile = lambda n: pl.BlockSpec((1, TM, n), lambda bi, j: (bi, j, 0))
    return pl.pallas_call(
        functools.partial(_gla_proj_kernel, d=d),
        out_shape=(jax.ShapeDtypeStruct((b, t, dk), BF16), jax.ShapeDtypeStruct((b, t, dk), BF16),
                   jax.ShapeDtypeStruct((b, t, d), BF16), jax.ShapeDtypeStruct((b, t, d), BF16),
                   jax.ShapeDtypeStruct((b, t, 2 * dk), F32)),
        grid=(b, nt),
        in_specs=[tile(d), pl.BlockSpec((1, 1, 6 * d), lambda bi, j: (jnp.where(j < n_lat, bi, ctx_row), 0, 0)),
                  row(win), row(wgd), row(wgu), row(bg)],
        out_specs=(tile(dk), tile(dk), tile(d), tile(d), tile(2 * dk)),
        compiler_params=_cparams(("parallel", "parallel")),
        name="gla_proj",
    )(xs, mod, win, wgd, wgu, bg)


def _split3(x):
    hi = x.astype(BF16)
    r1 = x - hi.astype(F32)
    mid = r1.astype(BF16)
    lo = (r1 - mid.astype(F32)).astype(BF16)
    return hi, mid, lo


class _GlaMasks:
    def __init__(self, reverse):
        L = GLA_CHUNK
        t = lax.broadcasted_iota(I32, (L, L), 0)
        s = lax.broadcasted_iota(I32, (L, L), 1)
        seen = (s >= t) if reverse else (s <= t)
        self.tri = seen.astype(BF16)
        blk = lambda r: jnp.right_shift(r, GLA_SUB.bit_length() - 1)
        dist = (blk(s) - blk(t)) if reverse else (blk(t) - blk(s))
        self.pairs = [seen & (dist == 0)] + [dist == n for n in range(1, L // GLA_SUB)]


def _gla_chunk_matmuls(q_ref, k_ref, v_ref, b_all, rows, h, state_ref, reverse):
    L, SB, DK = GLA_CHUNK, GLA_SUB, GLA_DK
    nb = L // SB
    kcols = slice(h * DK, (h + 1) * DK)
    vcols = slice(h * GLA_DV, (h + 1) * GLA_DV)
    bh = b_all[:, kcols]
    zero = jnp.zeros((1, DK), F32)
    earlier = lambda i, n: i + n if reverse else i - n

    def start_ref(i):
        if not 0 <= i < nb or i == (nb - 1 if reverse else 0):
            return zero
        r = (i + 1) * SB if reverse else i * SB - 1
        return bh[r:r + 1, :]

    def end_ref(i):
        r = i * SB if reverse else i * SB + SB - 1
        return bh[r:r + 1, :]

    per_block = lambda f: jnp.concatenate([jnp.broadcast_to(f(i), (SB, DK)) for i in range(nb)], axis=0)
    bs, be = per_block(start_ref), per_block(end_ref)
    b_last = end_ref(0 if reverse else nb - 1)
    q = q_ref[0, rows, kcols].astype(F32)
    k = k_ref[0, rows, kcols].astype(F32)
    v = v_ref[0, rows, vcols]
    scaled = lambda x, e: (x * jnp.exp(e)).astype(BF16)
    q_own = scaled(q, bh - bs)
    k_own, k_end = scaled(k, bs - bh), scaled(k, be - bh)
    pieces = [_dot_nt(q_own, k_own), _dot_nt(q_own, k_end)]
    for n in range(2, nb):
        pieces.append(_dot_nt(scaled(q, bh - per_block(lambda i: start_ref(earlier(i, n - 1)))), k_end))
    state_t = state_ref[h]
    o_inter = _dot_nt(scaled(q, bh), state_t.astype(BF16))
    update = _dot_tn(v, scaled(k, b_last - bh))
    return pieces, o_inter, update, b_last, state_t, v, vcols


def _gla_scan_kernel(qf, kf, vf, gf, qb, kb, vb, gb, of_ref, ob_ref, sf_ref, sb_ref):
    @pl.when(pl.program_id(1) == 0)
    def _():
        sf_ref[...] = jnp.zeros_like(sf_ref)
        sb_ref[...] = jnp.zeros_like(sb_ref)

    L = GLA_CHUNK
    n_chunks = TM // L
    dirs = ((qf, kf, vf, gf, of_ref, sf_ref, False, _GlaMasks(False)),
            (qb, kb, vb, gb, ob_ref, sb_ref, True, _GlaMasks(True)))
    for step in range(n_chunks):
        work = []
        for q_ref, k_ref, v_ref, g_ref, o_ref, state_ref, reverse, masks in dirs:
            c = n_chunks - 1 - step if reverse else step
            rows = slice(c * L, (c + 1) * L)
            b_all = sum(_dot(masks.tri, part) for part in _split3(g_ref[0, rows, :]))
            for h in range(GLA_HEADS):
                work.append((o_ref, state_ref, rows, h, masks,
                             _gla_chunk_matmuls(q_ref, k_ref, v_ref, b_all, rows, h, state_ref, reverse)))
        for o_ref, state_ref, rows, h, masks, (pieces, o_inter, update, b_last, state_t, v, vcols) in work:
            a = sum(jnp.where(m, p, 0.0) for m, p in zip(masks.pairs, pieces)).astype(BF16)
            o_ref[0, rows, vcols] = _dot(a, v) + o_inter
            state_ref[h] = jnp.exp(b_last) * state_t + update


def _gla_scan_call(q, k, v, g):
    b, t, d = v.shape
    nt = t // TM
    n_lat = nt - 1
    dk = GLA_HEADS * GLA_DK
    fwd = lambda j: jnp.where(j == 0, n_lat, j - 1)
    bwd = lambda j: jnp.where(j == 0, n_lat, n_lat - j)
    spec = lambda n, idx, col: pl.BlockSpec((1, TM, n), lambda bi, j: (bi, idx(j), col))
    return pl.pallas_call(
        _gla_scan_kernel,
        out_shape=(jax.ShapeDtypeStruct((b, n_lat * TM, d), F32), jax.ShapeDtypeStruct((b, n_lat * TM, d), F32)),
        grid=(b, nt),
        in_specs=[spec(dk, fwd, 0), spec(dk, fwd, 0), spec(d, fwd, 0), spec(dk, fwd, 0),
                  spec(dk, bwd, 0), spec(dk, bwd, 0), spec(d, bwd, 0), spec(dk, bwd, 1)],
        out_specs=(pl.BlockSpec((1, TM, d), lambda bi, j: (bi, jnp.maximum(j - 1, 0), 0)),
                   pl.BlockSpec((1, TM, d), lambda bi, j: (bi, n_lat - jnp.maximum(j, 1), 0))),
        scratch_shapes=[pltpu.VMEM((GLA_HEADS, GLA_DV, GLA_DK), F32), pltpu.VMEM((GLA_HEADS, GLA_DV, GLA_DK), F32)],
        compiler_params=_cparams(("parallel", "arbitrary")),
        name="gla_scan",
    )(q, k, v, g, q, k, v, g)


def _gla_out_kernel(of_ref, ob_ref, r_ref, x_ref, mod_ref, ng_ref, wo_ref, lng_ref, lnb_ref, rwt_ref, rb_ref,
                    x1_ref, ri_ref, rw_ref, cnt_ref, *, d):
    first = (pl.program_id(0) == 0) & (pl.program_id(1) == 0)
    o = of_ref[0] + ob_ref[0]
    r = r_ref[0].astype(F32)
    gate = _silu(r)
    ng = ng_ref[...]
    parts = []
    for h in range(GLA_HEADS):
        cols = slice(h * GLA_DV, (h + 1) * GLA_DV)
        oh = o[:, cols]
        ms = jnp.mean(oh * oh, axis=-1, keepdims=True)
        parts.append((oh * lax.rsqrt(ms + LN_EPS) * ng * gate[:, cols]).astype(BF16))
    y = _dot(jnp.concatenate(parts, axis=-1), wo_ref[...])
    _post_mixer(y, x_ref[0], mod_ref[0], lng_ref[...], lnb_ref[...], rwt_ref[...], rb_ref[...],
                x1_ref, ri_ref, rw_ref, cnt_ref, first, d)


def _gla_out_call(o_f, o_b, r, xs, mod, ng, wo, lng, lnb, rwt, rb):
    b, s_len, d = o_f.shape
    n_lat = s_len // TM
    row = lambda a: pl.BlockSpec(a.shape, lambda bi, j: (0,) * a.ndim)
    tile = pl.BlockSpec((1, TM, d), lambda bi, j: (bi, j, 0))
    return pl.pallas_call(
        functools.partial(_gla_out_kernel, d=d),
        out_shape=_route_out_shapes(b, n_lat, s_len, d),
        grid=(b, n_lat),
        in_specs=[tile, tile, tile, tile, pl.BlockSpec((1, 1, 6 * d), lambda bi, j: (bi, 0, 0)),
                  row(ng), row(wo), row(lng), row(lnb), row(rwt), row(rb)],
        out_specs=_route_out_specs(n_lat, d),
        compiler_params=_cparams(("arbitrary", "arbitrary")),
        name="gla_out_route",
    )(o_f, o_b, r, xs, mod, ng, wo, lng, lnb, rwt, rb)


def kernel(x, c, ctx, c_ctx, ada_w, ada_b, ln_g, ln_b, attn_w_qkv, attn_w_o, attn_lambda, attn_subln_g, gla_w_in, gla_wg_down, gla_wg_up, gla_bg, gla_norm_g, gla_w_o, router_w, router_b, moe_w_gate, moe_w_up, moe_w_down):
    b, s_len, d = x.shape
    c_len = ctx.shape[1]
    assert ada_w.shape[0] == DEPTH and c_len == TM and s_len % TQ == 0 and TQ % TM == 0 and s_len % GRID_W == 0
    assert d == DIFF_HEADS * 2 * DIFF_HEAD_DIM == GLA_HEADS * GLA_DV
    assert STAGE_ROWS % LANES == 0 and STAGE_ROWS >= 2 * TM + (ROW_GROUP - 1) * N_EXPERTS
    n_lat = s_len // TM
    nt = n_lat + 1
    ctx_row = b

    cond_rows = -(-(b + 1) // 8) * 8
    cond = jnp.concatenate([c, c_ctx[None, :], jnp.zeros((cond_rows - b - 1, d), F32)], axis=0)
    mods = _ada_call(cond, ada_w, ada_b)
    rw_pad = jnp.concatenate([router_w, jnp.zeros((d, LANES - N_EXPERTS), F32)], axis=1)
    rw_hi = rw_pad.astype(BF16)
    rwt = jnp.stack([rw_hi, (rw_pad - rw_hi.astype(F32)).astype(BF16)])
    rb = router_b.reshape(N_EXPERTS, 1)

    mod0 = mods[0].reshape(cond_rows, 1, 6 * d)
    wqkv = attn_w_qkv[0]
    wqk = wqkv[:, :2 * d].astype(BF16)
    wvt = wqkv[:, 2 * d:].T.astype(BF16)
    tables = tuple(jnp.asarray(t) for t in _rope_tables(s_len, c_len))
    q, k, vt = _qkv_call(x, ctx, mod0, wqk, wvt, tables, ctx_row)
    lambda_init = 0.8 - 0.6 * math.exp(-0.3 * 0)
    attn = functools.partial(_attn_call, q, k, vt, attn_lambda[0], attn_subln_g[0], lambda_init)
    o_lat = attn(tq=TQ, q_row0=0, n_rows=s_len, chunks=tuple(range(nt)), name="diff_attn_lat")
    o_ctx = attn(tq=c_len, q_row0=s_len, n_rows=c_len, chunks=(n_lat,), name="diff_attn_ctx")
    x1, ri, rw, counts = _attn_out_call(o_lat, o_ctx, x, ctx, mod0, attn_w_o[0].astype(BF16),
                                        ln_g[0, 0][None], ln_b[0, 0][None], rwt, rb, ctx_row)
    stream = _moe(x1, mod0, ri, rw, counts, ln_g[0, 1][None], ln_b[0, 1][None],
                  moe_w_gate, moe_w_up, moe_w_down, 0, nt, ctx_row, n_lat)

    mod1 = mods[1].reshape(cond_rows, 1, 6 * d)
    dk = GLA_HEADS * GLA_DK
    wgd = jnp.concatenate([gla_wg_down[0, 0], gla_wg_down[0, 1],
                           jnp.zeros((d, LANES - 2 * GLA_GATE_RANK), F32)], axis=1).astype(BF16)
    wgu = jnp.zeros((LANES, 2 * dk), F32)
    wgu = wgu.at[0:GLA_GATE_RANK, 0:dk].set(gla_wg_up[0, 0])
    wgu = wgu.at[GLA_GATE_RANK:2 * GLA_GATE_RANK, dk:].set(gla_wg_up[0, 1]).astype(BF16)
    bg = gla_bg[0].reshape(1, 2 * dk)
    gq, gk, gv, gr, gg = _gla_proj_call(stream, mod1, gla_w_in[0].astype(BF16), wgd, wgu, bg, ctx_row)
    o_f, o_b = _gla_scan_call(gq, gk, gv, gg)
    x1, ri, rw, counts = _gla_out_call(o_f, o_b, gr, stream, mod1, gla_norm_g[0][None], gla_w_o[0].astype(BF16),
                                       ln_g[1, 0][None], ln_b[1, 0][None], rwt, rb)
    return _moe(x1, mod1, ri, rw, counts, ln_g[1, 1][None], ln_b[1, 1][None],
                moe_w_gate, moe_w_up, moe_w_down, 1, n_lat, ctx_row, n_lat)
```

```python
import functools
import math

import numpy as np
import jax
import jax.numpy as jnp
from jax import lax
from jax.experimental import pallas as pl
from jax.experimental.pallas import tpu as pltpu

F32 = jnp.float32
BF16 = jnp.bfloat16
I32 = jnp.int32

DEPTH = 2
ALPHA = (2.0 * DEPTH) ** 0.25
LN_EPS = 1e-5
GRID_W = 64
ROPE_BASE = 10000.0
DIFF_HEADS = 8
DIFF_HEAD_DIM = 64
ROT_FREQS = DIFF_HEAD_DIM // 4
GLA_HEADS = 4
GLA_DK = 128
GLA_DV = 256
GLA_GATE_RANK = 16
GLA_TAU = 16.0
GLA_CHUNK = 64
GLA_SUB = 16
N_EXPERTS = 16
N_GROUPS = 4
EXPERTS_PER_GROUP = 4
LOG2E = 1.4426950408889634

LANES = 128
TM = 256
TK = 256
TQ = 512
MOE_BLK = 256
ADA_TN = 1536
ROW_GROUP = 16
STAGE_ROWS = 768
TAB_ROWS, TAB_LANES = 3, 32
VMEM_LIMIT = 48 * 1024 * 1024
NEG_BIG = -1e30
SOFTMAX_HEADROOM = 64.0
BOUND_SLACK = 1.001
SOFTMAX_MIN_SUM = 2.0 ** -40


def _cparams(sem):
    return pltpu.CompilerParams(dimension_semantics=sem, vmem_limit_bytes=VMEM_LIMIT)


def _silu(x):
    return x * (1.0 / (1.0 + jnp.exp(-x)))


def _layer_norm(z, g, b):
    mu = jnp.mean(z, axis=-1, keepdims=True)
    zc = z - mu
    var = jnp.mean(zc * zc, axis=-1, keepdims=True)
    return zc * lax.rsqrt(var + LN_EPS) * g + b


def _dot(a, b):
    return jnp.dot(a, b, preferred_element_type=F32)


def _dot_nt(a, b, precision=None):
    return lax.dot_general(a, b, (((1,), (1,)), ((), ())), preferred_element_type=F32, precision=precision)


def _dot_tn(a, b):
    return lax.dot_general(a, b, (((0,), (0,)), ((), ())), preferred_element_type=F32)


def _ada_kernel(c_ref, w_ref, b_ref, o_ref):
    s = _silu(c_ref[...])
    o_ref[0] = jnp.dot(s, w_ref[0], preferred_element_type=F32, precision=lax.Precision.HIGHEST) + b_ref[0]


def _ada_call(cond, ada_w, ada_b):
    depth, d, n = ada_w.shape
    rows = cond.shape[0]
    return pl.pallas_call(
        _ada_kernel,
        out_shape=jax.ShapeDtypeStruct((depth, rows, n), F32),
        grid=(depth, n // ADA_TN),
        in_specs=[
            pl.BlockSpec((rows, d), lambda i, j: (0, 0)),
            pl.BlockSpec((1, d, ADA_TN), lambda i, j: (i, 0, j)),
            pl.BlockSpec((1, 1, ADA_TN), lambda i, j: (i, 0, j)),
        ],
        out_specs=pl.BlockSpec((1, rows, ADA_TN), lambda i, j: (i, 0, j)),
        compiler_params=_cparams(("parallel", "parallel")),
        name="ada_mod",
    )(cond, ada_w, ada_b.reshape(depth, 1, n))


def _qkv_kernel(xl_ref, xc_ref, mod_ref, wqk_ref, wvt_ref, cos_ref, sa_ref, sb_ref, q_ref, k_ref, vt_ref, *, d, n_lat):
    mod = mod_ref[0]
    x = jnp.where(pl.program_id(1) < n_lat, xl_ref[0], xc_ref[0])
    h = (x * (1.0 + mod[:, d:2 * d]) + mod[:, 0:d]).astype(BF16)
    qk = _dot(h, wqk_ref[...])
    vt_ref[0, 0] = _dot_nt(wvt_ref[...], h).astype(BF16)
    cos, sa, sb = cos_ref[...], sa_ref[...], sb_ref[...]
    q_scale = DIFF_HEAD_DIM ** -0.5 * LOG2E
    for s in range(2 * d // LANES):
        z = qk[:, s * LANES:(s + 1) * LANES]
        zr = z * cos + pltpu.roll(z, LANES - ROT_FREQS, axis=1) * sa + pltpu.roll(z, ROT_FREQS, axis=1) * sb
        if s < d // LANES:
            q_ref[0, :, s * LANES:(s + 1) * LANES] = (zr * q_scale).astype(BF16)
        else:
            k_ref[0, :, (s * LANES - d):(s * LANES - d + LANES)] = zr.astype(BF16)


def _rope_tables(s_len, c_len):
    rows = s_len // GRID_W
    row = np.repeat(np.arange(rows, dtype=np.float32), GRID_W)
    col = np.tile(np.arange(GRID_W, dtype=np.float32), rows)
    inv_freq = np.float32(ROPE_BASE) ** (-np.arange(ROT_FREQS, dtype=np.float32) / np.float32(ROT_FREQS))
    ang_r = row[:, None] * inv_freq
    ang_c = col[:, None] * inv_freq
    ang = np.concatenate([ang_r, ang_r, ang_c, ang_c], axis=-1).astype(np.float64)
    cos, sin = np.cos(ang), np.sin(ang)
    first_half = (np.arange(DIFF_HEAD_DIM) // ROT_FREQS) % 2 == 0
    sa = np.where(first_half[None, :], -sin, 0.0)
    sb = np.where(first_half[None, :], 0.0, sin)
    pad = lambda t, fill: np.concatenate([t, np.full((c_len, DIFF_HEAD_DIM), fill)], axis=0)
    two = lambda t: np.concatenate([t, t], axis=-1).astype(np.float32)
    return two(pad(cos, 1.0)), two(pad(sa, 0.0)), two(pad(sb, 0.0))


def _qkv_call(x, ctx, mod, wqk, wvt, tables, ctx_row):
    b, s_len, d = x.shape
    n_lat = s_len // TM
    nt = n_lat + 1
    t = nt * TM
    mod_idx = lambda bi, j: (jnp.where(j < n_lat, bi, ctx_row), 0, 0)
    return pl.pallas_call(
        functools.partial(_qkv_kernel, d=d, n_lat=n_lat),
        out_shape=(jax.ShapeDtypeStruct((b, t, d), BF16), jax.ShapeDtypeStruct((b, t, d), BF16),
                   jax.ShapeDtypeStruct((b, nt, d, TM), BF16)),
        grid=(b, nt),
        in_specs=[
            pl.BlockSpec((1, TM, d), lambda bi, j: (bi, jnp.minimum(j, n_lat - 1), 0)),
            pl.BlockSpec((1, TM, d), lambda bi, j: (bi, 0, 0)),
            pl.BlockSpec((1, 1, 6 * d), mod_idx),
            pl.BlockSpec((d, 2 * d), lambda bi, j: (0, 0)),
            pl.BlockSpec((d, d), lambda bi, j: (0, 0)),
            pl.BlockSpec((TM, LANES), lambda bi, j: (j, 0)),
            pl.BlockSpec((TM, LANES), lambda bi, j: (j, 0)),
            pl.BlockSpec((TM, LANES), lambda bi, j: (j, 0)),
        ],
        out_specs=(pl.BlockSpec((1, TM, d), lambda bi, j: (bi, j, 0)),
                   pl.BlockSpec((1, TM, d), lambda bi, j: (bi, j, 0)),
                   pl.BlockSpec((1, 1, d, TM), lambda bi, j: (bi, j, 0, 0))),
        compiler_params=_cparams(("parallel", "parallel")),
        name="qkv_rope",
    )(x, ctx, mod, wqk, wvt, *tables)


def _attn_kernel(q_ref, k_ref, vt_ref, lam_ref, g_ref, o_ref, kn_ref, *, chunks, lambda_init, tq):
    q = q_ref[0]
    lane = lax.broadcasted_iota(I32, q.shape, 1)
    zero = jnp.zeros_like(q)
    qq = jnp.concatenate([jnp.where(lane < DIFF_HEAD_DIM, q, zero),
                          jnp.where(lane >= DIFF_HEAD_DIM, q, zero)], axis=0)
    k_chunk = lambda c: k_ref[0, c * TK:(c + 1) * TK, :]

    def finish(acc, l):
        acc = acc / l
        lv = lam_ref[...]
        lam = (jnp.exp(jnp.sum(lv[0:1] * lv[1:2], axis=1, keepdims=True))
               - jnp.exp(jnp.sum(lv[2:3] * lv[3:4], axis=1, keepdims=True)) + lambda_init)
        ot = acc[:, :tq] - lam * acc[:, tq:]
        ms = jnp.mean(ot * ot, axis=0, keepdims=True)
        ot = ot * lax.rsqrt(ms + LN_EPS) * g_ref[...] * (1.0 - lambda_init)
        o_ref[0] = ot.T.astype(BF16)

    @pl.when(pl.program_id(2) == 0)
    def _():
        klane = lax.broadcasted_iota(I32, (TK, LANES), 1)
        best1 = jnp.zeros((1, 1), F32)
        best2 = jnp.zeros((1, 1), F32)
        for c in chunks:
            kf = k_chunk(c).astype(F32)
            ksq = kf * kf
            n1 = jnp.sum(jnp.where(klane < DIFF_HEAD_DIM, ksq, 0.0), axis=1, keepdims=True)
            n2 = jnp.sum(jnp.where(klane >= DIFF_HEAD_DIM, ksq, 0.0), axis=1, keepdims=True)
            best1 = jnp.maximum(best1, jnp.max(n1, axis=0, keepdims=True))
            best2 = jnp.maximum(best2, jnp.max(n2, axis=0, keepdims=True))
        kn_ref[...] = jnp.where(lax.broadcasted_iota(I32, kn_ref.shape, 1) < DIFF_HEAD_DIM, best1, best2)

    kn = kn_ref[0:1, :]
    qf = qq.astype(F32)
    qsq = jnp.sum(qf * qf, axis=1, keepdims=True)
    half = lax.broadcasted_iota(I32, (2 * tq, 1), 0) < tq
    bound = jnp.sqrt(qsq * jnp.where(half, kn[:, 0:1], kn[:, DIFF_HEAD_DIM:DIFF_HEAD_DIM + 1])) * BOUND_SLACK
    shift = SOFTMAX_HEADROOM - bound
    lane2 = lax.broadcasted_iota(I32, (2 * tq, LANES), 1)
    q_aug = jnp.concatenate([qq, jnp.where(lane2 == 0, shift, 0.0).astype(BF16)], axis=1)
    one_lane = jnp.where(lax.broadcasted_iota(I32, (TK, LANES), 1) == 0, 1.0, 0.0).astype(BF16)
    scores = lambda c: _dot_nt(jnp.concatenate([k_chunk(c), one_lane], axis=1), q_aug)
    acc = jnp.zeros((LANES, 2 * tq), F32)
    l = jnp.zeros((1, 2 * tq), F32)
    st_next = scores(chunks[0])
    for i, c in enumerate(chunks):
        st = st_next
        if i + 1 < len(chunks):
            st_next = scores(chunks[i + 1])
        p = jnp.exp2(st)
        l = l + jnp.sum(p, axis=0, keepdims=True)
        acc = acc + _dot(vt_ref[0, c], p.astype(BF16))
    finish(acc, l)
    underflowed = jnp.sum(jnp.where(l > SOFTMAX_MIN_SUM, 0.0, 1.0)) > 0.0

    @pl.when(underflowed)
    def _():
        m = jnp.full((1, 2 * tq), NEG_BIG, F32)
        l = jnp.zeros((1, 2 * tq), F32)
        acc = jnp.zeros((LANES, 2 * tq), F32)
        for c in chunks:
            st = _dot_nt(k_chunk(c), qq)
            m_new = jnp.maximum(m, jnp.max(st, axis=0, keepdims=True))
            alpha = jnp.exp2(m - m_new)
            p = jnp.exp2(st - m_new)
            l = l * alpha + jnp.sum(p, axis=0, keepdims=True)
            acc = acc * alpha + _dot(vt_ref[0, c], p.astype(BF16))
            m = m_new
        finish(acc, l)


def _attn_call(q, k, vt, lam_vecs, subln_g, lambda_init, *, tq, q_row0, n_rows, chunk0, n_chunks, name):
    b, t, d = q.shape
    q0 = q_row0 // tq
    kb = chunk0 // n_chunks
    assert q_row0 % tq == 0 and chunk0 % n_chunks == 0
    return pl.pallas_call(
        functools.partial(_attn_kernel, chunks=tuple(range(n_chunks)), lambda_init=lambda_init, tq=tq),
        out_shape=jax.ShapeDtypeStruct((b, n_rows, d), BF16),
        grid=(b, DIFF_HEADS, n_rows // tq),
        in_specs=[
            pl.BlockSpec((1, tq, LANES), lambda bi, h, qi: (bi, q0 + qi, h)),
            pl.BlockSpec((1, n_chunks * TK, LANES), lambda bi, h, qi: (bi, kb, h)),
            pl.BlockSpec((1, n_chunks, LANES, TM), lambda bi, h, qi: (bi, kb, h, 0)),
            pl.BlockSpec((4, DIFF_HEAD_DIM), lambda bi, h, qi: (0, 0)),
            pl.BlockSpec((LANES, 1), lambda bi, h, qi: (0, 0)),
        ],
        out_specs=pl.BlockSpec((1, tq, LANES), lambda bi, h, qi: (bi, qi, h)),
        scratch_shapes=[pltpu.VMEM((8, LANES), F32)],
        compiler_params=_cparams(("parallel", "parallel", "arbitrary")),
        name=name,
    )(q, k, vt, lam_vecs, subln_g.reshape(LANES, 1))


def _route(logits_t, rb, ri_ref, rw_ref, cnt_ref, first):
    tm = logits_t.shape[1]
    aff = 1.0 / (1.0 + jnp.exp(-logits_t))
    sel = aff + rb
    row = lambda a, e: a[e:e + 1, :]
    gscore = []
    for g in range(N_GROUPS):
        a, b, c, d = (row(sel, g * EXPERTS_PER_GROUP + j) for j in range(EXPERTS_PER_GROUP))
        hi1, lo1, hi2, lo2 = jnp.maximum(a, b), jnp.minimum(a, b), jnp.maximum(c, d), jnp.minimum(c, d)
        gscore.append(jnp.maximum(hi1, hi2) + jnp.maximum(jnp.minimum(hi1, hi2), jnp.maximum(lo1, lo2)))
    best, grp = gscore[0], jnp.zeros((1, tm), I32)
    for g in range(1, N_GROUPS):
        better = gscore[g] > best
        best = jnp.where(better, gscore[g], best)
        grp = jnp.where(better, g, grp)
    in_sel, in_aff = [], []
    for j in range(EXPERTS_PER_GROUP):
        s_j, a_j = row(sel, j), row(aff, j)
        for g in range(1, N_GROUPS):
            s_j = jnp.where(grp == g, row(sel, g * EXPERTS_PER_GROUP + j), s_j)
            a_j = jnp.where(grp == g, row(aff, g * EXPERTS_PER_GROUP + j), a_j)
        in_sel.append(s_j)
        in_aff.append(a_j)
    v0, i0, a0 = in_sel[0], jnp.zeros((1, tm), I32), in_aff[0]
    for j in range(1, EXPERTS_PER_GROUP):
        better = in_sel[j] > v0
        v0 = jnp.where(better, in_sel[j], v0)
        i0 = jnp.where(better, j, i0)
        a0 = jnp.where(better, in_aff[j], a0)
    v1, i1, a1 = jnp.full((1, tm), -jnp.inf, F32), jnp.zeros((1, tm), I32), jnp.zeros((1, tm), F32)
    for j in range(EXPERTS_PER_GROUP):
        better = (in_sel[j] > v1) & (i0 != j)
        v1 = jnp.where(better, in_sel[j], v1)
        i1 = jnp.where(better, j, i1)
        a1 = jnp.where(better, in_aff[j], a1)
    e0 = grp * EXPERTS_PER_GROUP + i0
    e1 = grp * EXPERTS_PER_GROUP + i1
    wsum = a0 + a1
    erow = lax.broadcasted_iota(I32, (N_EXPERTS, tm), 0)
    oh0 = (erow == e0).astype(F32)
    oh1 = (erow == e1).astype(F32)
    both = oh0 + oh1
    both_bf = both.astype(BF16)
    tri = (lax.broadcasted_iota(I32, (tm, tm), 0) < lax.broadcasted_iota(I32, (tm, tm), 1)).astype(BF16)
    before = _dot(both_bf, tri)
    round_up = lambda n: jnp.floor((n + (ROW_GROUP - 1.0)) * (1.0 / ROW_GROUP)) * ROW_GROUP
    run_col = round_up(jnp.sum(both, axis=1, keepdims=True))
    e_r = lax.broadcasted_iota(I32, (N_EXPERTS, N_EXPERTS), 0)
    e_c = lax.broadcasted_iota(I32, (N_EXPERTS, N_EXPERTS), 1)
    off_col = _dot((e_c < e_r).astype(BF16), jnp.broadcast_to(run_col, (N_EXPERTS, LANES)).astype(BF16))[:, 0:1]
    base = off_col + before
    s0 = jnp.sum(oh0 * base, axis=0, keepdims=True)
    s1 = jnp.sum(oh1 * base, axis=0, keepdims=True)
    both_pad = jnp.concatenate([both_bf, jnp.zeros((LANES - N_EXPERTS, tm), BF16)], axis=0)
    run_row = round_up(_dot_nt(jnp.ones((8, tm), BF16), both_pad))
    l_r = lax.broadcasted_iota(I32, (LANES, LANES), 0)
    l_c = lax.broadcasted_iota(I32, (LANES, LANES), 1)
    off_row = _dot(run_row.astype(BF16), (l_r < l_c).astype(BF16))
    lane = lax.broadcasted_iota(I32, (1, LANES), 1)
    groups = run_row[0:1] * (1.0 / ROW_GROUP)
    groups = jnp.where(lane == N_EXPERTS, jnp.sum(groups, axis=1, keepdims=True), groups)

    @pl.when(first)
    def _():
        cnt_ref[...] = jnp.zeros_like(cnt_ref)

    zero_half = jnp.zeros((1, tm - LANES), I32)
    ri_ref[0, 0:1, :] = s0.astype(I32)
    ri_ref[0, 1:2, :] = s1.astype(I32)
    ri_ref[0, 2:3, :] = e0
    ri_ref[0, 3:4, :] = e1
    for r, v in ((4, groups), (5, cnt_ref[0:1, :]), (6, off_row[0:1])):
        ri_ref[0, r:r + 1, 0:LANES] = v.astype(I32)
        ri_ref[0, r:r + 1, LANES:] = zero_half
    ri_ref[0, 7:8, :] = jnp.zeros((1, tm), I32)
    cnt_ref[...] = cnt_ref[...] + run_row
    rw_ref[0, 0:1, :] = a0 / wsum
    rw_ref[0, 1:2, :] = a1 / wsum
    rw_ref[0, 2:8, :] = jnp.zeros((6, tm), F32)


def _post_mixer(y, x, mod, lng, lnb, rwt, rb, x1_ref, ri_ref, rw_ref, cnt_ref, first, d):
    x1 = _layer_norm(ALPHA * x + mod[:, 2 * d:3 * d] * y, lng, lnb)
    x1_ref[0] = x1
    tok = x1 * (1.0 + mod[:, 4 * d:5 * d]) + mod[:, 3 * d:4 * d]
    t_hi = tok.astype(BF16)
    t_mid = (tok - t_hi.astype(F32)).astype(BF16)
    w_hi, w_mid = rwt[0], rwt[1]
    logits = _dot(t_hi, w_hi) + (_dot(t_hi, w_mid) + _dot(t_mid, w_hi))
    _route(logits.T[0:N_EXPERTS, :], rb, ri_ref, rw_ref, cnt_ref, first)


def _attn_out_kernel(ol_ref, oc_ref, xl_ref, xc_ref, mod_ref, wo_ref, lng_ref, lnb_ref, rwt_ref, rb_ref,
                     x1_ref, ri_ref, rw_ref, cnt_ref, *, d, n_lat):
    first = (pl.program_id(0) == 0) & (pl.program_id(1) == 0)
    latent = pl.program_id(1) < n_lat
    o = jnp.where(latent, ol_ref[0], oc_ref[0])
    x = jnp.where(latent, xl_ref[0], xc_ref[0])
    y = _dot(o, wo_ref[...])
    _post_mixer(y, x, mod_ref[0], lng_ref[...], lnb_ref[...], rwt_ref[...], rb_ref[...],
                x1_ref, ri_ref, rw_ref, cnt_ref, first, d)


def _route_out_shapes(b, n_tiles, t_out, d):
    return (jax.ShapeDtypeStruct((b, t_out, d), F32),
            jax.ShapeDtypeStruct((b * n_tiles, 8, TM), I32),
            jax.ShapeDtypeStruct((b * n_tiles, 8, TM), F32),
            jax.ShapeDtypeStruct((8, LANES), F32))


def _route_out_specs(n_tiles, d):
    return (pl.BlockSpec((1, TM, d), lambda bi, j: (bi, j, 0)),
            pl.BlockSpec((1, 8, TM), lambda bi, j: (bi * n_tiles + j, 0, 0)),
            pl.BlockSpec((1, 8, TM), lambda bi, j: (bi * n_tiles + j, 0, 0)),
            pl.BlockSpec((8, LANES), lambda bi, j: (0, 0)))


def _attn_out_call(o_lat, o_ctx, x, ctx, mod, wo, lng, lnb, rwt, rb, ctx_row):
    b, s_len, d = x.shape
    n_lat = s_len // TM
    nt = n_lat + 1
    row = lambda a: pl.BlockSpec(a.shape, lambda bi, j: (0,) * a.ndim)
    lat = pl.BlockSpec((1, TM, d), lambda bi, j: (bi, jnp.minimum(j, n_lat - 1), 0))
    cxt = pl.BlockSpec((1, TM, d), lambda bi, j: (bi, 0, 0))
    return pl.pallas_call(
        functools.partial(_attn_out_kernel, d=d, n_lat=n_lat),
        out_shape=_route_out_shapes(b, nt, nt * TM, d),
        grid=(b, nt),
        in_specs=[
            lat, cxt, lat, cxt,
            pl.BlockSpec((1, 1, 6 * d), lambda bi, j: (jnp.where(j < n_lat, bi, ctx_row), 0, 0)),
            row(wo), row(lng), row(lnb), row(rwt), row(rb),
        ],
        out_specs=_route_out_specs(nt, d),
        compiler_params=_cparams(("arbitrary", "arbitrary")),
        name="attn_out_route",
    )(o_lat, o_ctx, x, ctx, mod, wo, lng, lnb, rwt, rb)


def _group_copy(src, r_src, dst, r_dst, sem):
    return pltpu.make_async_copy(src.at[pl.ds(pl.multiple_of(r_src, ROW_GROUP), ROW_GROUP)],
                                 dst.at[pl.ds(pl.multiple_of(r_dst, ROW_GROUP), ROW_GROUP)], sem)


def _tab(tab_ref, tile, row, e):
    return tab_ref[(tile * TAB_ROWS + row) * TAB_LANES + e]


def _run_copies(tab_ref, pad_ref, tile, stage, hbm, sem, to_hbm):
    for e in range(N_EXPERTS):
        s0 = _tab(tab_ref, tile, 2, e)
        h0 = pad_ref[e] + _tab(tab_ref, tile, 1, e)

        def issue(g, carry):
            s, h = s0 + g * ROW_GROUP, h0 + g * ROW_GROUP
            (_group_copy(stage, s, hbm, h, sem) if to_hbm else _group_copy(hbm, h, stage, s, sem)).start()
            return carry

        lax.fori_loop(0, _tab(tab_ref, tile, 0, e), issue, 0)


def _run_waits(tab_ref, tile, stage, hbm, sem, to_hbm):
    def drain(g, carry):
        (_group_copy(stage, 0, hbm, 0, sem) if to_hbm else _group_copy(hbm, 0, stage, 0, sem)).wait()
        return carry

    lax.fori_loop(0, _tab(tab_ref, tile, 0, N_EXPERTS), drain, 0)


def _dispatch_kernel(tab_ref, pad_ref, fill_ref, x_ref, mod_ref, ri_ref, xs_ref, stage_ref, zero_ref, sems, sem_z,
                     *, d, n_steps):
    step = pl.program_id(0) * pl.num_programs(1) + pl.program_id(1)
    buf = step % 2
    mod = mod_ref[0]
    tok = (x_ref[0] * (1.0 + mod[:, 4 * d:5 * d]) + mod[:, 3 * d:4 * d]).astype(BF16)
    slot = lax.broadcasted_iota(I32, (STAGE_ROWS, TM), 0)
    ri = ri_ref[0]
    pick = jnp.where((slot == ri[0:1, :]) | (slot == ri[1:2, :]), 1.0, 0.0).astype(BF16)
    for bsel in range(2):
        @pl.when(buf == bsel)
        def _():
            stage_ref[bsel] = _dot(pick, tok).astype(BF16)
            _run_copies(tab_ref, pad_ref, step, stage_ref.at[bsel], xs_ref, sems.at[bsel], True)

    for bsel in range(2):
        @pl.when((buf != bsel) & (step > 0))
        def _():
            _run_waits(tab_ref, step - 1, stage_ref.at[bsel], xs_ref, sems.at[bsel], True)

    @pl.when(step == n_steps - 1)
    def _():
        for bsel in range(2):
            @pl.when(buf == bsel)
            def _():
                _run_waits(tab_ref, step, stage_ref.at[bsel], xs_ref, sems.at[bsel], True)

        zero_ref[...] = jnp.zeros_like(zero_ref)
        for e in range(N_EXPERTS + 1):
            lo, hi = fill_ref[2 * e], fill_ref[2 * e + 1]

            def fill(g, carry):
                _group_copy(zero_ref, 0, xs_ref, g * ROW_GROUP, sem_z).start()
                return carry

            def fill_wait(g, carry):
                _group_copy(zero_ref, 0, xs_ref, 0, sem_z).wait()
                return carry

            lax.fori_loop(lo, hi, fill, 0)
            lax.fori_loop(lo, hi, fill_wait, 0)


def _dispatch_call(tab, pad_start, fill, x1, mod, ri, n_tiles, n_pad, ctx_row, n_lat):
    b, t, d = x1.shape
    grid_spec = pltpu.PrefetchScalarGridSpec(
        num_scalar_prefetch=3,
        grid=(b, n_tiles),
        in_specs=[
            pl.BlockSpec((1, TM, d), lambda bi, j, *_: (bi, j, 0)),
            pl.BlockSpec((1, 1, 6 * d), lambda bi, j, *_: (jnp.where(j < n_lat, bi, ctx_row), 0, 0)),
            pl.BlockSpec((1, 8, TM), lambda bi, j, *_: (bi * n_tiles + j, 0, 0)),
        ],
        out_specs=pl.BlockSpec(memory_space=pl.ANY),
        scratch_shapes=[pltpu.VMEM((2, STAGE_ROWS, d), BF16), pltpu.VMEM((ROW_GROUP, d), BF16),
                        pltpu.SemaphoreType.DMA((2,)), pltpu.SemaphoreType.DMA],
    )
    return pl.pallas_call(
        functools.partial(_dispatch_kernel, d=d, n_steps=b * n_tiles),
        out_shape=jax.ShapeDtypeStruct((n_pad, d), BF16),
        grid_spec=grid_spec,
        compiler_params=_cparams(("arbitrary", "arbitrary")),
        name="moe_dispatch",
    )(tab, pad_start, fill, x1, mod, ri)


def _ffn_kernel(be_ref, nu_ref, xs_ref, wg_ref, wu_ref, wd_ref, ys_ref, wg_s, wu_s, wd_s):
    i = pl.program_id(0)

    @pl.when(i < nu_ref[0])
    def _():
        prev = be_ref[jnp.maximum(i - 1, 0)]

        @pl.when((i == 0) | (be_ref[i] != prev))
        def _():
            wg_s[...] = wg_ref[0, 0].astype(BF16)
            wu_s[...] = wu_ref[0, 0].astype(BF16)
            wd_s[...] = wd_ref[0, 0].astype(BF16)

        x = xs_ref[...]
        g = _dot(x, wg_s[...])
        u = _dot(x, wu_s[...])
        ys_ref[...] = _dot((_silu(g) * u).astype(BF16), wd_s[...]).astype(BF16)

    @pl.when(i >= nu_ref[0])
    def _():
        ys_ref[...] = jnp.zeros_like(ys_ref)


def _ffn_call(block_expert, n_used, xs, w_gate, w_up, w_down, layer):
    n_pad, d = xs.shape
    de = w_gate.shape[-1]
    n_blocks = n_pad // MOE_BLK
    blk = lambda i, be, nu: (jnp.minimum(i, nu[0] - 1), 0)
    exp = lambda i, be, nu: (layer, be[jnp.minimum(i, nu[0] - 1)], 0, 0)
    grid_spec = pltpu.PrefetchScalarGridSpec(
        num_scalar_prefetch=2,
        grid=(n_blocks,),
        in_specs=[
            pl.BlockSpec((MOE_BLK, d), blk),
            pl.BlockSpec((1, 1, d, de), exp),
            pl.BlockSpec((1, 1, d, de), exp),
            pl.BlockSpec((1, 1, de, d), exp),
        ],
        out_specs=pl.BlockSpec((MOE_BLK, d), lambda i, be, nu: (i, 0)),
        scratch_shapes=[pltpu.VMEM((d, de), BF16), pltpu.VMEM((d, de), BF16), pltpu.VMEM((de, d), BF16)],
    )
    return pl.pallas_call(
        _ffn_kernel,
        out_shape=jax.ShapeDtypeStruct((n_pad, d), BF16),
        grid_spec=grid_spec,
        compiler_params=_cparams(("arbitrary",)),
        name="moe_ffn",
    )(block_expert, n_used, xs, w_gate, w_up, w_down)


def _combine_kernel(tab_ref, pad_ref, x_ref, mod_ref, rit_ref, rw_ref, lng_ref, lnb_ref, ys_ref, o_ref,
                    stage_ref, sems, *, d, n_steps):
    step = pl.program_id(0) * pl.num_programs(1) + pl.program_id(1)
    buf = step % 2

    @pl.when(step == 0)
    def _():
        stage_ref[...] = jnp.zeros_like(stage_ref)
        _run_copies(tab_ref, pad_ref, step, stage_ref.at[0], ys_ref, sems.at[0], False)

    for bsel in range(2):
        @pl.when((buf != bsel) & (step + 1 < n_steps))
        def _():
            _run_copies(tab_ref, pad_ref, step + 1, stage_ref.at[bsel], ys_ref, sems.at[bsel], False)

    rit = rit_ref[0]
    lane = lax.broadcasted_iota(I32, (TM, STAGE_ROWS), 1)
    pick0 = jnp.where(lane == rit[:, 0:1], 1.0, 0.0).astype(BF16)
    pick1 = jnp.where(lane == rit[:, 1:2], 1.0, 0.0).astype(BF16)
    w = rw_ref[0]
    mod = mod_ref[0]
    for bsel in range(2):
        @pl.when(buf == bsel)
        def _():
            _run_waits(tab_ref, step, stage_ref.at[bsel], ys_ref, sems.at[bsel], False)
            stage = stage_ref[bsel]
            moe = w[:, 0:1] * _dot(pick0, stage) + w[:, 1:2] * _dot(pick1, stage)
            o_ref[0] = _layer_norm(ALPHA * x_ref[0] + mod[:, 5 * d:6 * d] * moe, lng_ref[...], lnb_ref[...])


def _combine_call(tab, pad_start, x1, mod, ri_t, rw_t, lng, lnb, ys, n_tiles, ctx_row, n_lat):
    b, _, d = x1.shape
    row = lambda a: pl.BlockSpec(a.shape, lambda bi, j, *_: (0,) * a.ndim)
    grid_spec = pltpu.PrefetchScalarGridSpec(
        num_scalar_prefetch=2,
        grid=(b, n_tiles),
        in_specs=[
            pl.BlockSpec((1, TM, d), lambda bi, j, *_: (bi, j, 0)),
            pl.BlockSpec((1, 1, 6 * d), lambda bi, j, *_: (jnp.where(j < n_lat, bi, ctx_row), 0, 0)),
            pl.BlockSpec((1, TM, 8), lambda bi, j, *_: (bi * n_tiles + j, 0, 0)),
            pl.BlockSpec((1, TM, 8), lambda bi, j, *_: (bi * n_tiles + j, 0, 0)),
            row(lng), row(lnb),
            pl.BlockSpec(memory_space=pl.ANY),
        ],
        out_specs=pl.BlockSpec((1, TM, d), lambda bi, j, *_: (bi, j, 0)),
        scratch_shapes=[pltpu.VMEM((2, STAGE_ROWS, d), BF16), pltpu.SemaphoreType.DMA((2,))],
    )
    return pl.pallas_call(
        functools.partial(_combine_kernel, d=d, n_steps=b * n_tiles),
        out_shape=jax.ShapeDtypeStruct((b, n_tiles * TM, d), F32),
        grid_spec=grid_spec,
        compiler_params=_cparams(("arbitrary", "arbitrary")),
        name="moe_combine",
    )(tab, pad_start, x1, mod, ri_t, rw_t, lng, lnb, ys)


def _moe(x1, mod, ri, rw, counts, lng, lnb, w_gate, w_up, w_down, layer, n_tiles, ctx_row, n_lat):
    b, _, d = x1.shape
    n_steps = b * n_tiles
    n_rows_max = 2 * n_steps * TM + (ROW_GROUP - 1) * N_EXPERTS * n_steps
    n_pad = -(-n_rows_max // MOE_BLK) * MOE_BLK + N_EXPERTS * MOE_BLK
    n_blocks = n_pad // MOE_BLK
    cnt = counts[0, :N_EXPERTS].astype(I32)
    padded = (cnt + MOE_BLK - 1) // MOE_BLK * MOE_BLK
    pad_end = jnp.cumsum(padded)
    pad_start = jnp.concatenate([jnp.zeros((1,), I32), pad_end]).astype(I32)
    block_row0 = jnp.arange(n_blocks, dtype=I32) * MOE_BLK
    block_expert = jnp.minimum(
        jnp.sum((pad_end[None, :] <= block_row0[:, None]).astype(I32), axis=1), N_EXPERTS - 1).astype(I32)
    n_used = (pad_end[-1:] // MOE_BLK).astype(I32)
    fill_lo = jnp.concatenate([pad_start[:-1] + cnt, pad_end[-1:]]) // ROW_GROUP
    fill_hi = jnp.concatenate([pad_end, jnp.full((1,), n_pad, I32)]) // ROW_GROUP
    fill = jnp.stack([fill_lo, fill_hi], axis=1).reshape(-1).astype(I32)
    tab = ri[:, 4:4 + TAB_ROWS, :TAB_LANES].reshape(-1)
    xs = _dispatch_call(tab, pad_start, fill, x1, mod, ri, n_tiles, n_pad, ctx_row, n_lat)
    ys = _ffn_call(block_expert, n_used, xs, w_gate, w_up, w_down, layer)
    ri_t, rw_t = jnp.swapaxes(ri, 1, 2), jnp.swapaxes(rw, 1, 2)
    return _combine_call(tab, pad_start, x1, mod, ri_t, rw_t, lng, lnb, ys, n_tiles, ctx_row, n_lat)


def _gla_proj_kernel(x_ref, mod_ref, win_ref, wgd_ref, wgu_ref, bg_ref, q_ref, k_ref, v_ref, r_ref, g_ref, *, d):
    mod = mod_ref[0]
    h = (x_ref[0] * (1.0 + mod[:, d:2 * d]) + mod[:, 0:d]).astype(BF16)
    dk = GLA_HEADS * GLA_DK
    p = _dot(h, win_ref[...])
    q_ref[0] = (p[:, 0:dk] * GLA_DK ** -0.5).astype(BF16)
    k_ref[0] = p[:, dk:2 * dk].astype(BF16)
    v_ref[0] = p[:, 2 * dk:2 * dk + d].astype(BF16)
    r_ref[0] = p[:, 2 * dk + d:].astype(BF16)
    low = _dot(h, wgd_ref[...]).astype(BF16)
    pre = _dot(low, wgu_ref[...]) + bg_ref[...]
    g_ref[0] = (jnp.minimum(pre, 0.0) - jnp.log1p(jnp.exp(-jnp.abs(pre)))) * (1.0 / GLA_TAU)


def _gla_proj_call(xs, mod, win, wgd, wgu, bg, ctx_row):
    b, t, d = xs.shape
    nt = t // TM
    n_lat = nt - 1
    dk = GLA_HEADS * GLA_DK
    row = lambda a: pl.BlockSpec(a.shape, lambda bi, j: (0,) * a.ndim)
    tile = lambda n: pl.BlockSpec((1, TM, n), lambda bi, j: (bi, j, 0))
    return pl.pallas_call(
        functools.partial(_gla_proj_kernel, d=d),
        out_shape=(jax.ShapeDtypeStruct((b, t, dk), BF16), jax.ShapeDtypeStruct((b, t, dk), BF16),
                   jax.ShapeDtypeStruct((b, t, d), BF16), jax.ShapeDtypeStruct((b, t, d), BF16),
                   jax.ShapeDtypeStruct((b, t, 2 * dk), F32)),
        grid=(b, nt),
        in_specs=[tile(d), pl.BlockSpec((1, 1, 6 * d), lambda bi, j: (jnp.where(j < n_lat, bi, ctx_row), 0, 0)),
                  row(win), row(wgd), row(wgu), row(bg)],
        out_specs=(tile(dk), tile(dk), tile(d), tile(d), tile(2 * dk)),
        compiler_params=_cparams(("parallel", "parallel")),
        name="gla_proj",
    )(xs, mod, win, wgd, wgu, bg)


def _split3(x):
    hi = x.astype(BF16)
    r1 = x - hi.astype(F32)
    mid = r1.astype(BF16)
    lo = (r1 - mid.astype(F32)).astype(BF16)
    return hi, mid, lo


class _GlaMasks:
    def __init__(self, reverse):
        L = GLA_CHUNK
        t = lax.broadcasted_iota(I32, (L, L), 0)
        s = lax.broadcasted_iota(I32, (L, L), 1)
        seen = (s >= t) if reverse else (s <= t)
        self.tri = seen.astype(BF16)
        blk = lambda r: jnp.right_shift(r, GLA_SUB.bit_length() - 1)
        dist = (blk(s) - blk(t)) if reverse else (blk(t) - blk(s))
        self.pairs = [seen & (dist == 0)] + [dist == n for n in range(1, L // GLA_SUB)]


def _gla_chunk_matmuls(q_ref, k_ref, v_ref, b_all, rows, h, state_ref, reverse):
    L, SB, DK = GLA_CHUNK, GLA_SUB, GLA_DK
    nb = L // SB
    kcols = slice(h * DK, (h + 1) * DK)
    vcols = slice(h * GLA_DV, (h + 1) * GLA_DV)
    bh = b_all[:, kcols]
    zero = jnp.zeros((1, DK), F32)
    earlier = lambda i, n: i + n if reverse else i - n

    def start_ref(i):
        if not 0 <= i < nb or i == (nb - 1 if reverse else 0):
            return zero
        r = (i + 1) * SB if reverse else i * SB - 1
        return bh[r:r + 1, :]

    def end_ref(i):
        r = i * SB if reverse else i * SB + SB - 1
        return bh[r:r + 1, :]

    per_block = lambda f: jnp.concatenate([jnp.broadcast_to(f(i), (SB, DK)) for i in range(nb)], axis=0)
    bs, be = per_block(start_ref), per_block(end_ref)
    b_last = end_ref(0 if reverse else nb - 1)
    q = q_ref[0, rows, kcols].astype(F32)
    k = k_ref[0, rows, kcols].astype(F32)
    v = v_ref[0, rows, vcols]
    scaled = lambda x, e: (x * jnp.exp(e)).astype(BF16)
    q_own = scaled(q, bh - bs)
    k_own, k_end = scaled(k, bs - bh), scaled(k, be - bh)
    pieces = [_dot_nt(q_own, k_own), _dot_nt(q_own, k_end)]
    for n in range(2, nb):
        pieces.append(_dot_nt(scaled(q, bh - per_block(lambda i: start_ref(earlier(i, n - 1)))), k_end))
    state_t = state_ref[h]
    o_inter = _dot_nt(scaled(q, bh), state_t.astype(BF16))
    update = _dot_tn(v, scaled(k, b_last - bh))
    return pieces, o_inter, update, b_last, state_t, v, vcols


def _gla_scan_kernel(qf, kf, vf, gf, qb, kb, vb, gb, of_ref, ob_ref, sf_ref, sb_ref):
    @pl.when(pl.program_id(1) == 0)
    def _():
        sf_ref[...] = jnp.zeros_like(sf_ref)
        sb_ref[...] = jnp.zeros_like(sb_ref)

    L = GLA_CHUNK
    n_chunks = TM // L
    dirs = ((qf, kf, vf, gf, of_ref, sf_ref, False, _GlaMasks(False)),
            (qb, kb, vb, gb, ob_ref, sb_ref, True, _GlaMasks(True)))
    for step in range(n_chunks):
        work = []
        for q_ref, k_ref, v_ref, g_ref, o_ref, state_ref, reverse, masks in dirs:
            c = n_chunks - 1 - step if reverse else step
            rows = slice(c * L, (c + 1) * L)
            b_all = sum(_dot(masks.tri, part) for part in _split3(g_ref[0, rows, :]))
            for h in range(GLA_HEADS):
                work.append((o_ref, state_ref, rows, h, masks,
                             _gla_chunk_matmuls(q_ref, k_ref, v_ref, b_all, rows, h, state_ref, reverse)))
        for o_ref, state_ref, rows, h, masks, (pieces, o_inter, update, b_last, state_t, v, vcols) in work:
            a = sum(jnp.where(m, p, 0.0) for m, p in zip(masks.pairs, pieces)).astype(BF16)
            o_ref[0, rows, vcols] = _dot(a, v) + o_inter
            state_ref[h] = jnp.exp(b_last) * state_t + update


def _gla_scan_call(q, k, v, g):
    b, t, d = v.shape
    nt = t // TM
    n_lat = nt - 1
    dk = GLA_HEADS * GLA_DK
    fwd = lambda j: jnp.where(j == 0, n_lat, j - 1)
    bwd = lambda j: jnp.where(j == 0, n_lat, n_lat - j)
    spec = lambda n, idx, col: pl.BlockSpec((1, TM, n), lambda bi, j: (bi, idx(j), col))
    return pl.pallas_call(
        _gla_scan_kernel,
        out_shape=(jax.ShapeDtypeStruct((b, n_lat * TM, d), F32), jax.ShapeDtypeStruct((b, n_lat * TM, d), F32)),
        grid=(b, nt),
        in_specs=[spec(dk, fwd, 0), spec(dk, fwd, 0), spec(d, fwd, 0), spec(dk, fwd, 0),
                  spec(dk, bwd, 0), spec(dk, bwd, 0), spec(d, bwd, 0), spec(dk, bwd, 1)],
        out_specs=(pl.BlockSpec((1, TM, d), lambda bi, j: (bi, jnp.maximum(j - 1, 0), 0)),
                   pl.BlockSpec((1, TM, d), lambda bi, j: (bi, n_lat - jnp.maximum(j, 1), 0))),
        scratch_shapes=[pltpu.VMEM((GLA_HEADS, GLA_DV, GLA_DK), F32), pltpu.VMEM((GLA_HEADS, GLA_DV, GLA_DK), F32)],
        compiler_params=_cparams(("parallel", "arbitrary")),
        name="gla_scan",
    )(q, k, v, g, q, k, v, g)


def _gla_out_kernel(of_ref, ob_ref, r_ref, x_ref, mod_ref, ng_ref, wo_ref, lng_ref, lnb_ref, rwt_ref, rb_ref,
                    x1_ref, ri_ref, rw_ref, cnt_ref, *, d):
    first = (pl.program_id(0) == 0) & (pl.program_id(1) == 0)
    o = of_ref[0] + ob_ref[0]
    r = r_ref[0].astype(F32)
    gate = _silu(r)
    ng = ng_ref[...]
    parts = []
    for h in range(GLA_HEADS):
        cols = slice(h * GLA_DV, (h + 1) * GLA_DV)
        oh = o[:, cols]
        ms = jnp.mean(oh * oh, axis=-1, keepdims=True)
        parts.append((oh * lax.rsqrt(ms + LN_EPS) * ng * gate[:, cols]).astype(BF16))
    y = _dot(jnp.concatenate(parts, axis=-1), wo_ref[...])
    _post_mixer(y, x_ref[0], mod_ref[0], lng_ref[...], lnb_ref[...], rwt_ref[...], rb_ref[...],
                x1_ref, ri_ref, rw_ref, cnt_ref, first, d)


def _gla_out_call(o_f, o_b, r, xs, mod, ng, wo, lng, lnb, rwt, rb):
    b, s_len, d = o_f.shape
    n_lat = s_len // TM
    row = lambda a: pl.BlockSpec(a.shape, lambda bi, j: (0,) * a.ndim)
    tile = pl.BlockSpec((1, TM, d), lambda bi, j: (bi, j, 0))
    return pl.pallas_call(
        functools.partial(_gla_out_kernel, d=d),
        out_shape=_route_out_shapes(b, n_lat, s_len, d),
        grid=(b, n_lat),
        in_specs=[tile, tile, tile, tile, pl.BlockSpec((1, 1, 6 * d), lambda bi, j: (bi, 0, 0)),
                  row(ng), row(wo), row(lng), row(lnb), row(rwt), row(rb)],
        out_specs=_route_out_specs(n_lat, d),
        compiler_params=_cparams(("arbitrary", "arbitrary")),
        name="gla_out_route",
    )(o_f, o_b, r, xs, mod, ng, wo, lng, lnb, rwt, rb)


def kernel(x, c, ctx, c_ctx, ada_w, ada_b, ln_g, ln_b, attn_w_qkv, attn_w_o, attn_lambda, attn_subln_g, gla_w_in, gla_wg_down, gla_wg_up, gla_bg, gla_norm_g, gla_w_o, router_w, router_b, moe_w_gate, moe_w_up, moe_w_down):
    b, s_len, d = x.shape
    c_len = ctx.shape[1]
    assert ada_w.shape[0] == DEPTH and c_len == TM and s_len % TQ == 0 and TQ % TM == 0 and s_len % GRID_W == 0
    assert d == DIFF_HEADS * 2 * DIFF_HEAD_DIM == GLA_HEADS * GLA_DV
    assert STAGE_ROWS % LANES == 0 and STAGE_ROWS >= 2 * TM + (ROW_GROUP - 1) * N_EXPERTS
    n_lat = s_len // TM
    nt = n_lat + 1
    ctx_row = b

    cond_rows = -(-(b + 1) // 8) * 8
    cond = jnp.concatenate([c, c_ctx[None, :], jnp.zeros((cond_rows - b - 1, d), F32)], axis=0)
    mods = _ada_call(cond, ada_w, ada_b)
    rw_pad = jnp.concatenate([router_w, jnp.zeros((d, LANES - N_EXPERTS), F32)], axis=1)
    rw_hi = rw_pad.astype(BF16)
    rwt = jnp.stack([rw_hi, (rw_pad - rw_hi.astype(F32)).astype(BF16)])
    rb = router_b.reshape(N_EXPERTS, 1)

    mod0 = mods[0].reshape(cond_rows, 1, 6 * d)
    wqkv = attn_w_qkv[0]
    wqk = wqkv[:, :2 * d].astype(BF16)
    wvt = wqkv[:, 2 * d:].T.astype(BF16)
    tables = tuple(jnp.asarray(t) for t in _rope_tables(s_len, c_len))
    q, k, vt = _qkv_call(x, ctx, mod0, wqk, wvt, tables, ctx_row)
    lambda_init = 0.8 - 0.6 * math.exp(-0.3 * 0)
    attn = functools.partial(_attn_call, q, k, vt, attn_lambda[0], attn_subln_g[0], lambda_init)
    o_lat = attn(tq=TQ, q_row0=0, n_rows=s_len, chunk0=0, n_chunks=nt, name="diff_attn_lat")
    o_ctx = attn(tq=c_len, q_row0=s_len, n_rows=c_len, chunk0=n_lat, n_chunks=1, name="diff_attn_ctx")
    x1, ri, rw, counts = _attn_out_call(o_lat, o_ctx, x, ctx, mod0, attn_w_o[0].astype(BF16),
                                        ln_g[0, 0][None], ln_b[0, 0][None], rwt, rb, ctx_row)
    stream = _moe(x1, mod0, ri, rw, counts, ln_g[0, 1][None], ln_b[0, 1][None],
                  moe_w_gate, moe_w_up, moe_w_down, 0, nt, ctx_row, n_lat)

    mod1 = mods[1].reshape(cond_rows, 1, 6 * d)
    dk = GLA_HEADS * GLA_DK
    wgd = jnp.concatenate([gla_wg_down[0, 0], gla_wg_down[0, 1],
                           jnp.zeros((d, LANES - 2 * GLA_GATE_RANK), F32)], axis=1).astype(BF16)
    wgu = jnp.zeros((LANES, 2 * dk), F32)
    wgu = wgu.at[0:GLA_GATE_RANK, 0:dk].set(gla_wg_up[0, 0])
    wgu = wgu.at[GLA_GATE_RANK:2 * GLA_GATE_RANK, dk:].set(gla_wg_up[0, 1]).astype(BF16)
    bg = gla_bg[0].reshape(1, 2 * dk)
    gq, gk, gv, gr, gg = _gla_proj_call(stream, mod1, gla_w_in[0].astype(BF16), wgd, wgu, bg, ctx_row)
    o_f, o_b = _gla_scan_call(gq, gk, gv, gg)
    x1, ri, rw, counts = _gla_out_call(o_f, o_b, gr, stream, mod1, gla_norm_g[0][None], gla_w_o[0].astype(BF16),
                                       ln_g[1, 0][None], ln_b[1, 0][None], rwt, rb)
    return _moe(x1, mod1, ri, rw, counts, ln_g[1, 1][None], ln_b[1, 1][None],
                moe_w_gate, moe_w_up, moe_w_down, 1, n_lat, ctx_row, n_lat)
```

```python
import functools
import math

import numpy as np
import jax
import jax.numpy as jnp
from jax import lax
from jax.experimental import pallas as pl
from jax.experimental.pallas import tpu as pltpu

F32 = jnp.float32
BF16 = jnp.bfloat16
I32 = jnp.int32

DEPTH = 2
ALPHA = (2.0 * DEPTH) ** 0.25
LN_EPS = 1e-5
GRID_W = 64
ROPE_BASE = 10000.0
DIFF_HEADS = 8
DIFF_HEAD_DIM = 64
ROT_FREQS = DIFF_HEAD_DIM // 4
GLA_HEADS = 4
GLA_DK = 128
GLA_DV = 256
GLA_GATE_RANK = 16
GLA_TAU = 16.0
GLA_CHUNK = 64
GLA_SUB = 16
N_EXPERTS = 16
N_GROUPS = 4
EXPERTS_PER_GROUP = 4
LOG2E = 1.4426950408889634

LANES = 128
TM = 256
TK = 256
TQ = 512
MOE_BLK = 512
FFN_SLICES = 2
ADA_TN = 1536
ROW_GROUP = 16
STAGE_ROWS = 768
TAB_ROWS, TAB_LANES = 3, 32
VMEM_LIMIT = 48 * 1024 * 1024
NEG_BIG = -1e30
SOFTMAX_HEADROOM = 64.0
BOUND_SLACK = 1.001
SOFTMAX_MIN_SUM = 2.0 ** -40


def _cparams(sem):
    return pltpu.CompilerParams(dimension_semantics=sem, vmem_limit_bytes=VMEM_LIMIT)


def _silu(x):
    return x * (1.0 / (1.0 + jnp.exp(-x)))


def _layer_norm(z, g, b):
    mu = jnp.mean(z, axis=-1, keepdims=True)
    zc = z - mu
    var = jnp.mean(zc * zc, axis=-1, keepdims=True)
    return zc * lax.rsqrt(var + LN_EPS) * g + b


def _dot(a, b):
    return jnp.dot(a, b, preferred_element_type=F32)


def _dot_nt(a, b, precision=None):
    return lax.dot_general(a, b, (((1,), (1,)), ((), ())), preferred_element_type=F32, precision=precision)


def _dot_tn(a, b):
    return lax.dot_general(a, b, (((0,), (0,)), ((), ())), preferred_element_type=F32)


def _ada_kernel(c_ref, w_ref, b_ref, o_ref):
    s = _silu(c_ref[...])
    o_ref[0] = jnp.dot(s, w_ref[0], preferred_element_type=F32, precision=lax.Precision.HIGHEST) + b_ref[0]


def _ada_call(cond, ada_w, ada_b):
    depth, d, n = ada_w.shape
    rows = cond.shape[0]
    return pl.pallas_call(
        _ada_kernel,
        out_shape=jax.ShapeDtypeStruct((depth, rows, n), F32),
        grid=(depth, n // ADA_TN),
        in_specs=[
            pl.BlockSpec((rows, d), lambda i, j: (0, 0)),
            pl.BlockSpec((1, d, ADA_TN), lambda i, j: (i, 0, j)),
            pl.BlockSpec((1, 1, ADA_TN), lambda i, j: (i, 0, j)),
        ],
        out_specs=pl.BlockSpec((1, rows, ADA_TN), lambda i, j: (i, 0, j)),
        compiler_params=_cparams(("parallel", "parallel")),
        name="ada_mod",
    )(cond, ada_w, ada_b.reshape(depth, 1, n))


def _qkv_kernel(xl_ref, xc_ref, mod_ref, wqk_ref, wvt_ref, cos_ref, sa_ref, sb_ref, q_ref, k_ref, vt_ref, *, d, n_lat):
    mod = mod_ref[0]
    x = jnp.where(pl.program_id(1) < n_lat, xl_ref[0], xc_ref[0])
    h = (x * (1.0 + mod[:, d:2 * d]) + mod[:, 0:d]).astype(BF16)
    qk = _dot(h, wqk_ref[...])
    vt_ref[0, 0] = _dot_nt(wvt_ref[...], h).astype(BF16)
    cos, sa, sb = cos_ref[...], sa_ref[...], sb_ref[...]
    q_scale = DIFF_HEAD_DIM ** -0.5 * LOG2E
    for s in range(2 * d // LANES):
        z = qk[:, s * LANES:(s + 1) * LANES]
        zr = z * cos + pltpu.roll(z, LANES - ROT_FREQS, axis=1) * sa + pltpu.roll(z, ROT_FREQS, axis=1) * sb
        if s < d // LANES:
            q_ref[0, :, s * LANES:(s + 1) * LANES] = (zr * q_scale).astype(BF16)
        else:
            k_ref[0, :, (s * LANES - d):(s * LANES - d + LANES)] = zr.astype(BF16)


def _rope_tables(s_len, c_len):
    rows = s_len // GRID_W
    row = np.repeat(np.arange(rows, dtype=np.float32), GRID_W)
    col = np.tile(np.arange(GRID_W, dtype=np.float32), rows)
    inv_freq = np.float32(ROPE_BASE) ** (-np.arange(ROT_FREQS, dtype=np.float32) / np.float32(ROT_FREQS))
    ang_r = row[:, None] * inv_freq
    ang_c = col[:, None] * inv_freq
    ang = np.concatenate([ang_r, ang_r, ang_c, ang_c], axis=-1).astype(np.float64)
    cos, sin = np.cos(ang), np.sin(ang)
    first_half = (np.arange(DIFF_HEAD_DIM) // ROT_FREQS) % 2 == 0
    sa = np.where(first_half[None, :], -sin, 0.0)
    sb = np.where(first_half[None, :], 0.0, sin)
    pad = lambda t, fill: np.concatenate([t, np.full((c_len, DIFF_HEAD_DIM), fill)], axis=0)
    two = lambda t: np.concatenate([t, t], axis=-1).astype(np.float32)
    return two(pad(cos, 1.0)), two(pad(sa, 0.0)), two(pad(sb, 0.0))


def _qkv_call(x, ctx, mod, wqk, wvt, tables, ctx_row):
    b, s_len, d = x.shape
    n_lat = s_len // TM
    nt = n_lat + 1
    t = nt * TM
    mod_idx = lambda bi, j: (jnp.where(j < n_lat, bi, ctx_row), 0, 0)
    return pl.pallas_call(
        functools.partial(_qkv_kernel, d=d, n_lat=n_lat),
        out_shape=(jax.ShapeDtypeStruct((b, t, d), BF16), jax.ShapeDtypeStruct((b, t, d), BF16),
                   jax.ShapeDtypeStruct((b, nt, d, TM), BF16)),
        grid=(b, nt),
        in_specs=[
            pl.BlockSpec((1, TM, d), lambda bi, j: (bi, jnp.minimum(j, n_lat - 1), 0)),
            pl.BlockSpec((1, TM, d), lambda bi, j: (bi, 0, 0)),
            pl.BlockSpec((1, 1, 6 * d), mod_idx),
            pl.BlockSpec((d, 2 * d), lambda bi, j: (0, 0)),
            pl.BlockSpec((d, d), lambda bi, j: (0, 0)),
            pl.BlockSpec((TM, LANES), lambda bi, j: (j, 0)),
            pl.BlockSpec((TM, LANES), lambda bi, j: (j, 0)),
            pl.BlockSpec((TM, LANES), lambda bi, j: (j, 0)),
        ],
        out_specs=(pl.BlockSpec((1, TM, d), lambda bi, j: (bi, j, 0)),
                   pl.BlockSpec((1, TM, d), lambda bi, j: (bi, j, 0)),
                   pl.BlockSpec((1, 1, d, TM), lambda bi, j: (bi, j, 0, 0))),
        compiler_params=_cparams(("parallel", "parallel")),
        name="qkv_rope",
    )(x, ctx, mod, wqk, wvt, *tables)


def _attn_kernel(q_ref, k_ref, vt_ref, lam_ref, g_ref, o_ref, kn_ref, *, chunks, lambda_init, tq):
    q = q_ref[0]
    lane = lax.broadcasted_iota(I32, q.shape, 1)
    zero = jnp.zeros_like(q)
    qq = jnp.concatenate([jnp.where(lane < DIFF_HEAD_DIM, q, zero),
                          jnp.where(lane >= DIFF_HEAD_DIM, q, zero)], axis=0)
    k_chunk = lambda c: k_ref[0, c * TK:(c + 1) * TK, :]

    def finish(acc, l):
        acc = acc / l
        lv = lam_ref[...]
        lam = (jnp.exp(jnp.sum(lv[0:1] * lv[1:2], axis=1, keepdims=True))
               - jnp.exp(jnp.sum(lv[2:3] * lv[3:4], axis=1, keepdims=True)) + lambda_init)
        ot = acc[:, :tq] - lam * acc[:, tq:]
        ms = jnp.mean(ot * ot, axis=0, keepdims=True)
        ot = ot * lax.rsqrt(ms + LN_EPS) * g_ref[...] * (1.0 - lambda_init)
        o_ref[0] = ot.T.astype(BF16)

    @pl.when(pl.program_id(2) == 0)
    def _():
        klane = lax.broadcasted_iota(I32, (TK, LANES), 1)
        best1 = jnp.zeros((1, 1), F32)
        best2 = jnp.zeros((1, 1), F32)
        for c in chunks:
            kf = k_chunk(c).astype(F32)
            ksq = kf * kf
            n1 = jnp.sum(jnp.where(klane < DIFF_HEAD_DIM, ksq, 0.0), axis=1, keepdims=True)
            n2 = jnp.sum(jnp.where(klane >= DIFF_HEAD_DIM, ksq, 0.0), axis=1, keepdims=True)
            best1 = jnp.maximum(best1, jnp.max(n1, axis=0, keepdims=True))
            best2 = jnp.maximum(best2, jnp.max(n2, axis=0, keepdims=True))
        kn_ref[...] = jnp.where(lax.broadcasted_iota(I32, kn_ref.shape, 1) < DIFF_HEAD_DIM, best1, best2)

    kn = kn_ref[0:1, :]
    qf = qq.astype(F32)
    qsq = jnp.sum(qf * qf, axis=1, keepdims=True)
    half = lax.broadcasted_iota(I32, (2 * tq, 1), 0) < tq
    bound = jnp.sqrt(qsq * jnp.where(half, kn[:, 0:1], kn[:, DIFF_HEAD_DIM:DIFF_HEAD_DIM + 1])) * BOUND_SLACK
    shift = SOFTMAX_HEADROOM - bound
    lane2 = lax.broadcasted_iota(I32, (2 * tq, LANES), 1)
    q_aug = jnp.concatenate([qq, jnp.where(lane2 == 0, shift, 0.0).astype(BF16)], axis=1)
    one_lane = jnp.where(lax.broadcasted_iota(I32, (TK, LANES), 1) == 0, 1.0, 0.0).astype(BF16)
    scores = lambda c: _dot_nt(jnp.concatenate([k_chunk(c), one_lane], axis=1), q_aug)
    acc = jnp.zeros((LANES, 2 * tq), F32)
    l = jnp.zeros((1, 2 * tq), F32)
    st_next = scores(chunks[0])
    for i, c in enumerate(chunks):
        st = st_next
        if i + 1 < len(chunks):
            st_next = scores(chunks[i + 1])
        p = jnp.exp2(st)
        l = l + jnp.sum(p, axis=0, keepdims=True)
        acc = acc + _dot(vt_ref[0, c], p.astype(BF16))
    finish(acc, l)
    underflowed = jnp.sum(jnp.where(l > SOFTMAX_MIN_SUM, 0.0, 1.0)) > 0.0

    @pl.when(underflowed)
    def _():
        m = jnp.full((1, 2 * tq), NEG_BIG, F32)
        l = jnp.zeros((1, 2 * tq), F32)
        acc = jnp.zeros((LANES, 2 * tq), F32)
        for c in chunks:
            st = _dot_nt(k_chunk(c), qq)
            m_new = jnp.maximum(m, jnp.max(st, axis=0, keepdims=True))
            alpha = jnp.exp2(m - m_new)
            p = jnp.exp2(st - m_new)
            l = l * alpha + jnp.sum(p, axis=0, keepdims=True)
            acc = acc * alpha + _dot(vt_ref[0, c], p.astype(BF16))
            m = m_new
        finish(acc, l)


def _attn_call(q, k, vt, lam_vecs, subln_g, lambda_init, *, tq, q_row0, n_rows, chunk0, n_chunks, name):
    b, t, d = q.shape
    q0 = q_row0 // tq
    kb = chunk0 // n_chunks
    assert q_row0 % tq == 0 and chunk0 % n_chunks == 0
    return pl.pallas_call(
        functools.partial(_attn_kernel, chunks=tuple(range(n_chunks)), lambda_init=lambda_init, tq=tq),
        out_shape=jax.ShapeDtypeStruct((b, n_rows, d), BF16),
        grid=(b, DIFF_HEADS, n_rows // tq),
        in_specs=[
            pl.BlockSpec((1, tq, LANES), lambda bi, h, qi: (bi, q0 + qi, h)),
            pl.BlockSpec((1, n_chunks * TK, LANES), lambda bi, h, qi: (bi, kb, h)),
            pl.BlockSpec((1, n_chunks, LANES, TM), lambda bi, h, qi: (bi, kb, h, 0)),
            pl.BlockSpec((4, DIFF_HEAD_DIM), lambda bi, h, qi: (0, 0)),
            pl.BlockSpec((LANES, 1), lambda bi, h, qi: (0, 0)),
        ],
        out_specs=pl.BlockSpec((1, tq, LANES), lambda bi, h, qi: (bi, qi, h)),
        scratch_shapes=[pltpu.VMEM((8, LANES), F32)],
        compiler_params=_cparams(("parallel", "parallel", "arbitrary")),
        name=name,
    )(q, k, vt, lam_vecs, subln_g.reshape(LANES, 1))


def _route(logits_t, rb, ri_ref, rw_ref, cnt_ref, first):
    tm = logits_t.shape[1]
    aff = 1.0 / (1.0 + jnp.exp(-logits_t))
    sel = aff + rb
    row = lambda a, e: a[e:e + 1, :]
    gscore = []
    for g in range(N_GROUPS):
        a, b, c, d = (row(sel, g * EXPERTS_PER_GROUP + j) for j in range(EXPERTS_PER_GROUP))
        hi1, lo1, hi2, lo2 = jnp.maximum(a, b), jnp.minimum(a, b), jnp.maximum(c, d), jnp.minimum(c, d)
        gscore.append(jnp.maximum(hi1, hi2) + jnp.maximum(jnp.minimum(hi1, hi2), jnp.maximum(lo1, lo2)))
    best, grp = gscore[0], jnp.zeros((1, tm), I32)
    for g in range(1, N_GROUPS):
        better = gscore[g] > best
        best = jnp.where(better, gscore[g], best)
        grp = jnp.where(better, g, grp)
    in_sel, in_aff = [], []
    for j in range(EXPERTS_PER_GROUP):
        s_j, a_j = row(sel, j), row(aff, j)
        for g in range(1, N_GROUPS):
            s_j = jnp.where(grp == g, row(sel, g * EXPERTS_PER_GROUP + j), s_j)
            a_j = jnp.where(grp == g, row(aff, g * EXPERTS_PER_GROUP + j), a_j)
        in_sel.append(s_j)
        in_aff.append(a_j)
    v0, i0, a0 = in_sel[0], jnp.zeros((1, tm), I32), in_aff[0]
    for j in range(1, EXPERTS_PER_GROUP):
        better = in_sel[j] > v0
        v0 = jnp.where(better, in_sel[j], v0)
        i0 = jnp.where(better, j, i0)
        a0 = jnp.where(better, in_aff[j], a0)
    v1, i1, a1 = jnp.full((1, tm), -jnp.inf, F32), jnp.zeros((1, tm), I32), jnp.zeros((1, tm), F32)
    for j in range(EXPERTS_PER_GROUP):
        better = (in_sel[j] > v1) & (i0 != j)
        v1 = jnp.where(better, in_sel[j], v1)
        i1 = jnp.where(better, j, i1)
        a1 = jnp.where(better, in_aff[j], a1)
    e0 = grp * EXPERTS_PER_GROUP + i0
    e1 = grp * EXPERTS_PER_GROUP + i1
    wsum = a0 + a1
    erow = lax.broadcasted_iota(I32, (N_EXPERTS, tm), 0)
    oh0 = (erow == e0).astype(F32)
    oh1 = (erow == e1).astype(F32)
    both = oh0 + oh1
    both_bf = both.astype(BF16)
    tri = (lax.broadcasted_iota(I32, (tm, tm), 0) < lax.broadcasted_iota(I32, (tm, tm), 1)).astype(BF16)
    before = _dot(both_bf, tri)
    round_up = lambda n: jnp.floor((n + (ROW_GROUP - 1.0)) * (1.0 / ROW_GROUP)) * ROW_GROUP
    run_col = round_up(jnp.sum(both, axis=1, keepdims=True))
    e_r = lax.broadcasted_iota(I32, (N_EXPERTS, N_EXPERTS), 0)
    e_c = lax.broadcasted_iota(I32, (N_EXPERTS, N_EXPERTS), 1)
    off_col = _dot((e_c < e_r).astype(BF16), jnp.broadcast_to(run_col, (N_EXPERTS, LANES)).astype(BF16))[:, 0:1]
    base = off_col + before
    s0 = jnp.sum(oh0 * base, axis=0, keepdims=True)
    s1 = jnp.sum(oh1 * base, axis=0, keepdims=True)
    both_pad = jnp.concatenate([both_bf, jnp.zeros((LANES - N_EXPERTS, tm), BF16)], axis=0)
    run_row = round_up(_dot_nt(jnp.ones((8, tm), BF16), both_pad))
    l_r = lax.broadcasted_iota(I32, (LANES, LANES), 0)
    l_c = lax.broadcasted_iota(I32, (LANES, LANES), 1)
    off_row = _dot(run_row.astype(BF16), (l_r < l_c).astype(BF16))
    lane = lax.broadcasted_iota(I32, (1, LANES), 1)
    groups = run_row[0:1] * (1.0 / ROW_GROUP)
    groups = jnp.where(lane == N_EXPERTS, jnp.sum(groups, axis=1, keepdims=True), groups)

    @pl.when(first)
    def _():
        cnt_ref[...] = jnp.zeros_like(cnt_ref)

    zero_half = jnp.zeros((1, tm - LANES), I32)
    ri_ref[0, 0:1, :] = s0.astype(I32)
    ri_ref[0, 1:2, :] = s1.astype(I32)
    ri_ref[0, 2:3, :] = e0
    ri_ref[0, 3:4, :] = e1
    for r, v in ((4, groups), (5, cnt_ref[0:1, :]), (6, off_row[0:1])):
        ri_ref[0, r:r + 1, 0:LANES] = v.astype(I32)
        ri_ref[0, r:r + 1, LANES:] = zero_half
    ri_ref[0, 7:8, :] = jnp.zeros((1, tm), I32)
    cnt_ref[...] = cnt_ref[...] + run_row
    rw_ref[0, 0:1, :] = a0 / wsum
    rw_ref[0, 1:2, :] = a1 / wsum
    rw_ref[0, 2:8, :] = jnp.zeros((6, tm), F32)


def _post_mixer(y, x, mod, lng, lnb, rwt, rb, x1_ref, ri_ref, rw_ref, cnt_ref, first, d):
    x1 = _layer_norm(ALPHA * x + mod[:, 2 * d:3 * d] * y, lng, lnb)
    x1_ref[0] = x1
    tok = x1 * (1.0 + mod[:, 4 * d:5 * d]) + mod[:, 3 * d:4 * d]
    t_hi = tok.astype(BF16)
    t_mid = (tok - t_hi.astype(F32)).astype(BF16)
    w_hi, w_mid = rwt[0], rwt[1]
    logits = _dot(t_hi, w_hi) + (_dot(t_hi, w_mid) + _dot(t_mid, w_hi))
    _route(logits.T[0:N_EXPERTS, :], rb, ri_ref, rw_ref, cnt_ref, first)


def _attn_out_kernel(ol_ref, oc_ref, xl_ref, xc_ref, mod_ref, wo_ref, lng_ref, lnb_ref, rwt_ref, rb_ref,
                     x1_ref, ri_ref, rw_ref, cnt_ref, *, d, n_lat):
    first = (pl.program_id(0) == 0) & (pl.program_id(1) == 0)
    latent = pl.program_id(1) < n_lat
    o = jnp.where(latent, ol_ref[0], oc_ref[0])
    x = jnp.where(latent, xl_ref[0], xc_ref[0])
    y = _dot(o, wo_ref[...])
    _post_mixer(y, x, mod_ref[0], lng_ref[...], lnb_ref[...], rwt_ref[...], rb_ref[...],
                x1_ref, ri_ref, rw_ref, cnt_ref, first, d)


def _route_out_shapes(b, n_tiles, t_out, d):
    return (jax.ShapeDtypeStruct((b, t_out, d), F32),
            jax.ShapeDtypeStruct((b * n_tiles, 8, TM), I32),
            jax.ShapeDtypeStruct((b * n_tiles, 8, TM), F32),
            jax.ShapeDtypeStruct((8, LANES), F32))


def _route_out_specs(n_tiles, d):
    return (pl.BlockSpec((1, TM, d), lambda bi, j: (bi, j, 0)),
            pl.BlockSpec((1, 8, TM), lambda bi, j: (bi * n_tiles + j, 0, 0)),
            pl.BlockSpec((1, 8, TM), lambda bi, j: (bi * n_tiles + j, 0, 0)),
            pl.BlockSpec((8, LANES), lambda bi, j: (0, 0)))


def _attn_out_call(o_lat, o_ctx, x, ctx, mod, wo, lng, lnb, rwt, rb, ctx_row):
    b, s_len, d = x.shape
    n_lat = s_len // TM
    nt = n_lat + 1
    row = lambda a: pl.BlockSpec(a.shape, lambda bi, j: (0,) * a.ndim)
    lat = pl.BlockSpec((1, TM, d), lambda bi, j: (bi, jnp.minimum(j, n_lat - 1), 0))
    cxt = pl.BlockSpec((1, TM, d), lambda bi, j: (bi, 0, 0))
    return pl.pallas_call(
        functools.partial(_attn_out_kernel, d=d, n_lat=n_lat),
        out_shape=_route_out_shapes(b, nt, nt * TM, d),
        grid=(b, nt),
        in_specs=[
            lat, cxt, lat, cxt,
            pl.BlockSpec((1, 1, 6 * d), lambda bi, j: (jnp.where(j < n_lat, bi, ctx_row), 0, 0)),
            row(wo), row(lng), row(lnb), row(rwt), row(rb),
        ],
        out_specs=_route_out_specs(nt, d),
        compiler_params=_cparams(("arbitrary", "arbitrary")),
        name="attn_out_route",
    )(o_lat, o_ctx, x, ctx, mod, wo, lng, lnb, rwt, rb)


def _group_copy(src, r_src, dst, r_dst, sem):
    return pltpu.make_async_copy(src.at[pl.ds(pl.multiple_of(r_src, ROW_GROUP), ROW_GROUP)],
                                 dst.at[pl.ds(pl.multiple_of(r_dst, ROW_GROUP), ROW_GROUP)], sem)


def _tab(tab_ref, tile, row, e):
    return tab_ref[(tile * TAB_ROWS + row) * TAB_LANES + e]


def _run_copies(tab_ref, pad_ref, tile, stage, hbm, sem, to_hbm):
    for e in range(N_EXPERTS):
        s0 = _tab(tab_ref, tile, 2, e)
        h0 = pad_ref[e] + _tab(tab_ref, tile, 1, e)

        def issue(g, carry):
            s, h = s0 + g * ROW_GROUP, h0 + g * ROW_GROUP
            (_group_copy(stage, s, hbm, h, sem) if to_hbm else _group_copy(hbm, h, stage, s, sem)).start()
            return carry

        lax.fori_loop(0, _tab(tab_ref, tile, 0, e), issue, 0)


def _run_waits(tab_ref, tile, stage, hbm, sem, to_hbm):
    def drain(g, carry):
        (_group_copy(stage, 0, hbm, 0, sem) if to_hbm else _group_copy(hbm, 0, stage, 0, sem)).wait()
        return carry

    lax.fori_loop(0, _tab(tab_ref, tile, 0, N_EXPERTS), drain, 0)


def _dispatch_kernel(tab_ref, pad_ref, fill_ref, x_ref, mod_ref, ri_ref, xs_ref, stage_ref, zero_ref, sems, sem_z,
                     *, d, n_steps):
    step = pl.program_id(0) * pl.num_programs(1) + pl.program_id(1)
    buf = step % 2
    mod = mod_ref[0]
    tok = (x_ref[0] * (1.0 + mod[:, 4 * d:5 * d]) + mod[:, 3 * d:4 * d]).astype(BF16)
    slot = lax.broadcasted_iota(I32, (STAGE_ROWS, TM), 0)
    ri = ri_ref[0]
    pick = jnp.where((slot == ri[0:1, :]) | (slot == ri[1:2, :]), 1.0, 0.0).astype(BF16)
    for bsel in range(2):
        @pl.when(buf == bsel)
        def _():
            stage_ref[bsel] = _dot(pick, tok).astype(BF16)
            _run_copies(tab_ref, pad_ref, step, stage_ref.at[bsel], xs_ref, sems.at[bsel], True)

    for bsel in range(2):
        @pl.when((buf != bsel) & (step > 0))
        def _():
            _run_waits(tab_ref, step - 1, stage_ref.at[bsel], xs_ref, sems.at[bsel], True)

    @pl.when(step == n_steps - 1)
    def _():
        for bsel in range(2):
            @pl.when(buf == bsel)
            def _():
                _run_waits(tab_ref, step, stage_ref.at[bsel], xs_ref, sems.at[bsel], True)

        zero_ref[...] = jnp.zeros_like(zero_ref)
        for e in range(N_EXPERTS + 1):
            lo, hi = fill_ref[2 * e], fill_ref[2 * e + 1]

            def fill(g, carry):
                _group_copy(zero_ref, 0, xs_ref, g * ROW_GROUP, sem_z).start()
                return carry

            def fill_wait(g, carry):
                _group_copy(zero_ref, 0, xs_ref, 0, sem_z).wait()
                return carry

            lax.fori_loop(lo, hi, fill, 0)
            lax.fori_loop(lo, hi, fill_wait, 0)


def _dispatch_call(tab, pad_start, fill, x1, mod, ri, n_tiles, n_pad, ctx_row, n_lat):
    b, t, d = x1.shape
    grid_spec = pltpu.PrefetchScalarGridSpec(
        num_scalar_prefetch=3,
        grid=(b, n_tiles),
        in_specs=[
            pl.BlockSpec((1, TM, d), lambda bi, j, *_: (bi, j, 0)),
            pl.BlockSpec((1, 1, 6 * d), lambda bi, j, *_: (jnp.where(j < n_lat, bi, ctx_row), 0, 0)),
            pl.BlockSpec((1, 8, TM), lambda bi, j, *_: (bi * n_tiles + j, 0, 0)),
        ],
        out_specs=pl.BlockSpec(memory_space=pl.ANY),
        scratch_shapes=[pltpu.VMEM((2, STAGE_ROWS, d), BF16), pltpu.VMEM((ROW_GROUP, d), BF16),
                        pltpu.SemaphoreType.DMA((2,)), pltpu.SemaphoreType.DMA],
    )
    return pl.pallas_call(
        functools.partial(_dispatch_kernel, d=d, n_steps=b * n_tiles),
        out_shape=jax.ShapeDtypeStruct((n_pad, d), BF16),
        grid_spec=grid_spec,
        compiler_params=_cparams(("arbitrary", "arbitrary")),
        name="moe_dispatch",
    )(tab, pad_start, fill, x1, mod, ri)


def _ffn_kernel(be_ref, nu_ref, xs_ref, wg_ref, wu_ref, wd_ref, ys_ref, wg_s, wu_s, wd_s):
    i = pl.program_id(0)

    @pl.when(i < nu_ref[0])
    def _():
        prev = be_ref[jnp.maximum(i - 1, 0)]

        @pl.when((i == 0) | (be_ref[i] != prev))
        def _():
            wg_s[...] = wg_ref[0, 0].astype(BF16)
            wu_s[...] = wu_ref[0, 0].astype(BF16)
            wd_s[...] = wd_ref[0, 0].astype(BF16)

        rows = [slice(r, r + MOE_BLK // FFN_SLICES) for r in range(0, MOE_BLK, MOE_BLK // FFN_SLICES)]
        gate_up = [(_dot(xs_ref[r, :], wg_s[...]), _dot(xs_ref[r, :], wu_s[...])) for r in rows]
        for r, (g, u) in zip(rows, gate_up):
            ys_ref[r, :] = _dot((_silu(g) * u).astype(BF16), wd_s[...]).astype(BF16)

    @pl.when(i >= nu_ref[0])
    def _():
        ys_ref[...] = jnp.zeros_like(ys_ref)


def _ffn_call(block_expert, n_used, xs, w_gate, w_up, w_down, layer):
    n_pad, d = xs.shape
    de = w_gate.shape[-1]
    n_blocks = n_pad // MOE_BLK
    blk = lambda i, be, nu: (jnp.minimum(i, nu[0] - 1), 0)
    exp = lambda i, be, nu: (layer, be[jnp.minimum(i, nu[0] - 1)], 0, 0)
    grid_spec = pltpu.PrefetchScalarGridSpec(
        num_scalar_prefetch=2,
        grid=(n_blocks,),
        in_specs=[
            pl.BlockSpec((MOE_BLK, d), blk),
            pl.BlockSpec((1, 1, d, de), exp),
            pl.BlockSpec((1, 1, d, de), exp),
            pl.BlockSpec((1, 1, de, d), exp),
        ],
        out_specs=pl.BlockSpec((MOE_BLK, d), lambda i, be, nu: (i, 0)),
        scratch_shapes=[pltpu.VMEM((d, de), BF16), pltpu.VMEM((d, de), BF16), pltpu.VMEM((de, d), BF16)],
    )
    return pl.pallas_call(
        _ffn_kernel,
        out_shape=jax.ShapeDtypeStruct((n_pad, d), BF16),
        grid_spec=grid_spec,
        compiler_params=_cparams(("arbitrary",)),
        name="moe_ffn",
    )(block_expert, n_used, xs, w_gate, w_up, w_down)


def _combine_kernel(tab_ref, pad_ref, x_ref, mod_ref, rit_ref, rw_ref, lng_ref, lnb_ref, ys_ref, o_ref,
                    stage_ref, sems, *, d, n_steps):
    step = pl.program_id(0) * pl.num_programs(1) + pl.program_id(1)
    buf = step % 2

    @pl.when(step == 0)
    def _():
        stage_ref[...] = jnp.zeros_like(stage_ref)
        _run_copies(tab_ref, pad_ref, step, stage_ref.at[0], ys_ref, sems.at[0], False)

    for bsel in range(2):
        @pl.when((buf != bsel) & (step + 1 < n_steps))
        def _():
            _run_copies(tab_ref, pad_ref, step + 1, stage_ref.at[bsel], ys_ref, sems.at[bsel], False)

    rit = rit_ref[0]
    lane = lax.broadcasted_iota(I32, (TM, STAGE_ROWS), 1)
    pick0 = jnp.where(lane == rit[:, 0:1], 1.0, 0.0).astype(BF16)
    pick1 = jnp.where(lane == rit[:, 1:2], 1.0, 0.0).astype(BF16)
    w = rw_ref[0]
    mod = mod_ref[0]
    for bsel in range(2):
        @pl.when(buf == bsel)
        def _():
            _run_waits(tab_ref, step, stage_ref.at[bsel], ys_ref, sems.at[bsel], False)
            stage = stage_ref[bsel]
            moe = w[:, 0:1] * _dot(pick0, stage) + w[:, 1:2] * _dot(pick1, stage)
            o_ref[0] = _layer_norm(ALPHA * x_ref[0] + mod[:, 5 * d:6 * d] * moe, lng_ref[...], lnb_ref[...])


def _combine_call(tab, pad_start, x1, mod, ri_t, rw_t, lng, lnb, ys, n_tiles, ctx_row, n_lat):
    b, _, d = x1.shape
    row = lambda a: pl.BlockSpec(a.shape, lambda bi, j, *_: (0,) * a.ndim)
    grid_spec = pltpu.PrefetchScalarGridSpec(
        num_scalar_prefetch=2,
        grid=(b, n_tiles),
        in_specs=[
            pl.BlockSpec((1, TM, d), lambda bi, j, *_: (bi, j, 0)),
            pl.BlockSpec((1, 1, 6 * d), lambda bi, j, *_: (jnp.where(j < n_lat, bi, ctx_row), 0, 0)),
            pl.BlockSpec((1, TM, 8), lambda bi, j, *_: (bi * n_tiles + j, 0, 0)),
            pl.BlockSpec((1, TM, 8), lambda bi, j, *_: (bi * n_tiles + j, 0, 0)),
            row(lng), row(lnb),
            pl.BlockSpec(memory_space=pl.ANY),
        ],
        out_specs=pl.BlockSpec((1, TM, d), lambda bi, j, *_: (bi, j, 0)),
        scratch_shapes=[pltpu.VMEM((2, STAGE_ROWS, d), BF16), pltpu.SemaphoreType.DMA((2,))],
    )
    return pl.pallas_call(
        functools.partial(_combine_kernel, d=d, n_steps=b * n_tiles),
        out_shape=jax.ShapeDtypeStruct((b, n_tiles * TM, d), F32),
        grid_spec=grid_spec,
        compiler_params=_cparams(("arbitrary", "arbitrary")),
        name="moe_combine",
    )(tab, pad_start, x1, mod, ri_t, rw_t, lng, lnb, ys)


def _moe(x1, mod, ri, rw, counts, lng, lnb, w_gate, w_up, w_down, layer, n_tiles, ctx_row, n_lat):
    b, _, d = x1.shape
    n_steps = b * n_tiles
    n_rows_max = 2 * n_steps * TM + (ROW_GROUP - 1) * N_EXPERTS * n_steps
    n_pad = -(-n_rows_max // MOE_BLK) * MOE_BLK + N_EXPERTS * MOE_BLK
    n_blocks = n_pad // MOE_BLK
    cnt = counts[0, :N_EXPERTS].astype(I32)
    padded = (cnt + MOE_BLK - 1) // MOE_BLK * MOE_BLK
    pad_end = jnp.cumsum(padded)
    pad_start = jnp.concatenate([jnp.zeros((1,), I32), pad_end]).astype(I32)
    block_row0 = jnp.arange(n_blocks, dtype=I32) * MOE_BLK
    block_expert = jnp.minimum(
        jnp.sum((pad_end[None, :] <= block_row0[:, None]).astype(I32), axis=1), N_EXPERTS - 1).astype(I32)
    n_used = (pad_end[-1:] // MOE_BLK).astype(I32)
    fill_lo = jnp.concatenate([pad_start[:-1] + cnt, pad_end[-1:]]) // ROW_GROUP
    fill_hi = jnp.concatenate([pad_end, jnp.full((1,), n_pad, I32)]) // ROW_GROUP
    fill = jnp.stack([fill_lo, fill_hi], axis=1).reshape(-1).astype(I32)
    tab = ri[:, 4:4 + TAB_ROWS, :TAB_LANES].reshape(-1)
    xs = _dispatch_call(tab, pad_start, fill, x1, mod, ri, n_tiles, n_pad, ctx_row, n_lat)
    ys = _ffn_call(block_expert, n_used, xs, w_gate, w_up, w_down, layer)
    ri_t, rw_t = jnp.swapaxes(ri, 1, 2), jnp.swapaxes(rw, 1, 2)
    return _combine_call(tab, pad_start, x1, mod, ri_t, rw_t, lng, lnb, ys, n_tiles, ctx_row, n_lat)


def _gla_proj_kernel(x_ref, mod_ref, win_ref, wgd_ref, wgu_ref, bg_ref, q_ref, k_ref, v_ref, r_ref, g_ref, *, d):
    mod = mod_ref[0]
    h = (x_ref[0] * (1.0 + mod[:, d:2 * d]) + mod[:, 0:d]).astype(BF16)
    dk = GLA_HEADS * GLA_DK
    p = _dot(h, win_ref[...])
    q_ref[0] = (p[:, 0:dk] * GLA_DK ** -0.5).astype(BF16)
    k_ref[0] = p[:, dk:2 * dk].astype(BF16)
    v_ref[0] = p[:, 2 * dk:2 * dk + d].astype(BF16)
    r_ref[0] = p[:, 2 * dk + d:].astype(BF16)
    low = _dot(h, wgd_ref[...]).astype(BF16)
    pre = _dot(low, wgu_ref[...]) + bg_ref[...]
    g_ref[0] = (jnp.minimum(pre, 0.0) - jnp.log1p(jnp.exp(-jnp.abs(pre)))) * (1.0 / GLA_TAU)


def _gla_proj_call(xs, mod, win, wgd, wgu, bg, ctx_row):
    b, t, d = xs.shape
    nt = t // TM
    n_lat = nt - 1
    dk = GLA_HEADS * GLA_DK
    row = lambda a: pl.BlockSpec(a.shape, lambda bi, j: (0,) * a.ndim)
    tile = lambda n: pl.BlockSpec((1, TM, n), lambda bi, j: (bi, j, 0))
    return pl.pallas_call(
        functools.partial(_gla_proj_kernel, d=d),
        out_shape=(jax.ShapeDtypeStruct((b, t, dk), BF16), jax.ShapeDtypeStruct((b, t, dk), BF16),
                   jax.ShapeDtypeStruct((b, t, d), BF16), jax.ShapeDtypeStruct((b, t, d), BF16),
                   jax.ShapeDtypeStruct((b, t, 2 * dk), F32)),
        grid=(b, nt),
        in_specs=[tile(d), pl.BlockSpec((1, 1, 6 * d), lambda bi, j: (jnp.where(j < n_lat, bi, ctx_row), 0, 0)),
                  row(win), row(wgd), row(wgu), row(bg)],
        out_specs=(tile(dk), tile(dk), tile(d), tile(d), tile(2 * dk)),
        compiler_params=_cparams(("parallel", "parallel")),
        name="gla_proj",
    )(xs, mod, win, wgd, wgu, bg)


def _split3(x):
    hi = x.astype(BF16)
    r1 = x - hi.astype(F32)
    mid = r1.astype(BF16)
    lo = (r1 - mid.astype(F32)).astype(BF16)
    return hi, mid, lo


class _GlaMasks:
    def __init__(self, reverse):
        L = GLA_CHUNK
        t = lax.broadcasted_iota(I32, (L, L), 0)
        s = lax.broadcasted_iota(I32, (L, L), 1)
        seen = (s >= t) if reverse else (s <= t)
        self.tri = seen.astype(BF16)
        blk = lambda r: jnp.right_shift(r, GLA_SUB.bit_length() - 1)
        dist = (blk(s) - blk(t)) if reverse else (blk(t) - blk(s))
        self.pairs = [seen & (dist == 0)] + [dist == n for n in range(1, L // GLA_SUB)]


def _gla_chunk_matmuls(q_ref, k_ref, v_ref, b_all, rows, h, reverse):
    L, SB, DK = GLA_CHUNK, GLA_SUB, GLA_DK
    nb = L // SB
    kcols = slice(h * DK, (h + 1) * DK)
    vcols = slice(h * GLA_DV, (h + 1) * GLA_DV)
    bh = b_all[:, kcols]
    zero = jnp.zeros((1, DK), F32)
    earlier = lambda i, n: i + n if reverse else i - n

    def start_ref(i):
        if not 0 <= i < nb or i == (nb - 1 if reverse else 0):
            return zero
        r = (i + 1) * SB if reverse else i * SB - 1
        return bh[r:r + 1, :]

    def end_ref(i):
        r = i * SB if reverse else i * SB + SB - 1
        return bh[r:r + 1, :]

    per_block = lambda f: jnp.concatenate([jnp.broadcast_to(f(i), (SB, DK)) for i in range(nb)], axis=0)
    bs, be = per_block(start_ref), per_block(end_ref)
    b_last = end_ref(0 if reverse else nb - 1)
    q = q_ref[0, rows, kcols].astype(F32)
    k = k_ref[0, rows, kcols].astype(F32)
    v = v_ref[0, rows, vcols]
    scaled = lambda x, e: (x * jnp.exp(e)).astype(BF16)
    q_own = scaled(q, bh - bs)
    k_own, k_end = scaled(k, bs - bh), scaled(k, be - bh)
    pieces = [_dot_nt(q_own, k_own), _dot_nt(q_own, k_end)]
    for n in range(2, nb):
        pieces.append(_dot_nt(scaled(q, bh - per_block(lambda i: start_ref(earlier(i, n - 1)))), k_end))
    update = _dot_tn(v, scaled(k, b_last - bh))
    return pieces, scaled(q, bh), update, b_last, v, vcols


def _gla_scan_kernel(qf, kf, vf, gf, qb, kb, vb, gb, of_ref, ob_ref, sf_ref, sb_ref):
    @pl.when(pl.program_id(1) == 0)
    def _():
        sf_ref[...] = jnp.zeros_like(sf_ref)
        sb_ref[...] = jnp.zeros_like(sb_ref)

    L = GLA_CHUNK
    n_chunks = TM // L
    dirs = ((qf, kf, vf, gf, of_ref, sf_ref, False, _GlaMasks(False)),
            (qb, kb, vb, gb, ob_ref, sb_ref, True, _GlaMasks(True)))
    gate_sums = [[sum(_dot(masks.tri, part) for part in _split3(g_ref[0, c * L:(c + 1) * L, :]))
                  for c in range(n_chunks)] for _, _, _, g_ref, _, _, _, masks in dirs]
    work = []
    for step in range(n_chunks):
        for d_i, (q_ref, k_ref, v_ref, g_ref, o_ref, state_ref, reverse, masks) in enumerate(dirs):
            c = n_chunks - 1 - step if reverse else step
            rows = slice(c * L, (c + 1) * L)
            for h in range(GLA_HEADS):
                work.append((o_ref, state_ref, rows, h, masks,
                             _gla_chunk_matmuls(q_ref, k_ref, v_ref, gate_sums[d_i][c], rows, h, reverse)))
    for o_ref, state_ref, rows, h, masks, (pieces, q_int, update, b_last, v, vcols) in work:
        a = sum(jnp.where(m, p, 0.0) for m, p in zip(masks.pairs, pieces)).astype(BF16)
        state_t = state_ref[h]
        o_ref[0, rows, vcols] = _dot(a, v) + _dot_nt(q_int, state_t.astype(BF16))
        state_ref[h] = jnp.exp(b_last) * state_t + update


def _gla_scan_call(q, k, v, g):
    b, t, d = v.shape
    nt = t // TM
    n_lat = nt - 1
    dk = GLA_HEADS * GLA_DK
    fwd = lambda j: jnp.where(j == 0, n_lat, j - 1)
    bwd = lambda j: jnp.where(j == 0, n_lat, n_lat - j)
    spec = lambda n, idx, col: pl.BlockSpec((1, TM, n), lambda bi, j: (bi, idx(j), col))
    return pl.pallas_call(
        _gla_scan_kernel,
        out_shape=(jax.ShapeDtypeStruct((b, n_lat * TM, d), F32), jax.ShapeDtypeStruct((b, n_lat * TM, d), F32)),
        grid=(b, nt),
        in_specs=[spec(dk, fwd, 0), spec(dk, fwd, 0), spec(d, fwd, 0), spec(dk, fwd, 0),
                  spec(dk, bwd, 0), spec(dk, bwd, 0), spec(d, bwd, 0), spec(dk, bwd, 1)],
        out_specs=(pl.BlockSpec((1, TM, d), lambda bi, j: (bi, jnp.maximum(j - 1, 0), 0)),
                   pl.BlockSpec((1, TM, d), lambda bi, j: (bi, n_lat - jnp.maximum(j, 1), 0))),
        scratch_shapes=[pltpu.VMEM((GLA_HEADS, GLA_DV, GLA_DK), F32), pltpu.VMEM((GLA_HEADS, GLA_DV, GLA_DK), F32)],
        compiler_params=_cparams(("parallel", "arbitrary")),
        name="gla_scan",
    )(q, k, v, g, q, k, v, g)


def _gla_out_kernel(of_ref, ob_ref, r_ref, x_ref, mod_ref, ng_ref, wo_ref, lng_ref, lnb_ref, rwt_ref, rb_ref,
                    x1_ref, ri_ref, rw_ref, cnt_ref, *, d):
    first = (pl.program_id(0) == 0) & (pl.program_id(1) == 0)
    o = of_ref[0] + ob_ref[0]
    r = r_ref[0].astype(F32)
    gate = _silu(r)
    ng = ng_ref[...]
    parts = []
    for h in range(GLA_HEADS):
        cols = slice(h * GLA_DV, (h + 1) * GLA_DV)
        oh = o[:, cols]
        ms = jnp.mean(oh * oh, axis=-1, keepdims=True)
        parts.append((oh * lax.rsqrt(ms + LN_EPS) * ng * gate[:, cols]).astype(BF16))
    y = _dot(jnp.concatenate(parts, axis=-1), wo_ref[...])
    _post_mixer(y, x_ref[0], mod_ref[0], lng_ref[...], lnb_ref[...], rwt_ref[...], rb_ref[...],
                x1_ref, ri_ref, rw_ref, cnt_ref, first, d)


def _gla_out_call(o_f, o_b, r, xs, mod, ng, wo, lng, lnb, rwt, rb):
    b, s_len, d = o_f.shape
    n_lat = s_len // TM
    row = lambda a: pl.BlockSpec(a.shape, lambda bi, j: (0,) * a.ndim)
    tile = pl.BlockSpec((1, TM, d), lambda bi, j: (bi, j, 0))
    return pl.pallas_call(
        functools.partial(_gla_out_kernel, d=d),
        out_shape=_route_out_shapes(b, n_lat, s_len, d),
        grid=(b, n_lat),
        in_specs=[tile, tile, tile, tile, pl.BlockSpec((1, 1, 6 * d), lambda bi, j: (bi, 0, 0)),
                  row(ng), row(wo), row(lng), row(lnb), row(rwt), row(rb)],
        out_specs=_route_out_specs(n_lat, d),
        compiler_params=_cparams(("arbitrary", "arbitrary")),
        name="gla_out_route",
    )(o_f, o_b, r, xs, mod, ng, wo, lng, lnb, rwt, rb)


def kernel(x, c, ctx, c_ctx, ada_w, ada_b, ln_g, ln_b, attn_w_qkv, attn_w_o, attn_lambda, attn_subln_g, gla_w_in, gla_wg_down, gla_wg_up, gla_bg, gla_norm_g, gla_w_o, router_w, router_b, moe_w_gate, moe_w_up, moe_w_down):
    b, s_len, d = x.shape
    c_len = ctx.shape[1]
    assert ada_w.shape[0] == DEPTH and c_len == TM and s_len % TQ == 0 and TQ % TM == 0 and s_len % GRID_W == 0
    assert d == DIFF_HEADS * 2 * DIFF_HEAD_DIM == GLA_HEADS * GLA_DV
    assert STAGE_ROWS % LANES == 0 and STAGE_ROWS >= 2 * TM + (ROW_GROUP - 1) * N_EXPERTS
    n_lat = s_len // TM
    nt = n_lat + 1
    ctx_row = b

    cond_rows = -(-(b + 1) // 8) * 8
    cond = jnp.concatenate([c, c_ctx[None, :], jnp.zeros((cond_rows - b - 1, d), F32)], axis=0)
    mods = _ada_call(cond, ada_w, ada_b)
    rw_pad = jnp.concatenate([router_w, jnp.zeros((d, LANES - N_EXPERTS), F32)], axis=1)
    rw_hi = rw_pad.astype(BF16)
    rwt = jnp.stack([rw_hi, (rw_pad - rw_hi.astype(F32)).astype(BF16)])
    rb = router_b.reshape(N_EXPERTS, 1)

    mod0 = mods[0].reshape(cond_rows, 1, 6 * d)
    wqkv = attn_w_qkv[0]
    wqk = wqkv[:, :2 * d].astype(BF16)
    wvt = wqkv[:, 2 * d:].T.astype(BF16)
    tables = tuple(jnp.asarray(t) for t in _rope_tables(s_len, c_len))
    q, k, vt = _qkv_call(x, ctx, mod0, wqk, wvt, tables, ctx_row)
    lambda_init = 0.8 - 0.6 * math.exp(-0.3 * 0)
    attn = functools.partial(_attn_call, q, k, vt, attn_lambda[0], attn_subln_g[0], lambda_init)
    o_lat = attn(tq=TQ, q_row0=0, n_rows=s_len, chunk0=0, n_chunks=nt, name="diff_attn_lat")
    o_ctx = attn(tq=c_len, q_row0=s_len, n_rows=c_len, chunk0=n_lat, n_chunks=1, name="diff_attn_ctx")
    x1, ri, rw, counts = _attn_out_call(o_lat, o_ctx, x, ctx, mod0, attn_w_o[0].astype(BF16),
                                        ln_g[0, 0][None], ln_b[0, 0][None], rwt, rb, ctx_row)
    stream = _moe(x1, mod0, ri, rw, counts, ln_g[0, 1][None], ln_b[0, 1][None],
                  moe_w_gate, moe_w_up, moe_w_down, 0, nt, ctx_row, n_lat)

    mod1 = mods[1].reshape(cond_rows, 1, 6 * d)
    dk = GLA_HEADS * GLA_DK
    wgd = jnp.concatenate([gla_wg_down[0, 0], gla_wg_down[0, 1],
                           jnp.zeros((d, LANES - 2 * GLA_GATE_RANK), F32)], axis=1).astype(BF16)
    wgu = jnp.zeros((LANES, 2 * dk), F32)
    wgu = wgu.at[0:GLA_GATE_RANK, 0:dk].set(gla_wg_up[0, 0])
    wgu = wgu.at[GLA_GATE_RANK:2 * GLA_GATE_RANK, dk:].set(gla_wg_up[0, 1]).astype(BF16)
    bg = gla_bg[0].reshape(1, 2 * dk)
    gq, gk, gv, gr, gg = _gla_proj_call(stream, mod1, gla_w_in[0].astype(BF16), wgd, wgu, bg, ctx_row)
    o_f, o_b = _gla_scan_call(gq, gk, gv, gg)
    x1, ri, rw, counts = _gla_out_call(o_f, o_b, gr, stream, mod1, gla_norm_g[0][None], gla_w_o[0].astype(BF16),
                                       ln_g[1, 0][None], ln_b[1, 0][None], rwt, rb)
    return _moe(x1, mod1, ri, rw, counts, ln_g[1, 1][None], ln_b[1, 1][None],
                moe_w_gate, moe_w_up, moe_w_down, 1, n_lat, ctx_row, n_lat)
```

```python
import functools
import math

import numpy as np
import jax
import jax.numpy as jnp
from jax import lax
from jax.experimental import pallas as pl
from jax.experimental.pallas import tpu as pltpu

F32 = jnp.float32
BF16 = jnp.bfloat16
I32 = jnp.int32

DEPTH = 2
ALPHA = (2.0 * DEPTH) ** 0.25
LN_EPS = 1e-5
GRID_W = 64
ROPE_BASE = 10000.0
DIFF_HEADS = 8
DIFF_HEAD_DIM = 64
ROT_FREQS = DIFF_HEAD_DIM // 4
GLA_HEADS = 4
GLA_DK = 128
GLA_DV = 256
GLA_GATE_RANK = 16
GLA_TAU = 16.0
GLA_CHUNK = 64
GLA_SUB = 16
N_EXPERTS = 16
N_GROUPS = 4
EXPERTS_PER_GROUP = 4
LOG2E = 1.4426950408889634

LANES = 128
TM = 256
TK = 256
TQ = 512
ATTN_GROUP = 4
MOE_BLK = 512
FFN_SLICES = 2
ADA_TN = 1536
ROW_GROUP = 16
STAGE_ROWS = 768
TAB_ROWS, TAB_LANES = 3, 32
VMEM_LIMIT = 48 * 1024 * 1024
NEG_BIG = -1e30
SOFTMAX_HEADROOM = 64.0
BOUND_SLACK = 1.001
SOFTMAX_MIN_SUM = 2.0 ** -40


def _cparams(sem):
    return pltpu.CompilerParams(dimension_semantics=sem, vmem_limit_bytes=VMEM_LIMIT)


def _silu(x):
    return x * (1.0 / (1.0 + jnp.exp(-x)))


def _layer_norm(z, g, b):
    mu = jnp.mean(z, axis=-1, keepdims=True)
    zc = z - mu
    var = jnp.mean(zc * zc, axis=-1, keepdims=True)
    return zc * lax.rsqrt(var + LN_EPS) * g + b


def _dot(a, b):
    return jnp.dot(a, b, preferred_element_type=F32)


def _dot_nt(a, b, precision=None):
    return lax.dot_general(a, b, (((1,), (1,)), ((), ())), preferred_element_type=F32, precision=precision)


def _dot_tn(a, b):
    return lax.dot_general(a, b, (((0,), (0,)), ((), ())), preferred_element_type=F32)


def _ada_kernel(c_ref, w_ref, b_ref, o_ref):
    s = _silu(c_ref[...])
    o_ref[0] = jnp.dot(s, w_ref[0], preferred_element_type=F32, precision=lax.Precision.HIGHEST) + b_ref[0]


def _ada_call(cond, ada_w, ada_b):
    depth, d, n = ada_w.shape
    rows = cond.shape[0]
    return pl.pallas_call(
        _ada_kernel,
        out_shape=jax.ShapeDtypeStruct((depth, rows, n), F32),
        grid=(depth, n // ADA_TN),
        in_specs=[
            pl.BlockSpec((rows, d), lambda i, j: (0, 0)),
            pl.BlockSpec((1, d, ADA_TN), lambda i, j: (i, 0, j)),
            pl.BlockSpec((1, 1, ADA_TN), lambda i, j: (i, 0, j)),
        ],
        out_specs=pl.BlockSpec((1, rows, ADA_TN), lambda i, j: (i, 0, j)),
        compiler_params=_cparams(("parallel", "parallel")),
        name="ada_mod",
    )(cond, ada_w, ada_b.reshape(depth, 1, n))


def _qkv_kernel(xl_ref, xc_ref, mod_ref, wqk_ref, wvt_ref, cos_ref, sa_ref, sb_ref, q_ref, k_ref, vt_ref, *, d, n_lat):
    mod = mod_ref[0]
    x = jnp.where(pl.program_id(1) < n_lat, xl_ref[0], xc_ref[0])
    h = (x * (1.0 + mod[:, d:2 * d]) + mod[:, 0:d]).astype(BF16)
    qk = _dot(h, wqk_ref[...])
    vt_ref[0, 0] = _dot_nt(wvt_ref[...], h).astype(BF16)
    cos, sa, sb = cos_ref[...], sa_ref[...], sb_ref[...]
    q_scale = DIFF_HEAD_DIM ** -0.5 * LOG2E
    for s in range(2 * d // LANES):
        z = qk[:, s * LANES:(s + 1) * LANES]
        zr = z * cos + pltpu.roll(z, LANES - ROT_FREQS, axis=1) * sa + pltpu.roll(z, ROT_FREQS, axis=1) * sb
        if s < d // LANES:
            q_ref[0, :, s * LANES:(s + 1) * LANES] = (zr * q_scale).astype(BF16)
        else:
            k_ref[0, :, (s * LANES - d):(s * LANES - d + LANES)] = zr.astype(BF16)


def _rope_tables(s_len, c_len):
    rows = s_len // GRID_W
    row = np.repeat(np.arange(rows, dtype=np.float32), GRID_W)
    col = np.tile(np.arange(GRID_W, dtype=np.float32), rows)
    inv_freq = np.float32(ROPE_BASE) ** (-np.arange(ROT_FREQS, dtype=np.float32) / np.float32(ROT_FREQS))
    ang_r = row[:, None] * inv_freq
    ang_c = col[:, None] * inv_freq
    ang = np.concatenate([ang_r, ang_r, ang_c, ang_c], axis=-1).astype(np.float64)
    cos, sin = np.cos(ang), np.sin(ang)
    first_half = (np.arange(DIFF_HEAD_DIM) // ROT_FREQS) % 2 == 0
    sa = np.where(first_half[None, :], -sin, 0.0)
    sb = np.where(first_half[None, :], 0.0, sin)
    pad = lambda t, fill: np.concatenate([t, np.full((c_len, DIFF_HEAD_DIM), fill)], axis=0)
    two = lambda t: np.concatenate([t, t], axis=-1).astype(np.float32)
    return two(pad(cos, 1.0)), two(pad(sa, 0.0)), two(pad(sb, 0.0))


def _qkv_call(x, ctx, mod, wqk, wvt, tables, ctx_row):
    b, s_len, d = x.shape
    n_lat = s_len // TM
    nt = n_lat + 1
    t = nt * TM
    mod_idx = lambda bi, j: (jnp.where(j < n_lat, bi, ctx_row), 0, 0)
    return pl.pallas_call(
        functools.partial(_qkv_kernel, d=d, n_lat=n_lat),
        out_shape=(jax.ShapeDtypeStruct((b, t, d), BF16), jax.ShapeDtypeStruct((b, t, d), BF16),
                   jax.ShapeDtypeStruct((b, nt, d, TM), BF16)),
        grid=(b, nt),
        in_specs=[
            pl.BlockSpec((1, TM, d), lambda bi, j: (bi, jnp.minimum(j, n_lat - 1), 0)),
            pl.BlockSpec((1, TM, d), lambda bi, j: (bi, 0, 0)),
            pl.BlockSpec((1, 1, 6 * d), mod_idx),
            pl.BlockSpec((d, 2 * d), lambda bi, j: (0, 0)),
            pl.BlockSpec((d, d), lambda bi, j: (0, 0)),
            pl.BlockSpec((TM, LANES), lambda bi, j: (j, 0)),
            pl.BlockSpec((TM, LANES), lambda bi, j: (j, 0)),
            pl.BlockSpec((TM, LANES), lambda bi, j: (j, 0)),
        ],
        out_specs=(pl.BlockSpec((1, TM, d), lambda bi, j: (bi, j, 0)),
                   pl.BlockSpec((1, TM, d), lambda bi, j: (bi, j, 0)),
                   pl.BlockSpec((1, 1, d, TM), lambda bi, j: (bi, j, 0, 0))),
        compiler_params=_cparams(("parallel", "parallel")),
        name="qkv_rope",
    )(x, ctx, mod, wqk, wvt, *tables)


def _attn_kernel(q_ref, k_ref, vt_ref, lam_ref, g_ref, o_ref, kn_ref, *, chunks, lambda_init, tq):
    q = q_ref[0]
    lane = lax.broadcasted_iota(I32, q.shape, 1)
    zero = jnp.zeros_like(q)
    qq = jnp.concatenate([jnp.where(lane < DIFF_HEAD_DIM, q, zero),
                          jnp.where(lane >= DIFF_HEAD_DIM, q, zero)], axis=0)
    k_chunk = lambda c: k_ref[0, c * TK:(c + 1) * TK, :]

    def finish(acc, l):
        acc = acc / l
        lv = lam_ref[...]
        lam = (jnp.exp(jnp.sum(lv[0:1] * lv[1:2], axis=1, keepdims=True))
               - jnp.exp(jnp.sum(lv[2:3] * lv[3:4], axis=1, keepdims=True)) + lambda_init)
        ot = acc[:, :tq] - lam * acc[:, tq:]
        ms = jnp.mean(ot * ot, axis=0, keepdims=True)
        ot = ot * lax.rsqrt(ms + LN_EPS) * g_ref[...] * (1.0 - lambda_init)
        o_ref[0] = ot.T.astype(BF16)

    @pl.when(pl.program_id(2) == 0)
    def _():
        klane = lax.broadcasted_iota(I32, (TK, LANES), 1)
        best1 = jnp.zeros((1, 1), F32)
        best2 = jnp.zeros((1, 1), F32)
        for c in chunks:
            kf = k_chunk(c).astype(F32)
            ksq = kf * kf
            n1 = jnp.sum(jnp.where(klane < DIFF_HEAD_DIM, ksq, 0.0), axis=1, keepdims=True)
            n2 = jnp.sum(jnp.where(klane >= DIFF_HEAD_DIM, ksq, 0.0), axis=1, keepdims=True)
            best1 = jnp.maximum(best1, jnp.max(n1, axis=0, keepdims=True))
            best2 = jnp.maximum(best2, jnp.max(n2, axis=0, keepdims=True))
        kn_ref[...] = jnp.where(lax.broadcasted_iota(I32, kn_ref.shape, 1) < DIFF_HEAD_DIM, best1, best2)

    kn = kn_ref[0:1, :]
    qf = qq.astype(F32)
    qsq = jnp.sum(qf * qf, axis=1, keepdims=True)
    half = lax.broadcasted_iota(I32, (2 * tq, 1), 0) < tq
    bound = jnp.sqrt(qsq * jnp.where(half, kn[:, 0:1], kn[:, DIFF_HEAD_DIM:DIFF_HEAD_DIM + 1])) * BOUND_SLACK
    shift = SOFTMAX_HEADROOM - bound
    lane2 = lax.broadcasted_iota(I32, (2 * tq, LANES), 1)
    q_aug = jnp.concatenate([qq, jnp.where(lane2 == 0, shift, 0.0).astype(BF16)], axis=1)
    groups = [chunks[i:i + ATTN_GROUP] for i in range(0, len(chunks), ATTN_GROUP)]

    def scores(grp):
        kg = k_ref[0, grp[0] * TK:(grp[-1] + 1) * TK, :]
        one_lane = jnp.where(lax.broadcasted_iota(I32, kg.shape, 1) == 0, 1.0, 0.0).astype(BF16)
        return _dot_nt(jnp.concatenate([kg, one_lane], axis=1), q_aug)

    acc = jnp.zeros((LANES, 2 * tq), F32)
    l = jnp.zeros((1, 2 * tq), F32)
    st_next = scores(groups[0])
    for i, grp in enumerate(groups):
        st = st_next
        if i + 1 < len(groups):
            st_next = scores(groups[i + 1])
        p = jnp.exp2(st)
        l = l + jnp.sum(p, axis=0, keepdims=True)
        vt = jnp.concatenate([vt_ref[0, c] for c in grp], axis=1)
        acc = acc + _dot(vt, p.astype(BF16))
    finish(acc, l)
    underflowed = jnp.sum(jnp.where(l > SOFTMAX_MIN_SUM, 0.0, 1.0)) > 0.0

    @pl.when(underflowed)
    def _():
        m = jnp.full((1, 2 * tq), NEG_BIG, F32)
        l = jnp.zeros((1, 2 * tq), F32)
        acc = jnp.zeros((LANES, 2 * tq), F32)
        for c in chunks:
            st = _dot_nt(k_chunk(c), qq)
            m_new = jnp.maximum(m, jnp.max(st, axis=0, keepdims=True))
            alpha = jnp.exp2(m - m_new)
            p = jnp.exp2(st - m_new)
            l = l * alpha + jnp.sum(p, axis=0, keepdims=True)
            acc = acc * alpha + _dot(vt_ref[0, c], p.astype(BF16))
            m = m_new
        finish(acc, l)


def _attn_call(q, k, vt, lam_vecs, subln_g, lambda_init, *, tq, q_row0, n_rows, chunk0, n_chunks, name):
    b, t, d = q.shape
    q0 = q_row0 // tq
    kb = chunk0 // n_chunks
    assert q_row0 % tq == 0 and chunk0 % n_chunks == 0
    return pl.pallas_call(
        functools.partial(_attn_kernel, chunks=tuple(range(n_chunks)), lambda_init=lambda_init, tq=tq),
        out_shape=jax.ShapeDtypeStruct((b, n_rows, d), BF16),
        grid=(b, DIFF_HEADS, n_rows // tq),
        in_specs=[
            pl.BlockSpec((1, tq, LANES), lambda bi, h, qi: (bi, q0 + qi, h)),
            pl.BlockSpec((1, n_chunks * TK, LANES), lambda bi, h, qi: (bi, kb, h)),
            pl.BlockSpec((1, n_chunks, LANES, TM), lambda bi, h, qi: (bi, kb, h, 0)),
            pl.BlockSpec((4, DIFF_HEAD_DIM), lambda bi, h, qi: (0, 0)),
            pl.BlockSpec((LANES, 1), lambda bi, h, qi: (0, 0)),
        ],
        out_specs=pl.BlockSpec((1, tq, LANES), lambda bi, h, qi: (bi, qi, h)),
        scratch_shapes=[pltpu.VMEM((8, LANES), F32)],
        compiler_params=_cparams(("parallel", "parallel", "arbitrary")),
        name=name,
    )(q, k, vt, lam_vecs, subln_g.reshape(LANES, 1))


def _route(logits_t, rb, ri_ref, rw_ref, cnt_ref, first):
    tm = logits_t.shape[1]
    aff = 1.0 / (1.0 + jnp.exp(-logits_t))
    sel = aff + rb
    row = lambda a, e: a[e:e + 1, :]
    gscore = []
    for g in range(N_GROUPS):
        a, b, c, d = (row(sel, g * EXPERTS_PER_GROUP + j) for j in range(EXPERTS_PER_GROUP))
        hi1, lo1, hi2, lo2 = jnp.maximum(a, b), jnp.minimum(a, b), jnp.maximum(c, d), jnp.minimum(c, d)
        gscore.append(jnp.maximum(hi1, hi2) + jnp.maximum(jnp.minimum(hi1, hi2), jnp.maximum(lo1, lo2)))
    best, grp = gscore[0], jnp.zeros((1, tm), I32)
    for g in range(1, N_GROUPS):
        better = gscore[g] > best
        best = jnp.where(better, gscore[g], best)
        grp = jnp.where(better, g, grp)
    in_sel, in_aff = [], []
    for j in range(EXPERTS_PER_GROUP):
        s_j, a_j = row(sel, j), row(aff, j)
        for g in range(1, N_GROUPS):
            s_j = jnp.where(grp == g, row(sel, g * EXPERTS_PER_GROUP + j), s_j)
            a_j = jnp.where(grp == g, row(aff, g * EXPERTS_PER_GROUP + j), a_j)
        in_sel.append(s_j)
        in_aff.append(a_j)
    v0, i0, a0 = in_sel[0], jnp.zeros((1, tm), I32), in_aff[0]
    for j in range(1, EXPERTS_PER_GROUP):
        better = in_sel[j] > v0
        v0 = jnp.where(better, in_sel[j], v0)
        i0 = jnp.where(better, j, i0)
        a0 = jnp.where(better, in_aff[j], a0)
    v1, i1, a1 = jnp.full((1, tm), -jnp.inf, F32), jnp.zeros((1, tm), I32), jnp.zeros((1, tm), F32)
    for j in range(EXPERTS_PER_GROUP):
        better = (in_sel[j] > v1) & (i0 != j)
        v1 = jnp.where(better, in_sel[j], v1)
        i1 = jnp.where(better, j, i1)
        a1 = jnp.where(better, in_aff[j], a1)
    e0 = grp * EXPERTS_PER_GROUP + i0
    e1 = grp * EXPERTS_PER_GROUP + i1
    wsum = a0 + a1
    erow = lax.broadcasted_iota(I32, (N_EXPERTS, tm), 0)
    oh0 = (erow == e0).astype(F32)
    oh1 = (erow == e1).astype(F32)
    both = oh0 + oh1
    both_bf = both.astype(BF16)
    tri = (lax.broadcasted_iota(I32, (tm, tm), 0) < lax.broadcasted_iota(I32, (tm, tm), 1)).astype(BF16)
    before = _dot(both_bf, tri)
    round_up = lambda n: jnp.floor((n + (ROW_GROUP - 1.0)) * (1.0 / ROW_GROUP)) * ROW_GROUP
    run_col = round_up(jnp.sum(both, axis=1, keepdims=True))
    e_r = lax.broadcasted_iota(I32, (N_EXPERTS, N_EXPERTS), 0)
    e_c = lax.broadcasted_iota(I32, (N_EXPERTS, N_EXPERTS), 1)
    off_col = _dot((e_c < e_r).astype(BF16), jnp.broadcast_to(run_col, (N_EXPERTS, LANES)).astype(BF16))[:, 0:1]
    base = off_col + before
    s0 = jnp.sum(oh0 * base, axis=0, keepdims=True)
    s1 = jnp.sum(oh1 * base, axis=0, keepdims=True)
    both_pad = jnp.concatenate([both_bf, jnp.zeros((LANES - N_EXPERTS, tm), BF16)], axis=0)
    run_row = round_up(_dot_nt(jnp.ones((8, tm), BF16), both_pad))
    l_r = lax.broadcasted_iota(I32, (LANES, LANES), 0)
    l_c = lax.broadcasted_iota(I32, (LANES, LANES), 1)
    off_row = _dot(run_row.astype(BF16), (l_r < l_c).astype(BF16))
    lane = lax.broadcasted_iota(I32, (1, LANES), 1)
    groups = run_row[0:1] * (1.0 / ROW_GROUP)
    groups = jnp.where(lane == N_EXPERTS, jnp.sum(groups, axis=1, keepdims=True), groups)

    @pl.when(first)
    def _():
        cnt_ref[...] = jnp.zeros_like(cnt_ref)

    zero_half = jnp.zeros((1, tm - LANES), I32)
    ri_ref[0, 0:1, :] = s0.astype(I32)
    ri_ref[0, 1:2, :] = s1.astype(I32)
    ri_ref[0, 2:3, :] = e0
    ri_ref[0, 3:4, :] = e1
    for r, v in ((4, groups), (5, cnt_ref[0:1, :]), (6, off_row[0:1])):
        ri_ref[0, r:r + 1, 0:LANES] = v.astype(I32)
        ri_ref[0, r:r + 1, LANES:] = zero_half
    ri_ref[0, 7:8, :] = jnp.zeros((1, tm), I32)
    cnt_ref[...] = cnt_ref[...] + run_row
    rw_ref[0, 0:1, :] = a0 / wsum
    rw_ref[0, 1:2, :] = a1 / wsum
    rw_ref[0, 2:8, :] = jnp.zeros((6, tm), F32)


def _post_mixer(y, x, mod, lng, lnb, rwt, rb, x1_ref, ri_ref, rw_ref, cnt_ref, first, d):
    x1 = _layer_norm(ALPHA * x + mod[:, 2 * d:3 * d] * y, lng, lnb)
    x1_ref[0] = x1
    tok = x1 * (1.0 + mod[:, 4 * d:5 * d]) + mod[:, 3 * d:4 * d]
    t_hi = tok.astype(BF16)
    t_mid = (tok - t_hi.astype(F32)).astype(BF16)
    w_hi, w_mid = rwt[0], rwt[1]
    logits = _dot(t_hi, w_hi) + (_dot(t_hi, w_mid) + _dot(t_mid, w_hi))
    _route(logits.T[0:N_EXPERTS, :], rb, ri_ref, rw_ref, cnt_ref, first)


def _attn_out_kernel(ol_ref, oc_ref, xl_ref, xc_ref, mod_ref, wo_ref, lng_ref, lnb_ref, rwt_ref, rb_ref,
                     x1_ref, ri_ref, rw_ref, cnt_ref, *, d, n_lat):
    first = (pl.program_id(0) == 0) & (pl.program_id(1) == 0)
    latent = pl.program_id(1) < n_lat
    o = jnp.where(latent, ol_ref[0], oc_ref[0])
    x = jnp.where(latent, xl_ref[0], xc_ref[0])
    y = _dot(o, wo_ref[...])
    _post_mixer(y, x, mod_ref[0], lng_ref[...], lnb_ref[...], rwt_ref[...], rb_ref[...],
                x1_ref, ri_ref, rw_ref, cnt_ref, first, d)


def _route_out_shapes(b, n_tiles, t_out, d):
    return (jax.ShapeDtypeStruct((b, t_out, d), F32),
            jax.ShapeDtypeStruct((b * n_tiles, 8, TM), I32),
            jax.ShapeDtypeStruct((b * n_tiles, 8, TM), F32),
            jax.ShapeDtypeStruct((8, LANES), F32))


def _route_out_specs(n_tiles, d):
    return (pl.BlockSpec((1, TM, d), lambda bi, j: (bi, j, 0)),
            pl.BlockSpec((1, 8, TM), lambda bi, j: (bi * n_tiles + j, 0, 0)),
            pl.BlockSpec((1, 8, TM), lambda bi, j: (bi * n_tiles + j, 0, 0)),
            pl.BlockSpec((8, LANES), lambda bi, j: (0, 0)))


def _attn_out_call(o_lat, o_ctx, x, ctx, mod, wo, lng, lnb, rwt, rb, ctx_row):
    b, s_len, d = x.shape
    n_lat = s_len // TM
    nt = n_lat + 1
    row = lambda a: pl.BlockSpec(a.shape, lambda bi, j: (0,) * a.ndim)
    lat = pl.BlockSpec((1, TM, d), lambda bi, j: (bi, jnp.minimum(j, n_lat - 1), 0))
    cxt = pl.BlockSpec((1, TM, d), lambda bi, j: (bi, 0, 0))
    return pl.pallas_call(
        functools.partial(_attn_out_kernel, d=d, n_lat=n_lat),
        out_shape=_route_out_shapes(b, nt, nt * TM, d),
        grid=(b, nt),
        in_specs=[
            lat, cxt, lat, cxt,
            pl.BlockSpec((1, 1, 6 * d), lambda bi, j: (jnp.where(j < n_lat, bi, ctx_row), 0, 0)),
            row(wo), row(lng), row(lnb), row(rwt), row(rb),
        ],
        out_specs=_route_out_specs(nt, d),
        compiler_params=_cparams(("arbitrary", "arbitrary")),
        name="attn_out_route",
    )(o_lat, o_ctx, x, ctx, mod, wo, lng, lnb, rwt, rb)


def _group_copy(src, r_src, dst, r_dst, sem):
    return pltpu.make_async_copy(src.at[pl.ds(pl.multiple_of(r_src, ROW_GROUP), ROW_GROUP)],
                                 dst.at[pl.ds(pl.multiple_of(r_dst, ROW_GROUP), ROW_GROUP)], sem)


def _tab(tab_ref, tile, row, e):
    return tab_ref[(tile * TAB_ROWS + row) * TAB_LANES + e]


def _run_copies(tab_ref, pad_ref, tile, stage, hbm, sem, to_hbm):
    for e in range(N_EXPERTS):
        s0 = _tab(tab_ref, tile, 2, e)
        h0 = pad_ref[e] + _tab(tab_ref, tile, 1, e)

        def issue(g, carry):
            s, h = s0 + g * ROW_GROUP, h0 + g * ROW_GROUP
            (_group_copy(stage, s, hbm, h, sem) if to_hbm else _group_copy(hbm, h, stage, s, sem)).start()
            return carry

        lax.fori_loop(0, _tab(tab_ref, tile, 0, e), issue, 0)


def _run_waits(tab_ref, tile, stage, hbm, sem, to_hbm):
    def drain(g, carry):
        (_group_copy(stage, 0, hbm, 0, sem) if to_hbm else _group_copy(hbm, 0, stage, 0, sem)).wait()
        return carry

    lax.fori_loop(0, _tab(tab_ref, tile, 0, N_EXPERTS), drain, 0)


def _dispatch_kernel(tab_ref, pad_ref, fill_ref, x_ref, mod_ref, ri_ref, xs_ref, stage_ref, zero_ref, sems, sem_z,
                     *, d, n_steps):
    step = pl.program_id(0) * pl.num_programs(1) + pl.program_id(1)
    buf = step % 2
    mod = mod_ref[0]
    tok = (x_ref[0] * (1.0 + mod[:, 4 * d:5 * d]) + mod[:, 3 * d:4 * d]).astype(BF16)
    slot = lax.broadcasted_iota(I32, (STAGE_ROWS, TM), 0)
    ri = ri_ref[0]
    pick = jnp.where((slot == ri[0:1, :]) | (slot == ri[1:2, :]), 1.0, 0.0).astype(BF16)
    for bsel in range(2):
        @pl.when(buf == bsel)
        def _():
            stage_ref[bsel] = _dot(pick, tok).astype(BF16)
            _run_copies(tab_ref, pad_ref, step, stage_ref.at[bsel], xs_ref, sems.at[bsel], True)

    for bsel in range(2):
        @pl.when((buf != bsel) & (step > 0))
        def _():
            _run_waits(tab_ref, step - 1, stage_ref.at[bsel], xs_ref, sems.at[bsel], True)

    @pl.when(step == n_steps - 1)
    def _():
        for bsel in range(2):
            @pl.when(buf == bsel)
            def _():
                _run_waits(tab_ref, step, stage_ref.at[bsel], xs_ref, sems.at[bsel], True)

        zero_ref[...] = jnp.zeros_like(zero_ref)
        for e in range(N_EXPERTS + 1):
            lo, hi = fill_ref[2 * e], fill_ref[2 * e + 1]

            def fill(g, carry):
                _group_copy(zero_ref, 0, xs_ref, g * ROW_GROUP, sem_z).start()
                return carry

            def fill_wait(g, carry):
                _group_copy(zero_ref, 0, xs_ref, 0, sem_z).wait()
                return carry

            lax.fori_loop(lo, hi, fill, 0)
            lax.fori_loop(lo, hi, fill_wait, 0)


def _dispatch_call(tab, pad_start, fill, x1, mod, ri, n_tiles, n_pad, ctx_row, n_lat):
    b, t, d = x1.shape
    grid_spec = pltpu.PrefetchScalarGridSpec(
        num_scalar_prefetch=3,
        grid=(b, n_tiles),
        in_specs=[
            pl.BlockSpec((1, TM, d), lambda bi, j, *_: (bi, j, 0)),
            pl.BlockSpec((1, 1, 6 * d), lambda bi, j, *_: (jnp.where(j < n_lat, bi, ctx_row), 0, 0)),
            pl.BlockSpec((1, 8, TM), lambda bi, j, *_: (bi * n_tiles + j, 0, 0)),
        ],
        out_specs=pl.BlockSpec(memory_space=pl.ANY),
        scratch_shapes=[pltpu.VMEM((2, STAGE_ROWS, d), BF16), pltpu.VMEM((ROW_GROUP, d), BF16),
                        pltpu.SemaphoreType.DMA((2,)), pltpu.SemaphoreType.DMA],
    )
    return pl.pallas_call(
        functools.partial(_dispatch_kernel, d=d, n_steps=b * n_tiles),
        out_shape=jax.ShapeDtypeStruct((n_pad, d), BF16),
        grid_spec=grid_spec,
        compiler_params=_cparams(("arbitrary", "arbitrary")),
        name="moe_dispatch",
    )(tab, pad_start, fill, x1, mod, ri)


def _ffn_kernel(be_ref, nu_ref, xs_ref, wg_ref, wu_ref, wd_ref, ys_ref, wg_s, wu_s, wd_s):
    i = pl.program_id(0)

    @pl.when(i < nu_ref[0])
    def _():
        prev = be_ref[jnp.maximum(i - 1, 0)]

        @pl.when((i == 0) | (be_ref[i] != prev))
        def _():
            wg_s[...] = wg_ref[0, 0].astype(BF16)
            wu_s[...] = wu_ref[0, 0].astype(BF16)
            wd_s[...] = wd_ref[0, 0].astype(BF16)

        rows = [slice(r, r + MOE_BLK // FFN_SLICES) for r in range(0, MOE_BLK, MOE_BLK // FFN_SLICES)]
        gate_up = [(_dot(xs_ref[r, :], wg_s[...]), _dot(xs_ref[r, :], wu_s[...])) for r in rows]
        for r, (g, u) in zip(rows, gate_up):
            ys_ref[r, :] = _dot((_silu(g) * u).astype(BF16), wd_s[...]).astype(BF16)

    @pl.when(i >= nu_ref[0])
    def _():
        ys_ref[...] = jnp.zeros_like(ys_ref)


def _ffn_call(block_expert, n_used, xs, w_gate, w_up, w_down, layer):
    n_pad, d = xs.shape
    de = w_gate.shape[-1]
    n_blocks = n_pad // MOE_BLK
    blk = lambda i, be, nu: (jnp.minimum(i, nu[0] - 1), 0)
    exp = lambda i, be, nu: (layer, be[jnp.minimum(i, nu[0] - 1)], 0, 0)
    grid_spec = pltpu.PrefetchScalarGridSpec(
        num_scalar_prefetch=2,
        grid=(n_blocks,),
        in_specs=[
            pl.BlockSpec((MOE_BLK, d), blk),
            pl.BlockSpec((1, 1, d, de), exp),
            pl.BlockSpec((1, 1, d, de), exp),
            pl.BlockSpec((1, 1, de, d), exp),
        ],
        out_specs=pl.BlockSpec((MOE_BLK, d), lambda i, be, nu: (i, 0)),
        scratch_shapes=[pltpu.VMEM((d, de), BF16), pltpu.VMEM((d, de), BF16), pltpu.VMEM((de, d), BF16)],
    )
    return pl.pallas_call(
        _ffn_kernel,
        out_shape=jax.ShapeDtypeStruct((n_pad, d), BF16),
        grid_spec=grid_spec,
        compiler_params=_cparams(("arbitrary",)),
        name="moe_ffn",
    )(block_expert, n_used, xs, w_gate, w_up, w_down)


def _combine_kernel(tab_ref, pad_ref, x_ref, mod_ref, rit_ref, rw_ref, lng_ref, lnb_ref, ys_ref, o_ref,
                    stage_ref, sems, *, d, n_steps):
    step = pl.program_id(0) * pl.num_programs(1) + pl.program_id(1)
    buf = step % 2

    @pl.when(step == 0)
    def _():
        stage_ref[...] = jnp.zeros_like(stage_ref)
        _run_copies(tab_ref, pad_ref, step, stage_ref.at[0], ys_ref, sems.at[0], False)

    for bsel in range(2):
        @pl.when((buf != bsel) & (step + 1 < n_steps))
        def _():
            _run_copies(tab_ref, pad_ref, step + 1, stage_ref.at[bsel], ys_ref, sems.at[bsel], False)

    rit = rit_ref[0]
    lane = lax.broadcasted_iota(I32, (TM, STAGE_ROWS), 1)
    pick0 = jnp.where(lane == rit[:, 0:1], 1.0, 0.0).astype(BF16)
    pick1 = jnp.where(lane == rit[:, 1:2], 1.0, 0.0).astype(BF16)
    w = rw_ref[0]
    mod = mod_ref[0]
    for bsel in range(2):
        @pl.when(buf == bsel)
        def _():
            _run_waits(tab_ref, step, stage_ref.at[bsel], ys_ref, sems.at[bsel], False)
            stage = stage_ref[bsel]
            moe = w[:, 0:1] * _dot(pick0, stage) + w[:, 1:2] * _dot(pick1, stage)
            o_ref[0] = _layer_norm(ALPHA * x_ref[0] + mod[:, 5 * d:6 * d] * moe, lng_ref[...], lnb_ref[...])


def _combine_call(tab, pad_start, x1, mod, ri_t, rw_t, lng, lnb, ys, n_tiles, ctx_row, n_lat):
    b, _, d = x1.shape
    row = lambda a: pl.BlockSpec(a.shape, lambda bi, j, *_: (0,) * a.ndim)
    grid_spec = pltpu.PrefetchScalarGridSpec(
        num_scalar_prefetch=2,
        grid=(b, n_tiles),
        in_specs=[
            pl.BlockSpec((1, TM, d), lambda bi, j, *_: (bi, j, 0)),
            pl.BlockSpec((1, 1, 6 * d), lambda bi, j, *_: (jnp.where(j < n_lat, bi, ctx_row), 0, 0)),
            pl.BlockSpec((1, TM, 8), lambda bi, j, *_: (bi * n_tiles + j, 0, 0)),
            pl.BlockSpec((1, TM, 8), lambda bi, j, *_: (bi * n_tiles + j, 0, 0)),
            row(lng), row(lnb),
            pl.BlockSpec(memory_space=pl.ANY),
        ],
        out_specs=pl.BlockSpec((1, TM, d), lambda bi, j, *_: (bi, j, 0)),
        scratch_shapes=[pltpu.VMEM((2, STAGE_ROWS, d), BF16), pltpu.SemaphoreType.DMA((2,))],
    )
    return pl.pallas_call(
        functools.partial(_combine_kernel, d=d, n_steps=b * n_tiles),
        out_shape=jax.ShapeDtypeStruct((b, n_tiles * TM, d), F32),
        grid_spec=grid_spec,
        compiler_params=_cparams(("arbitrary", "arbitrary")),
        name="moe_combine",
    )(tab, pad_start, x1, mod, ri_t, rw_t, lng, lnb, ys)


def _moe(x1, mod, ri, rw, counts, lng, lnb, w_gate, w_up, w_down, layer, n_tiles, ctx_row, n_lat):
    b, _, d = x1.shape
    n_steps = b * n_tiles
    n_rows_max = 2 * n_steps * TM + (ROW_GROUP - 1) * N_EXPERTS * n_steps
    n_pad = -(-n_rows_max // MOE_BLK) * MOE_BLK + N_EXPERTS * MOE_BLK
    n_blocks = n_pad // MOE_BLK
    cnt = counts[0, :N_EXPERTS].astype(I32)
    padded = (cnt + MOE_BLK - 1) // MOE_BLK * MOE_BLK
    pad_end = jnp.cumsum(padded)
    pad_start = jnp.concatenate([jnp.zeros((1,), I32), pad_end]).astype(I32)
    block_row0 = jnp.arange(n_blocks, dtype=I32) * MOE_BLK
    block_expert = jnp.minimum(
        jnp.sum((pad_end[None, :] <= block_row0[:, None]).astype(I32), axis=1), N_EXPERTS - 1).astype(I32)
    n_used = (pad_end[-1:] // MOE_BLK).astype(I32)
    fill_lo = jnp.concatenate([pad_start[:-1] + cnt, pad_end[-1:]]) // ROW_GROUP
    fill_hi = jnp.concatenate([pad_end, jnp.full((1,), n_pad, I32)]) // ROW_GROUP
    fill = jnp.stack([fill_lo, fill_hi], axis=1).reshape(-1).astype(I32)
    tab = ri[:, 4:4 + TAB_ROWS, :TAB_LANES].reshape(-1)
    xs = _dispatch_call(tab, pad_start, fill, x1, mod, ri, n_tiles, n_pad, ctx_row, n_lat)
    ys = _ffn_call(block_expert, n_used, xs, w_gate, w_up, w_down, layer)
    ri_t, rw_t = jnp.swapaxes(ri, 1, 2), jnp.swapaxes(rw, 1, 2)
    return _combine_call(tab, pad_start, x1, mod, ri_t, rw_t, lng, lnb, ys, n_tiles, ctx_row, n_lat)


def _gla_proj_kernel(x_ref, mod_ref, win_ref, wgd_ref, wgu_ref, bg_ref, q_ref, k_ref, v_ref, r_ref, g_ref, *, d):
    mod = mod_ref[0]
    h = (x_ref[0] * (1.0 + mod[:, d:2 * d]) + mod[:, 0:d]).astype(BF16)
    dk = GLA_HEADS * GLA_DK
    p = _dot(h, win_ref[...])
    q_ref[0] = (p[:, 0:dk] * GLA_DK ** -0.5).astype(BF16)
    k_ref[0] = p[:, dk:2 * dk].astype(BF16)
    v_ref[0] = p[:, 2 * dk:2 * dk + d].astype(BF16)
    r_ref[0] = p[:, 2 * dk + d:].astype(BF16)
    low = _dot(h, wgd_ref[...]).astype(BF16)
    pre = _dot(low, wgu_ref[...]) + bg_ref[...]
    g_ref[0] = (jnp.minimum(pre, 0.0) - jnp.log1p(jnp.exp(-jnp.abs(pre)))) * (1.0 / GLA_TAU)


def _gla_proj_call(xs, mod, win, wgd, wgu, bg, ctx_row):
    b, t, d = xs.shape
    nt = t // TM
    n_lat = nt - 1
    dk = GLA_HEADS * GLA_DK
    row = lambda a: pl.BlockSpec(a.shape, lambda bi, j: (0,) * a.ndim)
    tile = lambda n: pl.BlockSpec((1, TM, n), lambda bi, j: (bi, j, 0))
    return pl.pallas_call(
        functools.partial(_gla_proj_kernel, d=d),
        out_shape=(jax.ShapeDtypeStruct((b, t, dk), BF16), jax.ShapeDtypeStruct((b, t, dk), BF16),
                   jax.ShapeDtypeStruct((b, t, d), BF16), jax.ShapeDtypeStruct((b, t, d), BF16),
                   jax.ShapeDtypeStruct((b, t, 2 * dk), F32)),
        grid=(b, nt),
        in_specs=[tile(d), pl.BlockSpec((1, 1, 6 * d), lambda bi, j: (jnp.where(j < n_lat, bi, ctx_row), 0, 0)),
                  row(win), row(wgd), row(wgu), row(bg)],
        out_specs=(tile(dk), tile(dk), tile(d), tile(d), tile(2 * dk)),
        compiler_params=_cparams(("parallel", "parallel")),
        name="gla_proj",
    )(xs, mod, win, wgd, wgu, bg)


def _split3(x):
    hi = x.astype(BF16)
    r1 = x - hi.astype(F32)
    mid = r1.astype(BF16)
    lo = (r1 - mid.astype(F32)).astype(BF16)
    return hi, mid, lo


class _GlaMasks:
    def __init__(self, reverse):
        L = GLA_CHUNK
        t = lax.broadcasted_iota(I32, (L, L), 0)
        s = lax.broadcasted_iota(I32, (L, L), 1)
        seen = (s >= t) if reverse else (s <= t)
        self.tri = seen.astype(BF16)
        blk = lambda r: jnp.right_shift(r, GLA_SUB.bit_length() - 1)
        dist = (blk(s) - blk(t)) if reverse else (blk(t) - blk(s))
        self.pairs = [seen & (dist == 0)] + [dist == n for n in range(1, L // GLA_SUB)]


def _gla_chunk_matmuls(q_ref, k_ref, v_ref, b_all, rows, h, reverse):
    L, SB, DK = GLA_CHUNK, GLA_SUB, GLA_DK
    nb = L // SB
    kcols = slice(h * DK, (h + 1) * DK)
    vcols = slice(h * GLA_DV, (h + 1) * GLA_DV)
    bh = b_all[:, kcols]
    zero = jnp.zeros((1, DK), F32)
    earlier = lambda i, n: i + n if reverse else i - n

    def start_ref(i):
        if not 0 <= i < nb or i == (nb - 1 if reverse else 0):
            return zero
        r = (i + 1) * SB if reverse else i * SB - 1
        return bh[r:r + 1, :]

    def end_ref(i):
        r = i * SB if reverse else i * SB + SB - 1
        return bh[r:r + 1, :]

    per_block = lambda f: jnp.concatenate([jnp.broadcast_to(f(i), (SB, DK)) for i in range(nb)], axis=0)
    bs, be = per_block(start_ref), per_block(end_ref)
    b_last = end_ref(0 if reverse else nb - 1)
    q = q_ref[0, rows, kcols].astype(F32)
    k = k_ref[0, rows, kcols].astype(F32)
    v = v_ref[0, rows, vcols]
    scaled = lambda x, e: (x * jnp.exp(e)).astype(BF16)
    q_own = scaled(q, bh - bs)
    k_own, k_end = scaled(k, bs - bh), scaled(k, be - bh)
    pieces = [_dot_nt(q_own, k_own), _dot_nt(q_own, k_end)]
    for n in range(2, nb):
        pieces.append(_dot_nt(scaled(q, bh - per_block(lambda i: start_ref(earlier(i, n - 1)))), k_end))
    update = _dot_tn(v, scaled(k, b_last - bh))
    return pieces, scaled(q, bh), update, b_last, v, vcols


def _gla_scan_kernel(qf, kf, vf, gf, qb, kb, vb, gb, of_ref, ob_ref, sf_ref, sb_ref):
    @pl.when(pl.program_id(1) == 0)
    def _():
        sf_ref[...] = jnp.zeros_like(sf_ref)
        sb_ref[...] = jnp.zeros_like(sb_ref)

    L = GLA_CHUNK
    n_chunks = TM // L
    dirs = ((qf, kf, vf, gf, of_ref, sf_ref, False, _GlaMasks(False)),
            (qb, kb, vb, gb, ob_ref, sb_ref, True, _GlaMasks(True)))
    gate_sums = [[sum(_dot(masks.tri, part) for part in _split3(g_ref[0, c * L:(c + 1) * L, :]))
                  for c in range(n_chunks)] for _, _, _, g_ref, _, _, _, masks in dirs]
    work = []
    for step in range(n_chunks):
        for d_i, (q_ref, k_ref, v_ref, g_ref, o_ref, state_ref, reverse, masks) in enumerate(dirs):
            c = n_chunks - 1 - step if reverse else step
            rows = slice(c * L, (c + 1) * L)
            for h in range(GLA_HEADS):
                work.append((o_ref, state_ref, rows, h, masks,
                             _gla_chunk_matmuls(q_ref, k_ref, v_ref, gate_sums[d_i][c], rows, h, reverse)))
    for o_ref, state_ref, rows, h, masks, (pieces, q_int, update, b_last, v, vcols) in work:
        a = sum(jnp.where(m, p, 0.0) for m, p in zip(masks.pairs, pieces)).astype(BF16)
        state_t = state_ref[h]
        o_ref[0, rows, vcols] = _dot(a, v) + _dot_nt(q_int, state_t.astype(BF16))
        state_ref[h] = jnp.exp(b_last) * state_t + update


def _gla_scan_call(q, k, v, g):
    b, t, d = v.shape
    nt = t // TM
    n_lat = nt - 1
    dk = GLA_HEADS * GLA_DK
    fwd = lambda j: jnp.where(j == 0, n_lat, j - 1)
    bwd = lambda j: jnp.where(j == 0, n_lat, n_lat - j)
    spec = lambda n, idx, col: pl.BlockSpec((1, TM, n), lambda bi, j: (bi, idx(j), col))
    return pl.pallas_call(
        _gla_scan_kernel,
        out_shape=(jax.ShapeDtypeStruct((b, n_lat * TM, d), F32), jax.ShapeDtypeStruct((b, n_lat * TM, d), F32)),
        grid=(b, nt),
        in_specs=[spec(dk, fwd, 0), spec(dk, fwd, 0), spec(d, fwd, 0), spec(dk, fwd, 0),
                  spec(dk, bwd, 0), spec(dk, bwd, 0), spec(d, bwd, 0), spec(dk, bwd, 1)],
        out_specs=(pl.BlockSpec((1, TM, d), lambda bi, j: (bi, jnp.maximum(j - 1, 0), 0)),
                   pl.BlockSpec((1, TM, d), lambda bi, j: (bi, n_lat - jnp.maximum(j, 1), 0))),
        scratch_shapes=[pltpu.VMEM((GLA_HEADS, GLA_DV, GLA_DK), F32), pltpu.VMEM((GLA_HEADS, GLA_DV, GLA_DK), F32)],
        compiler_params=_cparams(("parallel", "arbitrary")),
        name="gla_scan",
    )(q, k, v, g, q, k, v, g)


def _gla_out_kernel(of_ref, ob_ref, r_ref, x_ref, mod_ref, ng_ref, wo_ref, lng_ref, lnb_ref, rwt_ref, rb_ref,
                    x1_ref, ri_ref, rw_ref, cnt_ref, *, d):
    first = (pl.program_id(0) == 0) & (pl.program_id(1) == 0)
    o = of_ref[0] + ob_ref[0]
    r = r_ref[0].astype(F32)
    gate = _silu(r)
    ng = ng_ref[...]
    parts = []
    for h in range(GLA_HEADS):
        cols = slice(h * GLA_DV, (h + 1) * GLA_DV)
        oh = o[:, cols]
        ms = jnp.mean(oh * oh, axis=-1, keepdims=True)
        parts.append((oh * lax.rsqrt(ms + LN_EPS) * ng * gate[:, cols]).astype(BF16))
    y = _dot(jnp.concatenate(parts, axis=-1), wo_ref[...])
    _post_mixer(y, x_ref[0], mod_ref[0], lng_ref[...], lnb_ref[...], rwt_ref[...], rb_ref[...],
                x1_ref, ri_ref, rw_ref, cnt_ref, first, d)


def _gla_out_call(o_f, o_b, r, xs, mod, ng, wo, lng, lnb, rwt, rb):
    b, s_len, d = o_f.shape
    n_lat = s_len // TM
    row = lambda a: pl.BlockSpec(a.shape, lambda bi, j: (0,) * a.ndim)
    tile = pl.BlockSpec((1, TM, d), lambda bi, j: (bi, j, 0))
    return pl.pallas_call(
        functools.partial(_gla_out_kernel, d=d),
        out_shape=_route_out_shapes(b, n_lat, s_len, d),
        grid=(b, n_lat),
        in_specs=[tile, tile, tile, tile, pl.BlockSpec((1, 1, 6 * d), lambda bi, j: (bi, 0, 0)),
                  row(ng), row(wo), row(lng), row(lnb), row(rwt), row(rb)],
        out_specs=_route_out_specs(n_lat, d),
        compiler_params=_cparams(("arbitrary", "arbitrary")),
        name="gla_out_route",
    )(o_f, o_b, r, xs, mod, ng, wo, lng, lnb, rwt, rb)


def kernel(x, c, ctx, c_ctx, ada_w, ada_b, ln_g, ln_b, attn_w_qkv, attn_w_o, attn_lambda, attn_subln_g, gla_w_in, gla_wg_down, gla_wg_up, gla_bg, gla_norm_g, gla_w_o, router_w, router_b, moe_w_gate, moe_w_up, moe_w_down):
    b, s_len, d = x.shape
    c_len = ctx.shape[1]
    assert ada_w.shape[0] == DEPTH and c_len == TM and s_len % TQ == 0 and TQ % TM == 0 and s_len % GRID_W == 0
    assert d == DIFF_HEADS * 2 * DIFF_HEAD_DIM == GLA_HEADS * GLA_DV
    assert STAGE_ROWS % LANES == 0 and STAGE_ROWS >= 2 * TM + (ROW_GROUP - 1) * N_EXPERTS
    n_lat = s_len // TM
    nt = n_lat + 1
    ctx_row = b

    cond_rows = -(-(b + 1) // 8) * 8
    cond = jnp.concatenate([c, c_ctx[None, :], jnp.zeros((cond_rows - b - 1, d), F32)], axis=0)
    mods = _ada_call(cond, ada_w, ada_b)
    rw_pad = jnp.concatenate([router_w, jnp.zeros((d, LANES - N_EXPERTS), F32)], axis=1)
    rw_hi = rw_pad.astype(BF16)
    rwt = jnp.stack([rw_hi, (rw_pad - rw_hi.astype(F32)).astype(BF16)])
    rb = router_b.reshape(N_EXPERTS, 1)

    mod0 = mods[0].reshape(cond_rows, 1, 6 * d)
    wqkv = attn_w_qkv[0]
    wqk = wqkv[:, :2 * d].astype(BF16)
    wvt = wqkv[:, 2 * d:].T.astype(BF16)
    tables = tuple(jnp.asarray(t) for t in _rope_tables(s_len, c_len))
    q, k, vt = _qkv_call(x, ctx, mod0, wqk, wvt, tables, ctx_row)
    lambda_init = 0.8 - 0.6 * math.exp(-0.3 * 0)
    attn = functools.partial(_attn_call, q, k, vt, attn_lambda[0], attn_subln_g[0], lambda_init)
    o_lat = attn(tq=TQ, q_row0=0, n_rows=s_len, chunk0=0, n_chunks=nt, name="diff_attn_lat")
    o_ctx = attn(tq=c_len, q_row0=s_len, n_rows=c_len, chunk0=n_lat, n_chunks=1, name="diff_attn_ctx")
    x1, ri, rw, counts = _attn_out_call(o_lat, o_ctx, x, ctx, mod0, attn_w_o[0].astype(BF16),
                                        ln_g[0, 0][None], ln_b[0, 0][None], rwt, rb, ctx_row)
    stream = _moe(x1, mod0, ri, rw, counts, ln_g[0, 1][None], ln_b[0, 1][None],
                  moe_w_gate, moe_w_up, moe_w_down, 0, nt, ctx_row, n_lat)

    mod1 = mods[1].reshape(cond_rows, 1, 6 * d)
    dk = GLA_HEADS * GLA_DK
    wgd = jnp.concatenate([gla_wg_down[0, 0], gla_wg_down[0, 1],
                           jnp.zeros((d, LANES - 2 * GLA_GATE_RANK), F32)], axis=1).astype(BF16)
    wgu = jnp.zeros((LANES, 2 * dk), F32)
    wgu = wgu.at[0:GLA_GATE_RANK, 0:dk].set(gla_wg_up[0, 0])
    wgu = wgu.at[GLA_GATE_RANK:2 * GLA_GATE_RANK, dk:].set(gla_wg_up[0, 1]).astype(BF16)
    bg = gla_bg[0].reshape(1, 2 * dk)
    gq, gk, gv, gr, gg = _gla_proj_call(stream, mod1, gla_w_in[0].astype(BF16), wgd, wgu, bg, ctx_row)
    o_f, o_b = _gla_scan_call(gq, gk, gv, gg)
    x1, ri, rw, counts = _gla_out_call(o_f, o_b, gr, stream, mod1, gla_norm_g[0][None], gla_w_o[0].astype(BF16),
                                       ln_g[1, 0][None], ln_b[1, 0][None], rwt, rb)
    return _moe(x1, mod1, ri, rw, counts, ln_g[1, 1][None], ln_b[1, 1][None],
                moe_w_gate, moe_w_up, moe_w_down, 1, n_lat, ctx_row, n_lat)
```

```python
import functools
import math

import numpy as np
import jax
import jax.numpy as jnp
from jax import lax
from jax.experimental import pallas as pl
from jax.experimental.pallas import tpu as pltpu

F32 = jnp.float32
BF16 = jnp.bfloat16
I32 = jnp.int32

DEPTH = 2
ALPHA = (2.0 * DEPTH) ** 0.25
LN_EPS = 1e-5
GRID_W = 64
ROPE_BASE = 10000.0
DIFF_HEADS = 8
DIFF_HEAD_DIM = 64
ROT_FREQS = DIFF_HEAD_DIM // 4
GLA_HEADS = 4
GLA_DK = 128
GLA_DV = 256
GLA_GATE_RANK = 16
GLA_TAU = 16.0
GLA_CHUNK = 64
GLA_SUB = 16
N_EXPERTS = 16
N_GROUPS = 4
EXPERTS_PER_GROUP = 4
LOG2E = 1.4426950408889634

LANES = 128
TM = 256
TK = 256
TQ = 512
ATTN_GROUP = 4
MOE_BLK = 512
FFN_SLICES = 2
ADA_TN = 1536
ROW_GROUP = 16
STAGE_ROWS = 768
GROUP_TAB = 64
VMEM_LIMIT = 48 * 1024 * 1024
NEG_BIG = -1e30
SOFTMAX_HEADROOM = 64.0
BOUND_SLACK = 1.001
SOFTMAX_MIN_SUM = 2.0 ** -40


def _cparams(sem):
    return pltpu.CompilerParams(dimension_semantics=sem, vmem_limit_bytes=VMEM_LIMIT)


def _silu(x):
    return x * (1.0 / (1.0 + jnp.exp(-x)))


def _layer_norm(z, g, b):
    mu = jnp.mean(z, axis=-1, keepdims=True)
    zc = z - mu
    var = jnp.mean(zc * zc, axis=-1, keepdims=True)
    return zc * lax.rsqrt(var + LN_EPS) * g + b


def _dot(a, b):
    return jnp.dot(a, b, preferred_element_type=F32)


def _dot_nt(a, b, precision=None):
    return lax.dot_general(a, b, (((1,), (1,)), ((), ())), preferred_element_type=F32, precision=precision)


def _dot_tn(a, b):
    return lax.dot_general(a, b, (((0,), (0,)), ((), ())), preferred_element_type=F32)


def _ada_kernel(c_ref, w_ref, b_ref, o_ref):
    s = _silu(c_ref[...])
    o_ref[0] = jnp.dot(s, w_ref[0], preferred_element_type=F32, precision=lax.Precision.HIGHEST) + b_ref[0]


def _ada_call(cond, ada_w, ada_b):
    depth, d, n = ada_w.shape
    rows = cond.shape[0]
    return pl.pallas_call(
        _ada_kernel,
        out_shape=jax.ShapeDtypeStruct((depth, rows, n), F32),
        grid=(depth, n // ADA_TN),
        in_specs=[
            pl.BlockSpec((rows, d), lambda i, j: (0, 0)),
            pl.BlockSpec((1, d, ADA_TN), lambda i, j: (i, 0, j)),
            pl.BlockSpec((1, 1, ADA_TN), lambda i, j: (i, 0, j)),
        ],
        out_specs=pl.BlockSpec((1, rows, ADA_TN), lambda i, j: (i, 0, j)),
        compiler_params=_cparams(("parallel", "parallel")),
        name="ada_mod",
    )(cond, ada_w, ada_b.reshape(depth, 1, n))


def _qkv_kernel(xl_ref, xc_ref, mod_ref, wqk_ref, wvt_ref, cos_ref, sa_ref, sb_ref, q_ref, k_ref, vt_ref, *, d, n_lat):
    mod = mod_ref[0]
    x = jnp.where(pl.program_id(1) < n_lat, xl_ref[0], xc_ref[0])
    h = (x * (1.0 + mod[:, d:2 * d]) + mod[:, 0:d]).astype(BF16)
    qk = _dot(h, wqk_ref[...])
    vt_ref[0, 0] = _dot_nt(wvt_ref[...], h).astype(BF16)
    cos, sa, sb = cos_ref[...], sa_ref[...], sb_ref[...]
    q_scale = DIFF_HEAD_DIM ** -0.5 * LOG2E
    for s in range(2 * d // LANES):
        z = qk[:, s * LANES:(s + 1) * LANES]
        zr = z * cos + pltpu.roll(z, LANES - ROT_FREQS, axis=1) * sa + pltpu.roll(z, ROT_FREQS, axis=1) * sb
        if s < d // LANES:
            q_ref[0, :, s * LANES:(s + 1) * LANES] = (zr * q_scale).astype(BF16)
        else:
            k_ref[0, :, (s * LANES - d):(s * LANES - d + LANES)] = zr.astype(BF16)


def _rope_tables(s_len, c_len):
    rows = s_len // GRID_W
    row = np.repeat(np.arange(rows, dtype=np.float32), GRID_W)
    col = np.tile(np.arange(GRID_W, dtype=np.float32), rows)
    inv_freq = np.float32(ROPE_BASE) ** (-np.arange(ROT_FREQS, dtype=np.float32) / np.float32(ROT_FREQS))
    ang_r = row[:, None] * inv_freq
    ang_c = col[:, None] * inv_freq
    ang = np.concatenate([ang_r, ang_r, ang_c, ang_c], axis=-1).astype(np.float64)
    cos, sin = np.cos(ang), np.sin(ang)
    first_half = (np.arange(DIFF_HEAD_DIM) // ROT_FREQS) % 2 == 0
    sa = np.where(first_half[None, :], -sin, 0.0)
    sb = np.where(first_half[None, :], 0.0, sin)
    pad = lambda t, fill: np.concatenate([t, np.full((c_len, DIFF_HEAD_DIM), fill)], axis=0)
    two = lambda t: np.concatenate([t, t], axis=-1).astype(np.float32)
    return two(pad(cos, 1.0)), two(pad(sa, 0.0)), two(pad(sb, 0.0))


def _qkv_call(x, ctx, mod, wqk, wvt, tables, ctx_row):
    b, s_len, d = x.shape
    n_lat = s_len // TM
    nt = n_lat + 1
    t = nt * TM
    mod_idx = lambda bi, j: (jnp.where(j < n_lat, bi, ctx_row), 0, 0)
    return pl.pallas_call(
        functools.partial(_qkv_kernel, d=d, n_lat=n_lat),
        out_shape=(jax.ShapeDtypeStruct((b, t, d), BF16), jax.ShapeDtypeStruct((b, t, d), BF16),
                   jax.ShapeDtypeStruct((b, nt, d, TM), BF16)),
        grid=(b, nt),
        in_specs=[
            pl.BlockSpec((1, TM, d), lambda bi, j: (bi, jnp.minimum(j, n_lat - 1), 0)),
            pl.BlockSpec((1, TM, d), lambda bi, j: (bi, 0, 0)),
            pl.BlockSpec((1, 1, 6 * d), mod_idx),
            pl.BlockSpec((d, 2 * d), lambda bi, j: (0, 0)),
            pl.BlockSpec((d, d), lambda bi, j: (0, 0)),
            pl.BlockSpec((TM, LANES), lambda bi, j: (j, 0)),
            pl.BlockSpec((TM, LANES), lambda bi, j: (j, 0)),
            pl.BlockSpec((TM, LANES), lambda bi, j: (j, 0)),
        ],
        out_specs=(pl.BlockSpec((1, TM, d), lambda bi, j: (bi, j, 0)),
                   pl.BlockSpec((1, TM, d), lambda bi, j: (bi, j, 0)),
                   pl.BlockSpec((1, 1, d, TM), lambda bi, j: (bi, j, 0, 0))),
        compiler_params=_cparams(("parallel", "parallel")),
        name="qkv_rope",
    )(x, ctx, mod, wqk, wvt, *tables)


def _attn_kernel(q_ref, k_ref, vt_ref, lam_ref, g_ref, o_ref, kn_ref, *, chunks, lambda_init, tq):
    q = q_ref[0]
    lane = lax.broadcasted_iota(I32, q.shape, 1)
    zero = jnp.zeros_like(q)
    qq = jnp.concatenate([jnp.where(lane < DIFF_HEAD_DIM, q, zero),
                          jnp.where(lane >= DIFF_HEAD_DIM, q, zero)], axis=0)
    k_chunk = lambda c: k_ref[0, c * TK:(c + 1) * TK, :]

    def finish(acc, l):
        acc = acc / l
        lv = lam_ref[...]
        lam = (jnp.exp(jnp.sum(lv[0:1] * lv[1:2], axis=1, keepdims=True))
               - jnp.exp(jnp.sum(lv[2:3] * lv[3:4], axis=1, keepdims=True)) + lambda_init)
        ot = acc[:, :tq] - lam * acc[:, tq:]
        ms = jnp.mean(ot * ot, axis=0, keepdims=True)
        ot = ot * lax.rsqrt(ms + LN_EPS) * g_ref[...] * (1.0 - lambda_init)
        o_ref[0] = ot.T.astype(BF16)

    @pl.when(pl.program_id(2) == 0)
    def _():
        klane = lax.broadcasted_iota(I32, (TK, LANES), 1)
        best1 = jnp.zeros((1, 1), F32)
        best2 = jnp.zeros((1, 1), F32)
        for c in chunks:
            kf = k_chunk(c).astype(F32)
            ksq = kf * kf
            n1 = jnp.sum(jnp.where(klane < DIFF_HEAD_DIM, ksq, 0.0), axis=1, keepdims=True)
            n2 = jnp.sum(jnp.where(klane >= DIFF_HEAD_DIM, ksq, 0.0), axis=1, keepdims=True)
            best1 = jnp.maximum(best1, jnp.max(n1, axis=0, keepdims=True))
            best2 = jnp.maximum(best2, jnp.max(n2, axis=0, keepdims=True))
        kn_ref[...] = jnp.where(lax.broadcasted_iota(I32, kn_ref.shape, 1) < DIFF_HEAD_DIM, best1, best2)

    kn = kn_ref[0:1, :]
    qf = qq.astype(F32)
    qsq = jnp.sum(qf * qf, axis=1, keepdims=True)
    half = lax.broadcasted_iota(I32, (2 * tq, 1), 0) < tq
    bound = jnp.sqrt(qsq * jnp.where(half, kn[:, 0:1], kn[:, DIFF_HEAD_DIM:DIFF_HEAD_DIM + 1])) * BOUND_SLACK
    shift = SOFTMAX_HEADROOM - bound
    lane2 = lax.broadcasted_iota(I32, (2 * tq, LANES), 1)
    q_aug = jnp.concatenate([qq, jnp.where(lane2 == 0, shift, 0.0).astype(BF16)], axis=1)
    groups = [chunks[i:i + ATTN_GROUP] for i in range(0, len(chunks), ATTN_GROUP)]

    def scores(grp):
        kg = k_ref[0, grp[0] * TK:(grp[-1] + 1) * TK, :]
        one_lane = jnp.where(lax.broadcasted_iota(I32, kg.shape, 1) == 0, 1.0, 0.0).astype(BF16)
        return _dot_nt(jnp.concatenate([kg, one_lane], axis=1), q_aug)

    acc = jnp.zeros((LANES, 2 * tq), F32)
    l = jnp.zeros((1, 2 * tq), F32)
    st_next = scores(groups[0])
    for i, grp in enumerate(groups):
        st = st_next
        if i + 1 < len(groups):
            st_next = scores(groups[i + 1])
        p = jnp.exp2(st)
        l = l + jnp.sum(p, axis=0, keepdims=True)
        vt = jnp.concatenate([vt_ref[0, c] for c in grp], axis=1)
        acc = acc + _dot(vt, p.astype(BF16))
    finish(acc, l)
    underflowed = jnp.sum(jnp.where(l > SOFTMAX_MIN_SUM, 0.0, 1.0)) > 0.0

    @pl.when(underflowed)
    def _():
        m = jnp.full((1, 2 * tq), NEG_BIG, F32)
        l = jnp.zeros((1, 2 * tq), F32)
        acc = jnp.zeros((LANES, 2 * tq), F32)
        for c in chunks:
            st = _dot_nt(k_chunk(c), qq)
            m_new = jnp.maximum(m, jnp.max(st, axis=0, keepdims=True))
            alpha = jnp.exp2(m - m_new)
            p = jnp.exp2(st - m_new)
            l = l * alpha + jnp.sum(p, axis=0, keepdims=True)
            acc = acc * alpha + _dot(vt_ref[0, c], p.astype(BF16))
            m = m_new
        finish(acc, l)


def _attn_call(q, k, vt, lam_vecs, subln_g, lambda_init, *, tq, q_row0, n_rows, chunk0, n_chunks, name):
    b, t, d = q.shape
    q0 = q_row0 // tq
    kb = chunk0 // n_chunks
    assert q_row0 % tq == 0 and chunk0 % n_chunks == 0
    return pl.pallas_call(
        functools.partial(_attn_kernel, chunks=tuple(range(n_chunks)), lambda_init=lambda_init, tq=tq),
        out_shape=jax.ShapeDtypeStruct((b, n_rows, d), BF16),
        grid=(b, DIFF_HEADS, n_rows // tq),
        in_specs=[
            pl.BlockSpec((1, tq, LANES), lambda bi, h, qi: (bi, q0 + qi, h)),
            pl.BlockSpec((1, n_chunks * TK, LANES), lambda bi, h, qi: (bi, kb, h)),
            pl.BlockSpec((1, n_chunks, LANES, TM), lambda bi, h, qi: (bi, kb, h, 0)),
            pl.BlockSpec((4, DIFF_HEAD_DIM), lambda bi, h, qi: (0, 0)),
            pl.BlockSpec((LANES, 1), lambda bi, h, qi: (0, 0)),
        ],
        out_specs=pl.BlockSpec((1, tq, LANES), lambda bi, h, qi: (bi, qi, h)),
        scratch_shapes=[pltpu.VMEM((8, LANES), F32)],
        compiler_params=_cparams(("parallel", "parallel", "arbitrary")),
        name=name,
    )(q, k, vt, lam_vecs, subln_g.reshape(LANES, 1))


def _route(logits_t, rb, ri_ref, rw_ref, cnt_ref, gt_ref, first):
    tm = logits_t.shape[1]
    aff = 1.0 / (1.0 + jnp.exp(-logits_t))
    sel = aff + rb
    row = lambda a, e: a[e:e + 1, :]
    gscore = []
    for g in range(N_GROUPS):
        a, b, c, d = (row(sel, g * EXPERTS_PER_GROUP + j) for j in range(EXPERTS_PER_GROUP))
        hi1, lo1, hi2, lo2 = jnp.maximum(a, b), jnp.minimum(a, b), jnp.maximum(c, d), jnp.minimum(c, d)
        gscore.append(jnp.maximum(hi1, hi2) + jnp.maximum(jnp.minimum(hi1, hi2), jnp.maximum(lo1, lo2)))
    best, grp = gscore[0], jnp.zeros((1, tm), I32)
    for g in range(1, N_GROUPS):
        better = gscore[g] > best
        best = jnp.where(better, gscore[g], best)
        grp = jnp.where(better, g, grp)
    in_sel, in_aff = [], []
    for j in range(EXPERTS_PER_GROUP):
        s_j, a_j = row(sel, j), row(aff, j)
        for g in range(1, N_GROUPS):
            s_j = jnp.where(grp == g, row(sel, g * EXPERTS_PER_GROUP + j), s_j)
            a_j = jnp.where(grp == g, row(aff, g * EXPERTS_PER_GROUP + j), a_j)
        in_sel.append(s_j)
        in_aff.append(a_j)
    v0, i0, a0 = in_sel[0], jnp.zeros((1, tm), I32), in_aff[0]
    for j in range(1, EXPERTS_PER_GROUP):
        better = in_sel[j] > v0
        v0 = jnp.where(better, in_sel[j], v0)
        i0 = jnp.where(better, j, i0)
        a0 = jnp.where(better, in_aff[j], a0)
    v1, i1, a1 = jnp.full((1, tm), -jnp.inf, F32), jnp.zeros((1, tm), I32), jnp.zeros((1, tm), F32)
    for j in range(EXPERTS_PER_GROUP):
        better = (in_sel[j] > v1) & (i0 != j)
        v1 = jnp.where(better, in_sel[j], v1)
        i1 = jnp.where(better, j, i1)
        a1 = jnp.where(better, in_aff[j], a1)
    e0 = grp * EXPERTS_PER_GROUP + i0
    e1 = grp * EXPERTS_PER_GROUP + i1
    wsum = a0 + a1
    erow = lax.broadcasted_iota(I32, (N_EXPERTS, tm), 0)
    oh0 = (erow == e0).astype(F32)
    oh1 = (erow == e1).astype(F32)
    both = oh0 + oh1
    both_bf = both.astype(BF16)
    tri = (lax.broadcasted_iota(I32, (tm, tm), 0) < lax.broadcasted_iota(I32, (tm, tm), 1)).astype(BF16)
    before = _dot(both_bf, tri)
    round_up = lambda n: jnp.floor((n + (ROW_GROUP - 1.0)) * (1.0 / ROW_GROUP)) * ROW_GROUP
    run_col = round_up(jnp.sum(both, axis=1, keepdims=True))
    e_r = lax.broadcasted_iota(I32, (N_EXPERTS, N_EXPERTS), 0)
    e_c = lax.broadcasted_iota(I32, (N_EXPERTS, N_EXPERTS), 1)
    off_col = _dot((e_c < e_r).astype(BF16), jnp.broadcast_to(run_col, (N_EXPERTS, LANES)).astype(BF16))[:, 0:1]
    base = off_col + before
    s0 = jnp.sum(oh0 * base, axis=0, keepdims=True)
    s1 = jnp.sum(oh1 * base, axis=0, keepdims=True)
    both_pad = jnp.concatenate([both_bf, jnp.zeros((LANES - N_EXPERTS, tm), BF16)], axis=0)
    run_row = round_up(_dot_nt(jnp.ones((8, tm), BF16), both_pad))
    l_r = lax.broadcasted_iota(I32, (LANES, LANES), 0)
    l_c = lax.broadcasted_iota(I32, (LANES, LANES), 1)
    off_row = _dot(run_row.astype(BF16), (l_r < l_c).astype(BF16))

    @pl.when(first)
    def _():
        cnt_ref[...] = jnp.zeros_like(cnt_ref)

    g_row0 = lax.broadcasted_iota(I32, (GROUP_TAB, LANES), 0).astype(F32) * ROW_GROUP
    e_lane = lax.broadcasted_iota(I32, (GROUP_TAB, LANES), 1).astype(F32)
    off_b, run_b, used_b = off_row[0:1], run_row[0:1], cnt_ref[0:1, :]
    in_run = (g_row0 >= off_b) & (g_row0 < off_b + run_b)
    code = jnp.sum(jnp.where(in_run, (used_b + g_row0 - off_b) * N_EXPERTS + e_lane, 0.0), axis=1, keepdims=True)
    n_groups = jnp.sum(run_b, axis=1, keepdims=True) * (1.0 / ROW_GROUP)
    code = jnp.where(lax.broadcasted_iota(I32, (GROUP_TAB, 1), 0) == GROUP_TAB - 1, n_groups, code)
    gt_ref[0] = jnp.broadcast_to(code, (GROUP_TAB, LANES)).astype(I32)

    ri_ref[0, 0:1, :] = s0.astype(I32)
    ri_ref[0, 1:2, :] = s1.astype(I32)
    ri_ref[0, 2:3, :] = e0
    ri_ref[0, 3:4, :] = e1
    ri_ref[0, 4:8, :] = jnp.zeros((4, tm), I32)
    cnt_ref[...] = cnt_ref[...] + run_row
    rw_ref[0, 0:1, :] = a0 / wsum
    rw_ref[0, 1:2, :] = a1 / wsum
    rw_ref[0, 2:8, :] = jnp.zeros((6, tm), F32)


def _post_mixer(y, x, mod, lng, lnb, rwt, rb, x1_ref, ri_ref, rw_ref, cnt_ref, gt_ref, first, d):
    x1 = _layer_norm(ALPHA * x + mod[:, 2 * d:3 * d] * y, lng, lnb)
    x1_ref[0] = x1
    tok = x1 * (1.0 + mod[:, 4 * d:5 * d]) + mod[:, 3 * d:4 * d]
    t_hi = tok.astype(BF16)
    t_mid = (tok - t_hi.astype(F32)).astype(BF16)
    hi_terms = _dot(t_hi, rwt)
    logits = hi_terms[:, 0:LANES] + (hi_terms[:, LANES:] + _dot(t_mid, rwt[:, 0:LANES]))
    _route(logits.T[0:N_EXPERTS, :], rb, ri_ref, rw_ref, cnt_ref, gt_ref, first)


def _attn_out_kernel(ol_ref, oc_ref, xl_ref, xc_ref, mod_ref, wo_ref, lng_ref, lnb_ref, rwt_ref, rb_ref,
                     x1_ref, ri_ref, rw_ref, cnt_ref, gt_ref, *, d, n_lat):
    first = (pl.program_id(0) == 0) & (pl.program_id(1) == 0)
    latent = pl.program_id(1) < n_lat
    o = jnp.where(latent, ol_ref[0], oc_ref[0])
    x = jnp.where(latent, xl_ref[0], xc_ref[0])
    y = _dot(o, wo_ref[...])
    _post_mixer(y, x, mod_ref[0], lng_ref[...], lnb_ref[...], rwt_ref[...], rb_ref[...],
                x1_ref, ri_ref, rw_ref, cnt_ref, gt_ref, first, d)


def _route_out_shapes(b, n_tiles, t_out, d):
    return (jax.ShapeDtypeStruct((b, t_out, d), F32),
            jax.ShapeDtypeStruct((b * n_tiles, 8, TM), I32),
            jax.ShapeDtypeStruct((b * n_tiles, 8, TM), F32),
            jax.ShapeDtypeStruct((8, LANES), F32),
            jax.ShapeDtypeStruct((b * n_tiles, GROUP_TAB, LANES), I32))


def _route_out_specs(n_tiles, d):
    return (pl.BlockSpec((1, TM, d), lambda bi, j: (bi, j, 0)),
            pl.BlockSpec((1, 8, TM), lambda bi, j: (bi * n_tiles + j, 0, 0)),
            pl.BlockSpec((1, 8, TM), lambda bi, j: (bi * n_tiles + j, 0, 0)),
            pl.BlockSpec((8, LANES), lambda bi, j: (0, 0)),
            pl.BlockSpec((1, GROUP_TAB, LANES), lambda bi, j: (bi * n_tiles + j, 0, 0)))


def _attn_out_call(o_lat, o_ctx, x, ctx, mod, wo, lng, lnb, rwt, rb, ctx_row):
    b, s_len, d = x.shape
    n_lat = s_len // TM
    nt = n_lat + 1
    row = lambda a: pl.BlockSpec(a.shape, lambda bi, j: (0,) * a.ndim)
    lat = pl.BlockSpec((1, TM, d), lambda bi, j: (bi, jnp.minimum(j, n_lat - 1), 0))
    cxt = pl.BlockSpec((1, TM, d), lambda bi, j: (bi, 0, 0))
    return pl.pallas_call(
        functools.partial(_attn_out_kernel, d=d, n_lat=n_lat),
        out_shape=_route_out_shapes(b, nt, nt * TM, d),
        grid=(b, nt),
        in_specs=[
            lat, cxt, lat, cxt,
            pl.BlockSpec((1, 1, 6 * d), lambda bi, j: (jnp.where(j < n_lat, bi, ctx_row), 0, 0)),
            row(wo), row(lng), row(lnb), row(rwt), row(rb),
        ],
        out_specs=_route_out_specs(nt, d),
        compiler_params=_cparams(("arbitrary", "arbitrary")),
        name="attn_out_route",
    )(o_lat, o_ctx, x, ctx, mod, wo, lng, lnb, rwt, rb)


def _group_copy(src, r_src, dst, r_dst, sem):
    return pltpu.make_async_copy(src.at[pl.ds(pl.multiple_of(r_src, ROW_GROUP), ROW_GROUP)],
                                 dst.at[pl.ds(pl.multiple_of(r_dst, ROW_GROUP), ROW_GROUP)], sem)


def _n_groups(tab_ref, tile):
    return tab_ref[tile * GROUP_TAB + GROUP_TAB - 1]


def _run_copies(tab_ref, pad_ref, tile, stage, hbm, sem, to_hbm):
    def issue(g, carry):
        code = tab_ref[tile * GROUP_TAB + g]
        s = g * ROW_GROUP
        h = pad_ref[jnp.bitwise_and(code, N_EXPERTS - 1)] + jnp.right_shift(code, N_EXPERTS.bit_length() - 1)
        (_group_copy(stage, s, hbm, h, sem) if to_hbm else _group_copy(hbm, h, stage, s, sem)).start()
        return carry

    lax.fori_loop(0, _n_groups(tab_ref, tile), issue, 0)


def _run_waits(tab_ref, tile, stage, hbm, sem, to_hbm):
    def drain(g, carry):
        (_group_copy(stage, 0, hbm, 0, sem) if to_hbm else _group_copy(hbm, 0, stage, 0, sem)).wait()
        return carry

    lax.fori_loop(0, _n_groups(tab_ref, tile), drain, 0)


def _dispatch_kernel(tab_ref, pad_ref, fill_ref, x_ref, mod_ref, ri_ref, xs_ref, stage_ref, zero_ref, sems, sem_z,
                     *, d, n_steps):
    step = pl.program_id(0) * pl.num_programs(1) + pl.program_id(1)
    buf = step % 2
    mod = mod_ref[0]
    tok = (x_ref[0] * (1.0 + mod[:, 4 * d:5 * d]) + mod[:, 3 * d:4 * d]).astype(BF16)
    slot = lax.broadcasted_iota(I32, (STAGE_ROWS, TM), 0)
    ri = ri_ref[0]
    pick = jnp.where((slot == ri[0:1, :]) | (slot == ri[1:2, :]), 1.0, 0.0).astype(BF16)
    for bsel in range(2):
        @pl.when(buf == bsel)
        def _():
            stage_ref[bsel] = _dot(pick, tok).astype(BF16)
            _run_copies(tab_ref, pad_ref, step, stage_ref.at[bsel], xs_ref, sems.at[bsel], True)

    for bsel in range(2):
        @pl.when((buf != bsel) & (step > 0))
        def _():
            _run_waits(tab_ref, step - 1, stage_ref.at[bsel], xs_ref, sems.at[bsel], True)

    @pl.when(step == n_steps - 1)
    def _():
        for bsel in range(2):
            @pl.when(buf == bsel)
            def _():
                _run_waits(tab_ref, step, stage_ref.at[bsel], xs_ref, sems.at[bsel], True)

        zero_ref[...] = jnp.zeros_like(zero_ref)
        for e in range(N_EXPERTS + 1):
            lo, hi = fill_ref[2 * e], fill_ref[2 * e + 1]

            def fill(g, carry):
                _group_copy(zero_ref, 0, xs_ref, g * ROW_GROUP, sem_z).start()
                return carry

            def fill_wait(g, carry):
                _group_copy(zero_ref, 0, xs_ref, 0, sem_z).wait()
                return carry

            lax.fori_loop(lo, hi, fill, 0)
            lax.fori_loop(lo, hi, fill_wait, 0)


def _dispatch_call(tab, pad_start, fill, x1, mod, ri, n_tiles, n_pad, ctx_row, n_lat):
    b, t, d = x1.shape
    grid_spec = pltpu.PrefetchScalarGridSpec(
        num_scalar_prefetch=3,
        grid=(b, n_tiles),
        in_specs=[
            pl.BlockSpec((1, TM, d), lambda bi, j, *_: (bi, j, 0)),
            pl.BlockSpec((1, 1, 6 * d), lambda bi, j, *_: (jnp.where(j < n_lat, bi, ctx_row), 0, 0)),
            pl.BlockSpec((1, 8, TM), lambda bi, j, *_: (bi * n_tiles + j, 0, 0)),
        ],
        out_specs=pl.BlockSpec(memory_space=pl.ANY),
        scratch_shapes=[pltpu.VMEM((2, STAGE_ROWS, d), BF16), pltpu.VMEM((ROW_GROUP, d), BF16),
                        pltpu.SemaphoreType.DMA((2,)), pltpu.SemaphoreType.DMA],
    )
    return pl.pallas_call(
        functools.partial(_dispatch_kernel, d=d, n_steps=b * n_tiles),
        out_shape=jax.ShapeDtypeStruct((n_pad, d), BF16),
        grid_spec=grid_spec,
        compiler_params=_cparams(("arbitrary", "arbitrary")),
        name="moe_dispatch",
    )(tab, pad_start, fill, x1, mod, ri)


def _ffn_kernel(be_ref, nu_ref, xs_ref, wg_ref, wu_ref, wd_ref, ys_ref, wg_s, wu_s, wd_s):
    i = pl.program_id(0)

    @pl.when(i < nu_ref[0])
    def _():
        prev = be_ref[jnp.maximum(i - 1, 0)]

        @pl.when((i == 0) | (be_ref[i] != prev))
        def _():
            wg_s[...] = wg_ref[0, 0].astype(BF16)
            wu_s[...] = wu_ref[0, 0].astype(BF16)
            wd_s[...] = wd_ref[0, 0].astype(BF16)

        rows = [slice(r, r + MOE_BLK // FFN_SLICES) for r in range(0, MOE_BLK, MOE_BLK // FFN_SLICES)]
        gate_up = [(_dot(xs_ref[r, :], wg_s[...]), _dot(xs_ref[r, :], wu_s[...])) for r in rows]
        for r, (g, u) in zip(rows, gate_up):
            ys_ref[r, :] = _dot((_silu(g) * u).astype(BF16), wd_s[...]).astype(BF16)

    @pl.when(i >= nu_ref[0])
    def _():
        ys_ref[...] = jnp.zeros_like(ys_ref)


def _ffn_call(block_expert, n_used, xs, w_gate, w_up, w_down, layer):
    n_pad, d = xs.shape
    de = w_gate.shape[-1]
    n_blocks = n_pad // MOE_BLK
    blk = lambda i, be, nu: (jnp.minimum(i, nu[0] - 1), 0)
    exp = lambda i, be, nu: (layer, be[jnp.minimum(i, nu[0] - 1)], 0, 0)
    grid_spec = pltpu.PrefetchScalarGridSpec(
        num_scalar_prefetch=2,
        grid=(n_blocks,),
        in_specs=[
            pl.BlockSpec((MOE_BLK, d), blk),
            pl.BlockSpec((1, 1, d, de), exp),
            pl.BlockSpec((1, 1, d, de), exp),
            pl.BlockSpec((1, 1, de, d), exp),
        ],
        out_specs=pl.BlockSpec((MOE_BLK, d), lambda i, be, nu: (i, 0)),
        scratch_shapes=[pltpu.VMEM((d, de), BF16), pltpu.VMEM((d, de), BF16), pltpu.VMEM((de, d), BF16)],
    )
    return pl.pallas_call(
        _ffn_kernel,
        out_shape=jax.ShapeDtypeStruct((n_pad, d), BF16),
        grid_spec=grid_spec,
        compiler_params=_cparams(("arbitrary",)),
        name="moe_ffn",
    )(block_expert, n_used, xs, w_gate, w_up, w_down)


def _combine_kernel(tab_ref, pad_ref, x_ref, mod_ref, rit_ref, rw_ref, lng_ref, lnb_ref, ys_ref, o_ref,
                    stage_ref, sems, *, d, n_steps):
    step = pl.program_id(0) * pl.num_programs(1) + pl.program_id(1)
    buf = step % 2

    @pl.when(step == 0)
    def _():
        stage_ref[...] = jnp.zeros_like(stage_ref)
        _run_copies(tab_ref, pad_ref, step, stage_ref.at[0], ys_ref, sems.at[0], False)

    for bsel in range(2):
        @pl.when((buf != bsel) & (step + 1 < n_steps))
        def _():
            _run_copies(tab_ref, pad_ref, step + 1, stage_ref.at[bsel], ys_ref, sems.at[bsel], False)

    rit = rit_ref[0]
    lane = lax.broadcasted_iota(I32, (TM, STAGE_ROWS), 1)
    pick0 = jnp.where(lane == rit[:, 0:1], 1.0, 0.0).astype(BF16)
    pick1 = jnp.where(lane == rit[:, 1:2], 1.0, 0.0).astype(BF16)
    w = rw_ref[0]
    mod = mod_ref[0]
    for bsel in range(2):
        @pl.when(buf == bsel)
        def _():
            _run_waits(tab_ref, step, stage_ref.at[bsel], ys_ref, sems.at[bsel], False)
            stage = stage_ref[bsel]
            moe = w[:, 0:1] * _dot(pick0, stage) + w[:, 1:2] * _dot(pick1, stage)
            o_ref[0] = _layer_norm(ALPHA * x_ref[0] + mod[:, 5 * d:6 * d] * moe, lng_ref[...], lnb_ref[...])


def _combine_call(tab, pad_start, x1, mod, ri_t, rw_t, lng, lnb, ys, n_tiles, ctx_row, n_lat):
    b, _, d = x1.shape
    row = lambda a: pl.BlockSpec(a.shape, lambda bi, j, *_: (0,) * a.ndim)
    grid_spec = pltpu.PrefetchScalarGridSpec(
        num_scalar_prefetch=2,
        grid=(b, n_tiles),
        in_specs=[
            pl.BlockSpec((1, TM, d), lambda bi, j, *_: (bi, j, 0)),
            pl.BlockSpec((1, 1, 6 * d), lambda bi, j, *_: (jnp.where(j < n_lat, bi, ctx_row), 0, 0)),
            pl.BlockSpec((1, TM, 8), lambda bi, j, *_: (bi * n_tiles + j, 0, 0)),
            pl.BlockSpec((1, TM, 8), lambda bi, j, *_: (bi * n_tiles + j, 0, 0)),
            row(lng), row(lnb),
            pl.BlockSpec(memory_space=pl.ANY),
        ],
        out_specs=pl.BlockSpec((1, TM, d), lambda bi, j, *_: (bi, j, 0)),
        scratch_shapes=[pltpu.VMEM((2, STAGE_ROWS, d), BF16), pltpu.SemaphoreType.DMA((2,))],
    )
    return pl.pallas_call(
        functools.partial(_combine_kernel, d=d, n_steps=b * n_tiles),
        out_shape=jax.ShapeDtypeStruct((b, n_tiles * TM, d), F32),
        grid_spec=grid_spec,
        compiler_params=_cparams(("arbitrary", "arbitrary")),
        name="moe_combine",
    )(tab, pad_start, x1, mod, ri_t, rw_t, lng, lnb, ys)


def _moe(x1, mod, ri, rw, counts, gt, lng, lnb, w_gate, w_up, w_down, layer, n_tiles, ctx_row, n_lat):
    b, _, d = x1.shape
    n_steps = b * n_tiles
    n_rows_max = 2 * n_steps * TM + (ROW_GROUP - 1) * N_EXPERTS * n_steps
    n_pad = -(-n_rows_max // MOE_BLK) * MOE_BLK + N_EXPERTS * MOE_BLK
    n_blocks = n_pad // MOE_BLK
    cnt = counts[0, :N_EXPERTS].astype(I32)
    padded = (cnt + MOE_BLK - 1) // MOE_BLK * MOE_BLK
    pad_end = jnp.cumsum(padded)
    pad_start = jnp.concatenate([jnp.zeros((1,), I32), pad_end]).astype(I32)
    block_row0 = jnp.arange(n_blocks, dtype=I32) * MOE_BLK
    block_expert = jnp.minimum(
        jnp.sum((pad_end[None, :] <= block_row0[:, None]).astype(I32), axis=1), N_EXPERTS - 1).astype(I32)
    n_used = (pad_end[-1:] // MOE_BLK).astype(I32)
    fill_lo = jnp.concatenate([pad_start[:-1] + cnt, pad_end[-1:]]) // ROW_GROUP
    fill_hi = jnp.concatenate([pad_end, jnp.full((1,), n_pad, I32)]) // ROW_GROUP
    fill = jnp.stack([fill_lo, fill_hi], axis=1).reshape(-1).astype(I32)
    tab = gt[:, :, 0].reshape(-1)
    xs = _dispatch_call(tab, pad_start, fill, x1, mod, ri, n_tiles, n_pad, ctx_row, n_lat)
    ys = _ffn_call(block_expert, n_used, xs, w_gate, w_up, w_down, layer)
    ri_t, rw_t = jnp.swapaxes(ri, 1, 2), jnp.swapaxes(rw, 1, 2)
    return _combine_call(tab, pad_start, x1, mod, ri_t, rw_t, lng, lnb, ys, n_tiles, ctx_row, n_lat)


def _gla_proj_kernel(x_ref, mod_ref, win_ref, wgd_ref, wgu_ref, bg_ref, q_ref, k_ref, v_ref, r_ref, g_ref, *, d):
    mod = mod_ref[0]
    h = (x_ref[0] * (1.0 + mod[:, d:2 * d]) + mod[:, 0:d]).astype(BF16)
    dk = GLA_HEADS * GLA_DK
    p = _dot(h, win_ref[...])
    q_ref[0] = (p[:, 0:dk] * GLA_DK ** -0.5).astype(BF16)
    k_ref[0] = p[:, dk:2 * dk].astype(BF16)
    v_ref[0] = p[:, 2 * dk:2 * dk + d].astype(BF16)
    r_ref[0] = p[:, 2 * dk + d:].astype(BF16)
    low = _dot(h, wgd_ref[...]).astype(BF16)
    pre = _dot(low, wgu_ref[...]) + bg_ref[...]
    g_ref[0] = (jnp.minimum(pre, 0.0) - jnp.log1p(jnp.exp(-jnp.abs(pre)))) * (1.0 / GLA_TAU)


def _gla_proj_call(xs, mod, win, wgd, wgu, bg, ctx_row):
    b, t, d = xs.shape
    nt = t // TM
    n_lat = nt - 1
    dk = GLA_HEADS * GLA_DK
    row = lambda a: pl.BlockSpec(a.shape, lambda bi, j: (0,) * a.ndim)
    tile = lambda n: pl.BlockSpec((1, TM, n), lambda bi, j: (bi, j, 0))
    return pl.pallas_call(
        functools.partial(_gla_proj_kernel, d=d),
        out_shape=(jax.ShapeDtypeStruct((b, t, dk), BF16), jax.ShapeDtypeStruct((b, t, dk), BF16),
                   jax.ShapeDtypeStruct((b, t, d), BF16), jax.ShapeDtypeStruct((b, t, d), BF16),
                   jax.ShapeDtypeStruct((b, t, 2 * dk), F32)),
        grid=(b, nt),
        in_specs=[tile(d), pl.BlockSpec((1, 1, 6 * d), lambda bi, j: (jnp.where(j < n_lat, bi, ctx_row), 0, 0)),
                  row(win), row(wgd), row(wgu), row(bg)],
        out_specs=(tile(dk), tile(dk), tile(d), tile(d), tile(2 * dk)),
        compiler_params=_cparams(("parallel", "parallel")),
        name="gla_proj",
    )(xs, mod, win, wgd, wgu, bg)


def _split3(x):
    hi = x.astype(BF16)
    r1 = x - hi.astype(F32)
    mid = r1.astype(BF16)
    lo = (r1 - mid.astype(F32)).astype(BF16)
    return hi, mid, lo


class _GlaMasks:
    def __init__(self, reverse):
        L = GLA_CHUNK
        t = lax.broadcasted_iota(I32, (L, L), 0)
        s = lax.broadcasted_iota(I32, (L, L), 1)
        seen = (s >= t) if reverse else (s <= t)
        self.tri = seen.astype(BF16)
        blk = lambda r: jnp.right_shift(r, GLA_SUB.bit_length() - 1)
        dist = (blk(s) - blk(t)) if reverse else (blk(t) - blk(s))
        self.pairs = [seen & (dist == 0)] + [dist == n for n in range(1, L // GLA_SUB)]


def _gla_chunk_matmuls(q_ref, k_ref, v_ref, b_all, rows, h, reverse):
    L, SB, DK = GLA_CHUNK, GLA_SUB, GLA_DK
    nb = L // SB
    kcols = slice(h * DK, (h + 1) * DK)
    vcols = slice(h * GLA_DV, (h + 1) * GLA_DV)
    bh = b_all[:, kcols]
    zero = jnp.zeros((1, DK), F32)
    earlier = lambda i, n: i + n if reverse else i - n

    def start_ref(i):
        if not 0 <= i < nb or i == (nb - 1 if reverse else 0):
            return zero
        r = (i + 1) * SB if reverse else i * SB - 1
        return bh[r:r + 1, :]

    def end_ref(i):
        r = i * SB if reverse else i * SB + SB - 1
        return bh[r:r + 1, :]

    per_block = lambda f: jnp.concatenate([jnp.broadcast_to(f(i), (SB, DK)) for i in range(nb)], axis=0)
    bs, be = per_block(start_ref), per_block(end_ref)
    b_last = end_ref(0 if reverse else nb - 1)
    q = q_ref[0, rows, kcols].astype(F32)
    k = k_ref[0, rows, kcols].astype(F32)
    v = v_ref[0, rows, vcols]
    scaled = lambda x, e: (x * jnp.exp(e)).astype(BF16)
    q_own = scaled(q, bh - bs)
    k_own, k_end = scaled(k, bs - bh), scaled(k, be - bh)
    pieces = [_dot_nt(q_own, k_own), _dot_nt(q_own, k_end)]
    for n in range(2, nb):
        pieces.append(_dot_nt(scaled(q, bh - per_block(lambda i: start_ref(earlier(i, n - 1)))), k_end))
    update = _dot_tn(v, scaled(k, b_last - bh))
    return pieces, scaled(q, bh), update, b_last, v, vcols


def _gla_scan_kernel(qf, kf, vf, gf, qb, kb, vb, gb, of_ref, ob_ref, sf_ref, sb_ref):
    @pl.when(pl.program_id(1) == 0)
    def _():
        sf_ref[...] = jnp.zeros_like(sf_ref)
        sb_ref[...] = jnp.zeros_like(sb_ref)

    L = GLA_CHUNK
    n_chunks = TM // L
    dirs = ((qf, kf, vf, gf, of_ref, sf_ref, False, _GlaMasks(False)),
            (qb, kb, vb, gb, ob_ref, sb_ref, True, _GlaMasks(True)))
    gate_sums = [[sum(_dot(masks.tri, part) for part in _split3(g_ref[0, c * L:(c + 1) * L, :]))
                  for c in range(n_chunks)] for _, _, _, g_ref, _, _, _, masks in dirs]
    work = []
    for step in range(n_chunks):
        for d_i, (q_ref, k_ref, v_ref, g_ref, o_ref, state_ref, reverse, masks) in enumerate(dirs):
            c = n_chunks - 1 - step if reverse else step
            rows = slice(c * L, (c + 1) * L)
            for h in range(GLA_HEADS):
                work.append((o_ref, state_ref, rows, h, masks,
                             _gla_chunk_matmuls(q_ref, k_ref, v_ref, gate_sums[d_i][c], rows, h, reverse)))
    for o_ref, state_ref, rows, h, masks, (pieces, q_int, update, b_last, v, vcols) in work:
        a = sum(jnp.where(m, p, 0.0) for m, p in zip(masks.pairs, pieces)).astype(BF16)
        state_t = state_ref[h]
        o_ref[0, rows, vcols] = _dot(a, v) + _dot_nt(q_int, state_t.astype(BF16))
        state_ref[h] = jnp.exp(b_last) * state_t + update


def _gla_scan_call(q, k, v, g):
    b, t, d = v.shape
    nt = t // TM
    n_lat = nt - 1
    dk = GLA_HEADS * GLA_DK
    fwd = lambda j: jnp.where(j == 0, n_lat, j - 1)
    bwd = lambda j: jnp.where(j == 0, n_lat, n_lat - j)
    spec = lambda n, idx, col: pl.BlockSpec((1, TM, n), lambda bi, j: (bi, idx(j), col))
    return pl.pallas_call(
        _gla_scan_kernel,
        out_shape=(jax.ShapeDtypeStruct((b, n_lat * TM, d), F32), jax.ShapeDtypeStruct((b, n_lat * TM, d), F32)),
        grid=(b, nt),
        in_specs=[spec(dk, fwd, 0), spec(dk, fwd, 0), spec(d, fwd, 0), spec(dk, fwd, 0),
                  spec(dk, bwd, 0), spec(dk, bwd, 0), spec(d, bwd, 0), spec(dk, bwd, 1)],
        out_specs=(pl.BlockSpec((1, TM, d), lambda bi, j: (bi, jnp.maximum(j - 1, 0), 0)),
                   pl.BlockSpec((1, TM, d), lambda bi, j: (bi, n_lat - jnp.maximum(j, 1), 0))),
        scratch_shapes=[pltpu.VMEM((GLA_HEADS, GLA_DV, GLA_DK), F32), pltpu.VMEM((GLA_HEADS, GLA_DV, GLA_DK), F32)],
        compiler_params=_cparams(("parallel", "arbitrary")),
        name="gla_scan",
    )(q, k, v, g, q, k, v, g)


def _gla_out_kernel(of_ref, ob_ref, r_ref, x_ref, mod_ref, ng_ref, wo_ref, lng_ref, lnb_ref, rwt_ref, rb_ref,
                    x1_ref, ri_ref, rw_ref, cnt_ref, gt_ref, *, d):
    first = (pl.program_id(0) == 0) & (pl.program_id(1) == 0)
    o = of_ref[0] + ob_ref[0]
    r = r_ref[0].astype(F32)
    gate = _silu(r)
    ng = ng_ref[...]
    parts = []
    for h in range(GLA_HEADS):
        cols = slice(h * GLA_DV, (h + 1) * GLA_DV)
        oh = o[:, cols]
        ms = jnp.mean(oh * oh, axis=-1, keepdims=True)
        parts.append((oh * lax.rsqrt(ms + LN_EPS) * ng * gate[:, cols]).astype(BF16))
    y = _dot(jnp.concatenate(parts, axis=-1), wo_ref[...])
    _post_mixer(y, x_ref[0], mod_ref[0], lng_ref[...], lnb_ref[...], rwt_ref[...], rb_ref[...],
                x1_ref, ri_ref, rw_ref, cnt_ref, gt_ref, first, d)


def _gla_out_call(o_f, o_b, r, xs, mod, ng, wo, lng, lnb, rwt, rb):
    b, s_len, d = o_f.shape
    n_lat = s_len // TM
    row = lambda a: pl.BlockSpec(a.shape, lambda bi, j: (0,) * a.ndim)
    tile = pl.BlockSpec((1, TM, d), lambda bi, j: (bi, j, 0))
    return pl.pallas_call(
        functools.partial(_gla_out_kernel, d=d),
        out_shape=_route_out_shapes(b, n_lat, s_len, d),
        grid=(b, n_lat),
        in_specs=[tile, tile, tile, tile, pl.BlockSpec((1, 1, 6 * d), lambda bi, j: (bi, 0, 0)),
                  row(ng), row(wo), row(lng), row(lnb), row(rwt), row(rb)],
        out_specs=_route_out_specs(n_lat, d),
        compiler_params=_cparams(("arbitrary", "arbitrary")),
        name="gla_out_route",
    )(o_f, o_b, r, xs, mod, ng, wo, lng, lnb, rwt, rb)


def kernel(x, c, ctx, c_ctx, ada_w, ada_b, ln_g, ln_b, attn_w_qkv, attn_w_o, attn_lambda, attn_subln_g, gla_w_in, gla_wg_down, gla_wg_up, gla_bg, gla_norm_g, gla_w_o, router_w, router_b, moe_w_gate, moe_w_up, moe_w_down):
    b, s_len, d = x.shape
    c_len = ctx.shape[1]
    assert ada_w.shape[0] == DEPTH and c_len == TM and s_len % TQ == 0 and TQ % TM == 0 and s_len % GRID_W == 0
    assert d == DIFF_HEADS * 2 * DIFF_HEAD_DIM == GLA_HEADS * GLA_DV
    assert STAGE_ROWS % LANES == 0 and STAGE_ROWS >= 2 * TM + (ROW_GROUP - 1) * N_EXPERTS
    n_lat = s_len // TM
    nt = n_lat + 1
    ctx_row = b

    cond_rows = -(-(b + 1) // 8) * 8
    cond = jnp.concatenate([c, c_ctx[None, :], jnp.zeros((cond_rows - b - 1, d), F32)], axis=0)
    mods = _ada_call(cond, ada_w, ada_b)
    rw_pad = jnp.concatenate([router_w, jnp.zeros((d, LANES - N_EXPERTS), F32)], axis=1)
    rw_hi = rw_pad.astype(BF16)
    rwt = jnp.concatenate([rw_hi, (rw_pad - rw_hi.astype(F32)).astype(BF16)], axis=1)
    rb = router_b.reshape(N_EXPERTS, 1)

    mod0 = mods[0].reshape(cond_rows, 1, 6 * d)
    wqkv = attn_w_qkv[0]
    wqk = wqkv[:, :2 * d].astype(BF16)
    wvt = wqkv[:, 2 * d:].T.astype(BF16)
    tables = tuple(jnp.asarray(t) for t in _rope_tables(s_len, c_len))
    q, k, vt = _qkv_call(x, ctx, mod0, wqk, wvt, tables, ctx_row)
    lambda_init = 0.8 - 0.6 * math.exp(-0.3 * 0)
    attn = functools.partial(_attn_call, q, k, vt, attn_lambda[0], attn_subln_g[0], lambda_init)
    o_lat = attn(tq=TQ, q_row0=0, n_rows=s_len, chunk0=0, n_chunks=nt, name="diff_attn_lat")
    o_ctx = attn(tq=c_len, q_row0=s_len, n_rows=c_len, chunk0=n_lat, n_chunks=1, name="diff_attn_ctx")
    x1, ri, rw, counts, gt = _attn_out_call(o_lat, o_ctx, x, ctx, mod0, attn_w_o[0].astype(BF16),
                                        ln_g[0, 0][None], ln_b[0, 0][None], rwt, rb, ctx_row)
    stream = _moe(x1, mod0, ri, rw, counts, gt, ln_g[0, 1][None], ln_b[0, 1][None],
                  moe_w_gate, moe_w_up, moe_w_down, 0, nt, ctx_row, n_lat)

    mod1 = mods[1].reshape(cond_rows, 1, 6 * d)
    dk = GLA_HEADS * GLA_DK
    wgd = jnp.concatenate([gla_wg_down[0, 0], gla_wg_down[0, 1],
                           jnp.zeros((d, LANES - 2 * GLA_GATE_RANK), F32)], axis=1).astype(BF16)
    wgu = jnp.zeros((LANES, 2 * dk), F32)
    wgu = wgu.at[0:GLA_GATE_RANK, 0:dk].set(gla_wg_up[0, 0])
    wgu = wgu.at[GLA_GATE_RANK:2 * GLA_GATE_RANK, dk:].set(gla_wg_up[0, 1]).astype(BF16)
    bg = gla_bg[0].reshape(1, 2 * dk)
    gq, gk, gv, gr, gg = _gla_proj_call(stream, mod1, gla_w_in[0].astype(BF16), wgd, wgu, bg, ctx_row)
    o_f, o_b = _gla_scan_call(gq, gk, gv, gg)
    x1, ri, rw, counts, gt = _gla_out_call(o_f, o_b, gr, stream, mod1, gla_norm_g[0][None], gla_w_o[0].astype(BF16),
                                       ln_g[1, 0][None], ln_b[1, 0][None], rwt, rb)
    return _moe(x1, mod1, ri, rw, counts, gt, ln_g[1, 1][None], ln_b[1, 1][None],
                moe_w_gate, moe_w_up, moe_w_down, 1, n_lat, ctx_row, n_lat)
```

```python
import functools
import math

import numpy as np
import jax
import jax.numpy as jnp
from jax import lax
from jax.experimental import pallas as pl
from jax.experimental.pallas import tpu as pltpu

F32 = jnp.float32
BF16 = jnp.bfloat16
I32 = jnp.int32

DEPTH = 2
ALPHA = (2.0 * DEPTH) ** 0.25
LN_EPS = 1e-5
GRID_W = 64
ROPE_BASE = 10000.0
DIFF_HEADS = 8
DIFF_HEAD_DIM = 64
ROT_FREQS = DIFF_HEAD_DIM // 4
GLA_HEADS = 4
GLA_DK = 128
GLA_DV = 256
GLA_GATE_RANK = 16
GLA_TAU = 16.0
GLA_CHUNK = 64
GLA_SUB = 16
N_EXPERTS = 16
N_GROUPS = 4
EXPERTS_PER_GROUP = 4
LOG2E = 1.4426950408889634

LANES = 128
TM = 256
TK = 256
TQ = 512
ATTN_GROUP = 4
MOE_BLK = 512
FFN_SLICES = 2
ADA_TN = 1536
ROW_GROUP = 16
STAGE_ROWS = 768
GROUP_TAB = 64
VMEM_LIMIT = 48 * 1024 * 1024
NEG_BIG = -1e30
SOFTMAX_HEADROOM = 64.0
BOUND_SLACK = 1.01
SOFTMAX_MIN_SUM = 2.0 ** -40


def _cparams(sem):
    return pltpu.CompilerParams(dimension_semantics=sem, vmem_limit_bytes=VMEM_LIMIT)


def _silu(x):
    return x * (1.0 / (1.0 + jnp.exp(-x)))


def _layer_norm(z, g, b):
    mu = jnp.mean(z, axis=-1, keepdims=True)
    zc = z - mu
    var = jnp.mean(zc * zc, axis=-1, keepdims=True)
    return zc * lax.rsqrt(var + LN_EPS) * g + b


def _dot(a, b):
    return jnp.dot(a, b, preferred_element_type=F32)


def _dot_nt(a, b, precision=None):
    return lax.dot_general(a, b, (((1,), (1,)), ((), ())), preferred_element_type=F32, precision=precision)


def _dot_tn(a, b):
    return lax.dot_general(a, b, (((0,), (0,)), ((), ())), preferred_element_type=F32)


def _ada_kernel(c_ref, w_ref, b_ref, o_ref):
    s = _silu(c_ref[...])
    o_ref[0] = jnp.dot(s, w_ref[0], preferred_element_type=F32, precision=lax.Precision.HIGHEST) + b_ref[0]


def _ada_call(cond, ada_w, ada_b):
    depth, d, n = ada_w.shape
    rows = cond.shape[0]
    return pl.pallas_call(
        _ada_kernel,
        out_shape=jax.ShapeDtypeStruct((depth, rows, n), F32),
        grid=(depth, n // ADA_TN),
        in_specs=[
            pl.BlockSpec((rows, d), lambda i, j: (0, 0)),
            pl.BlockSpec((1, d, ADA_TN), lambda i, j: (i, 0, j)),
            pl.BlockSpec((1, 1, ADA_TN), lambda i, j: (i, 0, j)),
        ],
        out_specs=pl.BlockSpec((1, rows, ADA_TN), lambda i, j: (i, 0, j)),
        compiler_params=_cparams(("parallel", "parallel")),
        name="ada_mod",
    )(cond, ada_w, ada_b.reshape(depth, 1, n))


def _qkv_kernel(xl_ref, xc_ref, mod_ref, wqk_ref, wvt_ref, cos_ref, sa_ref, sb_ref, q_ref, k_ref, vt_ref, *, d, n_lat):
    mod = mod_ref[0]
    x = jnp.where(pl.program_id(1) < n_lat, xl_ref[0], xc_ref[0])
    h = (x * (1.0 + mod[:, d:2 * d]) + mod[:, 0:d]).astype(BF16)
    qk = _dot(h, wqk_ref[...])
    vt_ref[0, 0] = _dot_nt(wvt_ref[...], h).astype(BF16)
    cos, sa, sb = cos_ref[...], sa_ref[...], sb_ref[...]
    q_scale = DIFF_HEAD_DIM ** -0.5 * LOG2E
    for s in range(2 * d // LANES):
        z = qk[:, s * LANES:(s + 1) * LANES]
        zr = z * cos + pltpu.roll(z, LANES - ROT_FREQS, axis=1) * sa + pltpu.roll(z, ROT_FREQS, axis=1) * sb
        if s < d // LANES:
            q_ref[0, :, s * LANES:(s + 1) * LANES] = (zr * q_scale).astype(BF16)
        else:
            k_ref[0, :, (s * LANES - d):(s * LANES - d + LANES)] = zr.astype(BF16)


def _rope_tables(s_len, c_len):
    rows = s_len // GRID_W
    row = np.repeat(np.arange(rows, dtype=np.float32), GRID_W)
    col = np.tile(np.arange(GRID_W, dtype=np.float32), rows)
    inv_freq = np.float32(ROPE_BASE) ** (-np.arange(ROT_FREQS, dtype=np.float32) / np.float32(ROT_FREQS))
    ang_r = row[:, None] * inv_freq
    ang_c = col[:, None] * inv_freq
    ang = np.concatenate([ang_r, ang_r, ang_c, ang_c], axis=-1).astype(np.float64)
    cos, sin = np.cos(ang), np.sin(ang)
    first_half = (np.arange(DIFF_HEAD_DIM) // ROT_FREQS) % 2 == 0
    sa = np.where(first_half[None, :], -sin, 0.0)
    sb = np.where(first_half[None, :], 0.0, sin)
    pad = lambda t, fill: np.concatenate([t, np.full((c_len, DIFF_HEAD_DIM), fill)], axis=0)
    two = lambda t: np.concatenate([t, t], axis=-1).astype(np.float32)
    return two(pad(cos, 1.0)), two(pad(sa, 0.0)), two(pad(sb, 0.0))


def _qkv_call(x, ctx, mod, wqk, wvt, tables, ctx_row):
    b, s_len, d = x.shape
    n_lat = s_len // TM
    nt = n_lat + 1
    t = nt * TM
    mod_idx = lambda bi, j: (jnp.where(j < n_lat, bi, ctx_row), 0, 0)
    return pl.pallas_call(
        functools.partial(_qkv_kernel, d=d, n_lat=n_lat),
        out_shape=(jax.ShapeDtypeStruct((b, t, d), BF16), jax.ShapeDtypeStruct((b, t, d), BF16),
                   jax.ShapeDtypeStruct((b, nt, d, TM), BF16)),
        grid=(b, nt),
        in_specs=[
            pl.BlockSpec((1, TM, d), lambda bi, j: (bi, jnp.minimum(j, n_lat - 1), 0)),
            pl.BlockSpec((1, TM, d), lambda bi, j: (bi, 0, 0)),
            pl.BlockSpec((1, 1, 6 * d), mod_idx),
            pl.BlockSpec((d, 2 * d), lambda bi, j: (0, 0)),
            pl.BlockSpec((d, d), lambda bi, j: (0, 0)),
            pl.BlockSpec((TM, LANES), lambda bi, j: (j, 0)),
            pl.BlockSpec((TM, LANES), lambda bi, j: (j, 0)),
            pl.BlockSpec((TM, LANES), lambda bi, j: (j, 0)),
        ],
        out_specs=(pl.BlockSpec((1, TM, d), lambda bi, j: (bi, j, 0)),
                   pl.BlockSpec((1, TM, d), lambda bi, j: (bi, j, 0)),
                   pl.BlockSpec((1, 1, d, TM), lambda bi, j: (bi, j, 0, 0))),
        compiler_params=_cparams(("parallel", "parallel")),
        name="qkv_rope",
    )(x, ctx, mod, wqk, wvt, *tables)


def _attn_kernel(q_ref, k_ref, vt_ref, lam_ref, g_ref, o_ref, kn_ref, *, chunks, lambda_init, tq):
    q = q_ref[0]
    lane = lax.broadcasted_iota(I32, q.shape, 1)
    zero = jnp.zeros_like(q)
    qq = jnp.concatenate([jnp.where(lane < DIFF_HEAD_DIM, q, zero),
                          jnp.where(lane >= DIFF_HEAD_DIM, q, zero)], axis=0)
    k_chunk = lambda c: k_ref[0, c * TK:(c + 1) * TK, :]

    def finish(acc, l):
        acc = acc / l
        lv = lam_ref[...]
        lam = (jnp.exp(jnp.sum(lv[0:1] * lv[1:2], axis=1, keepdims=True))
               - jnp.exp(jnp.sum(lv[2:3] * lv[3:4], axis=1, keepdims=True)) + lambda_init)
        ot = acc[:, :tq] - lam * acc[:, tq:]
        ms = jnp.mean(ot * ot, axis=0, keepdims=True)
        ot = ot * lax.rsqrt(ms + LN_EPS) * g_ref[...] * (1.0 - lambda_init)
        o_ref[0] = ot.T.astype(BF16)

    @pl.when(pl.program_id(2) == 0)
    def _():
        sel_r = lax.broadcasted_iota(I32, (LANES, LANES), 0)
        sel_c = lax.broadcasted_iota(I32, (LANES, LANES), 1)
        halves = jnp.where((sel_r < DIFF_HEAD_DIM) == (sel_c < DIFF_HEAD_DIM), 1.0, 0.0).astype(BF16)
        best = jnp.zeros((1, LANES), F32)
        for c in chunks:
            kf = k_chunk(c).astype(F32)
            norms = _dot((kf * kf).astype(BF16), halves)
            best = jnp.maximum(best, jnp.max(norms, axis=0, keepdims=True))
        kn_ref[...] = jnp.broadcast_to(best, kn_ref.shape)

    kn = kn_ref[0:1, :]
    qf = qq.astype(F32)
    qsq = jnp.sum(qf * qf, axis=1, keepdims=True)
    half = lax.broadcasted_iota(I32, (2 * tq, 1), 0) < tq
    bound = jnp.sqrt(qsq * jnp.where(half, kn[:, 0:1], kn[:, DIFF_HEAD_DIM:DIFF_HEAD_DIM + 1])) * BOUND_SLACK
    shift = SOFTMAX_HEADROOM - bound
    lane2 = lax.broadcasted_iota(I32, (2 * tq, LANES), 1)
    q_aug = jnp.concatenate([qq, jnp.where(lane2 == 0, shift, 0.0).astype(BF16)], axis=1)
    groups = [chunks[i:i + ATTN_GROUP] for i in range(0, len(chunks), ATTN_GROUP)]

    def scores(grp):
        kg = k_ref[0, grp[0] * TK:(grp[-1] + 1) * TK, :]
        one_lane = jnp.where(lax.broadcasted_iota(I32, kg.shape, 1) == 0, 1.0, 0.0).astype(BF16)
        return _dot_nt(jnp.concatenate([kg, one_lane], axis=1), q_aug)

    acc = jnp.zeros((LANES, 2 * tq), F32)
    l = jnp.zeros((1, 2 * tq), F32)
    st_next = scores(groups[0])
    for i, grp in enumerate(groups):
        st = st_next
        if i + 1 < len(groups):
            st_next = scores(groups[i + 1])
        p = jnp.exp2(st)
        l = l + jnp.sum(p, axis=0, keepdims=True)
        vt = jnp.concatenate([vt_ref[0, c] for c in grp], axis=1)
        acc = acc + _dot(vt, p.astype(BF16))
    finish(acc, l)
    underflowed = jnp.sum(jnp.where(l > SOFTMAX_MIN_SUM, 0.0, 1.0)) > 0.0

    @pl.when(underflowed)
    def _():
        m = jnp.full((1, 2 * tq), NEG_BIG, F32)
        l = jnp.zeros((1, 2 * tq), F32)
        acc = jnp.zeros((LANES, 2 * tq), F32)
        for c in chunks:
            st = _dot_nt(k_chunk(c), qq)
            m_new = jnp.maximum(m, jnp.max(st, axis=0, keepdims=True))
            alpha = jnp.exp2(m - m_new)
            p = jnp.exp2(st - m_new)
            l = l * alpha + jnp.sum(p, axis=0, keepdims=True)
            acc = acc * alpha + _dot(vt_ref[0, c], p.astype(BF16))
            m = m_new
        finish(acc, l)


def _attn_call(q, k, vt, lam_vecs, subln_g, lambda_init, *, tq, q_row0, n_rows, chunk0, n_chunks, name):
    b, t, d = q.shape
    q0 = q_row0 // tq
    kb = chunk0 // n_chunks
    assert q_row0 % tq == 0 and chunk0 % n_chunks == 0
    return pl.pallas_call(
        functools.partial(_attn_kernel, chunks=tuple(range(n_chunks)), lambda_init=lambda_init, tq=tq),
        out_shape=jax.ShapeDtypeStruct((b, n_rows, d), BF16),
        grid=(b, DIFF_HEADS, n_rows // tq),
        in_specs=[
            pl.BlockSpec((1, tq, LANES), lambda bi, h, qi: (bi, q0 + qi, h)),
            pl.BlockSpec((1, n_chunks * TK, LANES), lambda bi, h, qi: (bi, kb, h)),
            pl.BlockSpec((1, n_chunks, LANES, TM), lambda bi, h, qi: (bi, kb, h, 0)),
            pl.BlockSpec((4, DIFF_HEAD_DIM), lambda bi, h, qi: (0, 0)),
            pl.BlockSpec((LANES, 1), lambda bi, h, qi: (0, 0)),
        ],
        out_specs=pl.BlockSpec((1, tq, LANES), lambda bi, h, qi: (bi, qi, h)),
        scratch_shapes=[pltpu.VMEM((8, LANES), F32)],
        compiler_params=_cparams(("parallel", "parallel", "arbitrary")),
        name=name,
    )(q, k, vt, lam_vecs, subln_g.reshape(LANES, 1))


def _route(logits_t, rb, ri_ref, rw_ref, cnt_ref, gt_ref, first):
    tm = logits_t.shape[1]
    aff = 1.0 / (1.0 + jnp.exp(-logits_t))
    sel = aff + rb
    row = lambda a, e: a[e:e + 1, :]
    gscore = []
    for g in range(N_GROUPS):
        a, b, c, d = (row(sel, g * EXPERTS_PER_GROUP + j) for j in range(EXPERTS_PER_GROUP))
        hi1, lo1, hi2, lo2 = jnp.maximum(a, b), jnp.minimum(a, b), jnp.maximum(c, d), jnp.minimum(c, d)
        gscore.append(jnp.maximum(hi1, hi2) + jnp.maximum(jnp.minimum(hi1, hi2), jnp.maximum(lo1, lo2)))
    best, grp = gscore[0], jnp.zeros((1, tm), I32)
    for g in range(1, N_GROUPS):
        better = gscore[g] > best
        best = jnp.where(better, gscore[g], best)
        grp = jnp.where(better, g, grp)
    in_sel, in_aff = [], []
    for j in range(EXPERTS_PER_GROUP):
        s_j, a_j = row(sel, j), row(aff, j)
        for g in range(1, N_GROUPS):
            s_j = jnp.where(grp == g, row(sel, g * EXPERTS_PER_GROUP + j), s_j)
            a_j = jnp.where(grp == g, row(aff, g * EXPERTS_PER_GROUP + j), a_j)
        in_sel.append(s_j)
        in_aff.append(a_j)
    v0, i0, a0 = in_sel[0], jnp.zeros((1, tm), I32), in_aff[0]
    for j in range(1, EXPERTS_PER_GROUP):
        better = in_sel[j] > v0
        v0 = jnp.where(better, in_sel[j], v0)
        i0 = jnp.where(better, j, i0)
        a0 = jnp.where(better, in_aff[j], a0)
    v1, i1, a1 = jnp.full((1, tm), -jnp.inf, F32), jnp.zeros((1, tm), I32), jnp.zeros((1, tm), F32)
    for j in range(EXPERTS_PER_GROUP):
        better = (in_sel[j] > v1) & (i0 != j)
        v1 = jnp.where(better, in_sel[j], v1)
        i1 = jnp.where(better, j, i1)
        a1 = jnp.where(better, in_aff[j], a1)
    e0 = grp * EXPERTS_PER_GROUP + i0
    e1 = grp * EXPERTS_PER_GROUP + i1
    wsum = a0 + a1
    erow = lax.broadcasted_iota(I32, (N_EXPERTS, tm), 0)
    oh0 = (erow == e0).astype(F32)
    oh1 = (erow == e1).astype(F32)
    both = oh0 + oh1
    both_bf = both.astype(BF16)
    tri = (lax.broadcasted_iota(I32, (tm, tm), 0) < lax.broadcasted_iota(I32, (tm, tm), 1)).astype(BF16)
    before = _dot(both_bf, tri)
    round_up = lambda n: jnp.floor((n + (ROW_GROUP - 1.0)) * (1.0 / ROW_GROUP)) * ROW_GROUP
    run_col = round_up(jnp.sum(both, axis=1, keepdims=True))
    e_r = lax.broadcasted_iota(I32, (N_EXPERTS, N_EXPERTS), 0)
    e_c = lax.broadcasted_iota(I32, (N_EXPERTS, N_EXPERTS), 1)
    off_col = _dot((e_c < e_r).astype(BF16), jnp.broadcast_to(run_col, (N_EXPERTS, LANES)).astype(BF16))[:, 0:1]
    base = off_col + before
    s0 = jnp.sum(oh0 * base, axis=0, keepdims=True)
    s1 = jnp.sum(oh1 * base, axis=0, keepdims=True)
    both_pad = jnp.concatenate([both_bf, jnp.zeros((LANES - N_EXPERTS, tm), BF16)], axis=0)
    run_row = round_up(_dot_nt(jnp.ones((8, tm), BF16), both_pad))
    l_r = lax.broadcasted_iota(I32, (LANES, LANES), 0)
    l_c = lax.broadcasted_iota(I32, (LANES, LANES), 1)
    off_row = _dot(run_row.astype(BF16), (l_r < l_c).astype(BF16))

    @pl.when(first)
    def _():
        cnt_ref[...] = jnp.zeros_like(cnt_ref)

    g_row0 = lax.broadcasted_iota(I32, (GROUP_TAB, LANES), 0).astype(F32) * ROW_GROUP
    e_lane = lax.broadcasted_iota(I32, (GROUP_TAB, LANES), 1).astype(F32)
    off_b, run_b, used_b = off_row[0:1], run_row[0:1], cnt_ref[0:1, :]
    in_run = (g_row0 >= off_b) & (g_row0 < off_b + run_b)
    code = jnp.sum(jnp.where(in_run, (used_b + g_row0 - off_b) * N_EXPERTS + e_lane, 0.0), axis=1, keepdims=True)
    n_groups = jnp.sum(run_b, axis=1, keepdims=True) * (1.0 / ROW_GROUP)
    code = jnp.where(lax.broadcasted_iota(I32, (GROUP_TAB, 1), 0) == GROUP_TAB - 1, n_groups, code)
    gt_ref[0] = jnp.broadcast_to(code, (GROUP_TAB, LANES)).astype(I32)

    ri_ref[0, 0:1, :] = s0.astype(I32)
    ri_ref[0, 1:2, :] = s1.astype(I32)
    ri_ref[0, 2:3, :] = e0
    ri_ref[0, 3:4, :] = e1
    ri_ref[0, 4:8, :] = jnp.zeros((4, tm), I32)
    cnt_ref[...] = cnt_ref[...] + run_row
    rw_ref[0, 0:1, :] = a0 / wsum
    rw_ref[0, 1:2, :] = a1 / wsum
    rw_ref[0, 2:8, :] = jnp.zeros((6, tm), F32)


def _post_mixer(y, x, mod, lng, lnb, rwt, rb, x1_ref, ri_ref, rw_ref, cnt_ref, gt_ref, first, d):
    x1 = _layer_norm(ALPHA * x + mod[:, 2 * d:3 * d] * y, lng, lnb)
    x1_ref[0] = x1
    tok = x1 * (1.0 + mod[:, 4 * d:5 * d]) + mod[:, 3 * d:4 * d]
    t_hi = tok.astype(BF16)
    t_mid = (tok - t_hi.astype(F32)).astype(BF16)
    hi_terms = _dot(t_hi, rwt)
    logits = hi_terms[:, 0:LANES] + (hi_terms[:, LANES:] + _dot(t_mid, rwt[:, 0:LANES]))
    _route(logits.T[0:N_EXPERTS, :], rb, ri_ref, rw_ref, cnt_ref, gt_ref, first)


def _attn_out_kernel(ol_ref, oc_ref, xl_ref, xc_ref, mod_ref, wo_ref, lng_ref, lnb_ref, rwt_ref, rb_ref,
                     x1_ref, ri_ref, rw_ref, cnt_ref, gt_ref, *, d, n_lat):
    first = (pl.program_id(0) == 0) & (pl.program_id(1) == 0)
    latent = pl.program_id(1) < n_lat
    o = jnp.where(latent, ol_ref[0], oc_ref[0])
    x = jnp.where(latent, xl_ref[0], xc_ref[0])
    y = _dot(o, wo_ref[...])
    _post_mixer(y, x, mod_ref[0], lng_ref[...], lnb_ref[...], rwt_ref[...], rb_ref[...],
                x1_ref, ri_ref, rw_ref, cnt_ref, gt_ref, first, d)


def _route_out_shapes(b, n_tiles, t_out, d):
    return (jax.ShapeDtypeStruct((b, t_out, d), F32),
            jax.ShapeDtypeStruct((b * n_tiles, 8, TM), I32),
            jax.ShapeDtypeStruct((b * n_tiles, 8, TM), F32),
            jax.ShapeDtypeStruct((8, LANES), F32),
            jax.ShapeDtypeStruct((b * n_tiles, GROUP_TAB, LANES), I32))


def _route_out_specs(n_tiles, d):
    return (pl.BlockSpec((1, TM, d), lambda bi, j: (bi, j, 0)),
            pl.BlockSpec((1, 8, TM), lambda bi, j: (bi * n_tiles + j, 0, 0)),
            pl.BlockSpec((1, 8, TM), lambda bi, j: (bi * n_tiles + j, 0, 0)),
            pl.BlockSpec((8, LANES), lambda bi, j: (0, 0)),
            pl.BlockSpec((1, GROUP_TAB, LANES), lambda bi, j: (bi * n_tiles + j, 0, 0)))


def _attn_out_call(o_lat, o_ctx, x, ctx, mod, wo, lng, lnb, rwt, rb, ctx_row):
    b, s_len, d = x.shape
    n_lat = s_len // TM
    nt = n_lat + 1
    row = lambda a: pl.BlockSpec(a.shape, lambda bi, j: (0,) * a.ndim)
    lat = pl.BlockSpec((1, TM, d), lambda bi, j: (bi, jnp.minimum(j, n_lat - 1), 0))
    cxt = pl.BlockSpec((1, TM, d), lambda bi, j: (bi, 0, 0))
    return pl.pallas_call(
        functools.partial(_attn_out_kernel, d=d, n_lat=n_lat),
        out_shape=_route_out_shapes(b, nt, nt * TM, d),
        grid=(b, nt),
        in_specs=[
            lat, cxt, lat, cxt,
            pl.BlockSpec((1, 1, 6 * d), lambda bi, j: (jnp.where(j < n_lat, bi, ctx_row), 0, 0)),
            row(wo), row(lng), row(lnb), row(rwt), row(rb),
        ],
        out_specs=_route_out_specs(nt, d),
        compiler_params=_cparams(("arbitrary", "arbitrary")),
        name="attn_out_route",
    )(o_lat, o_ctx, x, ctx, mod, wo, lng, lnb, rwt, rb)


def _group_copy(src, r_src, dst, r_dst, sem):
    return pltpu.make_async_copy(src.at[pl.ds(pl.multiple_of(r_src, ROW_GROUP), ROW_GROUP)],
                                 dst.at[pl.ds(pl.multiple_of(r_dst, ROW_GROUP), ROW_GROUP)], sem)


def _n_groups(tab_ref, tile):
    return tab_ref[tile * GROUP_TAB + GROUP_TAB - 1]


def _run_copies(tab_ref, pad_ref, tile, stage, hbm, sem, to_hbm):
    def issue(g, carry):
        code = tab_ref[tile * GROUP_TAB + g]
        s = g * ROW_GROUP
        h = pad_ref[jnp.bitwise_and(code, N_EXPERTS - 1)] + jnp.right_shift(code, N_EXPERTS.bit_length() - 1)
        (_group_copy(stage, s, hbm, h, sem) if to_hbm else _group_copy(hbm, h, stage, s, sem)).start()
        return carry

    lax.fori_loop(0, _n_groups(tab_ref, tile), issue, 0)


def _run_waits(tab_ref, tile, stage, hbm, sem, to_hbm):
    def drain(g, carry):
        (_group_copy(stage, 0, hbm, 0, sem) if to_hbm else _group_copy(hbm, 0, stage, 0, sem)).wait()
        return carry

    lax.fori_loop(0, _n_groups(tab_ref, tile), drain, 0)


def _dispatch_kernel(tab_ref, pad_ref, fill_ref, x_ref, mod_ref, ri_ref, xs_ref, stage_ref, zero_ref, sems, sem_z,
                     *, d, n_steps):
    step = pl.program_id(0) * pl.num_programs(1) + pl.program_id(1)
    buf = step % 2
    mod = mod_ref[0]
    tok = (x_ref[0] * (1.0 + mod[:, 4 * d:5 * d]) + mod[:, 3 * d:4 * d]).astype(BF16)
    slot = lax.broadcasted_iota(I32, (STAGE_ROWS, TM), 0)
    ri = ri_ref[0]
    pick = jnp.where((slot == ri[0:1, :]) | (slot == ri[1:2, :]), 1.0, 0.0).astype(BF16)
    for bsel in range(2):
        @pl.when(buf == bsel)
        def _():
            stage_ref[bsel] = _dot(pick, tok).astype(BF16)
            _run_copies(tab_ref, pad_ref, step, stage_ref.at[bsel], xs_ref, sems.at[bsel], True)

    for bsel in range(2):
        @pl.when((buf != bsel) & (step > 0))
        def _():
            _run_waits(tab_ref, step - 1, stage_ref.at[bsel], xs_ref, sems.at[bsel], True)

    @pl.when(step == n_steps - 1)
    def _():
        for bsel in range(2):
            @pl.when(buf == bsel)
            def _():
                _run_waits(tab_ref, step, stage_ref.at[bsel], xs_ref, sems.at[bsel], True)

        zero_ref[...] = jnp.zeros_like(zero_ref)
        for e in range(N_EXPERTS + 1):
            lo, hi = fill_ref[2 * e], fill_ref[2 * e + 1]

            def fill(g, carry):
                _group_copy(zero_ref, 0, xs_ref, g * ROW_GROUP, sem_z).start()
                return carry

            def fill_wait(g, carry):
                _group_copy(zero_ref, 0, xs_ref, 0, sem_z).wait()
                return carry

            lax.fori_loop(lo, hi, fill, 0)
            lax.fori_loop(lo, hi, fill_wait, 0)


def _dispatch_call(tab, pad_start, fill, x1, mod, ri, n_tiles, n_pad, ctx_row, n_lat):
    b, t, d = x1.shape
    grid_spec = pltpu.PrefetchScalarGridSpec(
        num_scalar_prefetch=3,
        grid=(b, n_tiles),
        in_specs=[
            pl.BlockSpec((1, TM, d), lambda bi, j, *_: (bi, j, 0)),
            pl.BlockSpec((1, 1, 6 * d), lambda bi, j, *_: (jnp.where(j < n_lat, bi, ctx_row), 0, 0)),
            pl.BlockSpec((1, 8, TM), lambda bi, j, *_: (bi * n_tiles + j, 0, 0)),
        ],
        out_specs=pl.BlockSpec(memory_space=pl.ANY),
        scratch_shapes=[pltpu.VMEM((2, STAGE_ROWS, d), BF16), pltpu.VMEM((ROW_GROUP, d), BF16),
                        pltpu.SemaphoreType.DMA((2,)), pltpu.SemaphoreType.DMA],
    )
    return pl.pallas_call(
        functools.partial(_dispatch_kernel, d=d, n_steps=b * n_tiles),
        out_shape=jax.ShapeDtypeStruct((n_pad, d), BF16),
        grid_spec=grid_spec,
        compiler_params=_cparams(("arbitrary", "arbitrary")),
        name="moe_dispatch",
    )(tab, pad_start, fill, x1, mod, ri)


def _ffn_kernel(be_ref, nu_ref, xs_ref, wg_ref, wu_ref, wd_ref, ys_ref, wg_s, wu_s, wd_s):
    i = pl.program_id(0)

    @pl.when(i < nu_ref[0])
    def _():
        prev = be_ref[jnp.maximum(i - 1, 0)]

        @pl.when((i == 0) | (be_ref[i] != prev))
        def _():
            wg_s[...] = wg_ref[0, 0].astype(BF16)
            wu_s[...] = wu_ref[0, 0].astype(BF16)
            wd_s[...] = wd_ref[0, 0].astype(BF16)

        rows = [slice(r, r + MOE_BLK // FFN_SLICES) for r in range(0, MOE_BLK, MOE_BLK // FFN_SLICES)]
        gate_up = [(_dot(xs_ref[r, :], wg_s[...]), _dot(xs_ref[r, :], wu_s[...])) for r in rows]
        for r, (g, u) in zip(rows, gate_up):
            ys_ref[r, :] = _dot((_silu(g) * u).astype(BF16), wd_s[...]).astype(BF16)

    @pl.when(i >= nu_ref[0])
    def _():
        ys_ref[...] = jnp.zeros_like(ys_ref)


def _ffn_call(block_expert, n_used, xs, w_gate, w_up, w_down, layer):
    n_pad, d = xs.shape
    de = w_gate.shape[-1]
    n_blocks = n_pad // MOE_BLK
    blk = lambda i, be, nu: (jnp.minimum(i, nu[0] - 1), 0)
    exp = lambda i, be, nu: (layer, be[jnp.minimum(i, nu[0] - 1)], 0, 0)
    grid_spec = pltpu.PrefetchScalarGridSpec(
        num_scalar_prefetch=2,
        grid=(n_blocks,),
        in_specs=[
            pl.BlockSpec((MOE_BLK, d), blk),
            pl.BlockSpec((1, 1, d, de), exp),
            pl.BlockSpec((1, 1, d, de), exp),
            pl.BlockSpec((1, 1, de, d), exp),
        ],
        out_specs=pl.BlockSpec((MOE_BLK, d), lambda i, be, nu: (i, 0)),
        scratch_shapes=[pltpu.VMEM((d, de), BF16), pltpu.VMEM((d, de), BF16), pltpu.VMEM((de, d), BF16)],
    )
    return pl.pallas_call(
        _ffn_kernel,
        out_shape=jax.ShapeDtypeStruct((n_pad, d), BF16),
        grid_spec=grid_spec,
        compiler_params=_cparams(("arbitrary",)),
        name="moe_ffn",
    )(block_expert, n_used, xs, w_gate, w_up, w_down)


def _combine_kernel(tab_ref, pad_ref, x_ref, mod_ref, rit_ref, rw_ref, lng_ref, lnb_ref, ys_ref, *rest,
                    d, n_steps, with_proj):
    if with_proj:
        proj_in, (o_ref, *proj_out), (stage_ref, sems) = rest[:5], rest[5:11], rest[11:]
    else:
        o_ref, stage_ref, sems = rest
    step = pl.program_id(0) * pl.num_programs(1) + pl.program_id(1)
    buf = step % 2

    @pl.when(step == 0)
    def _():
        stage_ref[...] = jnp.zeros_like(stage_ref)
        _run_copies(tab_ref, pad_ref, step, stage_ref.at[0], ys_ref, sems.at[0], False)

    for bsel in range(2):
        @pl.when((buf != bsel) & (step + 1 < n_steps))
        def _():
            _run_copies(tab_ref, pad_ref, step + 1, stage_ref.at[bsel], ys_ref, sems.at[bsel], False)

    rit = rit_ref[0]
    lane = lax.broadcasted_iota(I32, (TM, STAGE_ROWS), 1)
    pick0 = jnp.where(lane == rit[:, 0:1], 1.0, 0.0).astype(BF16)
    pick1 = jnp.where(lane == rit[:, 1:2], 1.0, 0.0).astype(BF16)
    w = rw_ref[0]
    mod = mod_ref[0]
    for bsel in range(2):
        @pl.when(buf == bsel)
        def _():
            _run_waits(tab_ref, step, stage_ref.at[bsel], ys_ref, sems.at[bsel], False)
            stage = stage_ref[bsel]
            moe = w[:, 0:1] * _dot(pick0, stage) + w[:, 1:2] * _dot(pick1, stage)
            o_ref[0] = _layer_norm(ALPHA * x_ref[0] + mod[:, 5 * d:6 * d] * moe, lng_ref[...], lnb_ref[...])

    if with_proj:
        mod_n, *weights = proj_in
        _gla_proj(o_ref[0], mod_n[0], *weights, *proj_out, d=d)


def _combine_call(tab, pad_start, x1, mod, ri_t, rw_t, lng, lnb, ys, n_tiles, ctx_row, n_lat, proj=None):
    b, _, d = x1.shape
    t = n_tiles * TM
    row = lambda a: pl.BlockSpec(a.shape, lambda bi, j, *_: (0,) * a.ndim)
    tile = lambda n: pl.BlockSpec((1, TM, n), lambda bi, j, *_: (bi, j, 0))
    mod_spec = pl.BlockSpec((1, 1, 6 * d), lambda bi, j, *_: (jnp.where(j < n_lat, bi, ctx_row), 0, 0))
    in_specs = [
        tile(d), mod_spec,
        pl.BlockSpec((1, TM, 8), lambda bi, j, *_: (bi * n_tiles + j, 0, 0)),
        pl.BlockSpec((1, TM, 8), lambda bi, j, *_: (bi * n_tiles + j, 0, 0)),
        row(lng), row(lnb),
        pl.BlockSpec(memory_space=pl.ANY),
    ]
    out_specs, out_shape, args = tile(d), jax.ShapeDtypeStruct((b, t, d), F32), ()
    if proj is not None:
        dk = GLA_HEADS * GLA_DK
        in_specs += [mod_spec] + [row(a) for a in proj[1:]]
        out_specs = (out_specs, tile(dk), tile(dk), tile(d), tile(d), tile(2 * dk))
        out_shape = (out_shape, *_gla_proj_shapes(b, t, d))
        args = proj
    grid_spec = pltpu.PrefetchScalarGridSpec(
        num_scalar_prefetch=2,
        grid=(b, n_tiles),
        in_specs=in_specs,
        out_specs=out_specs,
        scratch_shapes=[pltpu.VMEM((2, STAGE_ROWS, d), BF16), pltpu.SemaphoreType.DMA((2,))],
    )
    return pl.pallas_call(
        functools.partial(_combine_kernel, d=d, n_steps=b * n_tiles, with_proj=proj is not None),
        out_shape=out_shape,
        grid_spec=grid_spec,
        compiler_params=_cparams(("arbitrary", "arbitrary")),
        name="moe_combine_proj" if proj is not None else "moe_combine",
    )(tab, pad_start, x1, mod, ri_t, rw_t, lng, lnb, ys, *args)


def _moe(x1, mod, ri, rw, counts, gt, lng, lnb, w_gate, w_up, w_down, layer, n_tiles, ctx_row, n_lat, proj=None):
    b, _, d = x1.shape
    n_steps = b * n_tiles
    n_rows_max = 2 * n_steps * TM + (ROW_GROUP - 1) * N_EXPERTS * n_steps
    n_pad = -(-n_rows_max // MOE_BLK) * MOE_BLK + N_EXPERTS * MOE_BLK
    n_blocks = n_pad // MOE_BLK
    cnt = counts[0, :N_EXPERTS].astype(I32)
    padded = (cnt + MOE_BLK - 1) // MOE_BLK * MOE_BLK
    pad_end = jnp.cumsum(padded)
    pad_start = jnp.concatenate([jnp.zeros((1,), I32), pad_end]).astype(I32)
    block_row0 = jnp.arange(n_blocks, dtype=I32) * MOE_BLK
    block_expert = jnp.minimum(
        jnp.sum((pad_end[None, :] <= block_row0[:, None]).astype(I32), axis=1), N_EXPERTS - 1).astype(I32)
    n_used = (pad_end[-1:] // MOE_BLK).astype(I32)
    fill_lo = jnp.concatenate([pad_start[:-1] + cnt, pad_end[-1:]]) // ROW_GROUP
    fill_hi = jnp.concatenate([pad_end, jnp.full((1,), n_pad, I32)]) // ROW_GROUP
    fill = jnp.stack([fill_lo, fill_hi], axis=1).reshape(-1).astype(I32)
    tab = gt[:, :, 0].reshape(-1)
    xs = _dispatch_call(tab, pad_start, fill, x1, mod, ri, n_tiles, n_pad, ctx_row, n_lat)
    ys = _ffn_call(block_expert, n_used, xs, w_gate, w_up, w_down, layer)
    ri_t, rw_t = jnp.swapaxes(ri, 1, 2), jnp.swapaxes(rw, 1, 2)
    return _combine_call(tab, pad_start, x1, mod, ri_t, rw_t, lng, lnb, ys, n_tiles, ctx_row, n_lat, proj)


def _gla_proj(x, mod, win_ref, wgd_ref, wgu_ref, bg_ref, q_ref, k_ref, v_ref, r_ref, g_ref, *, d):
    h = (x * (1.0 + mod[:, d:2 * d]) + mod[:, 0:d]).astype(BF16)
    dk = GLA_HEADS * GLA_DK
    p = _dot(h, win_ref[...])
    q_ref[0] = (p[:, 0:dk] * GLA_DK ** -0.5).astype(BF16)
    k_ref[0] = p[:, dk:2 * dk].astype(BF16)
    v_ref[0] = p[:, 2 * dk:2 * dk + d].astype(BF16)
    r_ref[0] = p[:, 2 * dk + d:].astype(BF16)
    low = _dot(h, wgd_ref[...]).astype(BF16)
    pre = _dot(low, wgu_ref[...]) + bg_ref[...]
    g_ref[0] = (jnp.minimum(pre, 0.0) - jnp.log1p(jnp.exp(-jnp.abs(pre)))) * (1.0 / GLA_TAU)


def _gla_proj_shapes(b, t, d):
    dk = GLA_HEADS * GLA_DK
    return (jax.ShapeDtypeStruct((b, t, dk), BF16), jax.ShapeDtypeStruct((b, t, dk), BF16),
            jax.ShapeDtypeStruct((b, t, d), BF16), jax.ShapeDtypeStruct((b, t, d), BF16),
            jax.ShapeDtypeStruct((b, t, 2 * dk), F32))


def _split3(x):
    hi = x.astype(BF16)
    r1 = x - hi.astype(F32)
    mid = r1.astype(BF16)
    lo = (r1 - mid.astype(F32)).astype(BF16)
    return hi, mid, lo


class _GlaMasks:
    def __init__(self, reverse):
        L = GLA_CHUNK
        t = lax.broadcasted_iota(I32, (L, L), 0)
        s = lax.broadcasted_iota(I32, (L, L), 1)
        seen = (s >= t) if reverse else (s <= t)
        self.tri = seen.astype(BF16)
        blk = lambda r: jnp.right_shift(r, GLA_SUB.bit_length() - 1)
        dist = (blk(s) - blk(t)) if reverse else (blk(t) - blk(s))
        self.pairs = [seen & (dist == 0)] + [dist == n for n in range(1, L // GLA_SUB)]


def _gla_chunk_matmuls(q_ref, k_ref, v_ref, b_all, rows, h, reverse):
    L, SB, DK = GLA_CHUNK, GLA_SUB, GLA_DK
    nb = L // SB
    kcols = slice(h * DK, (h + 1) * DK)
    vcols = slice(h * GLA_DV, (h + 1) * GLA_DV)
    bh = b_all[:, kcols]
    zero = jnp.zeros((1, DK), F32)
    earlier = lambda i, n: i + n if reverse else i - n

    def start_ref(i):
        if not 0 <= i < nb or i == (nb - 1 if reverse else 0):
            return zero
        r = (i + 1) * SB if reverse else i * SB - 1
        return bh[r:r + 1, :]

    def end_ref(i):
        r = i * SB if reverse else i * SB + SB - 1
        return bh[r:r + 1, :]

    per_block = lambda f: jnp.concatenate([jnp.broadcast_to(f(i), (SB, DK)) for i in range(nb)], axis=0)
    bs, be = per_block(start_ref), per_block(end_ref)
    b_last = end_ref(0 if reverse else nb - 1)
    q = q_ref[0, rows, kcols].astype(F32)
    k = k_ref[0, rows, kcols].astype(F32)
    v = v_ref[0, rows, vcols]
    scaled = lambda x, e: (x * jnp.exp(e)).astype(BF16)
    q_own = scaled(q, bh - bs)
    k_own, k_end = scaled(k, bs - bh), scaled(k, be - bh)
    pieces = [_dot_nt(q_own, k_own), _dot_nt(q_own, k_end)]
    for n in range(2, nb):
        pieces.append(_dot_nt(scaled(q, bh - per_block(lambda i: start_ref(earlier(i, n - 1)))), k_end))
    update = _dot_tn(v, scaled(k, b_last - bh))
    return pieces, scaled(q, bh), update, b_last, v, vcols


def _gla_scan_kernel(qf, kf, vf, gf, qb, kb, vb, gb, of_ref, ob_ref, sf_ref, sb_ref):
    @pl.when(pl.program_id(1) == 0)
    def _():
        sf_ref[...] = jnp.zeros_like(sf_ref)
        sb_ref[...] = jnp.zeros_like(sb_ref)

    L = GLA_CHUNK
    n_chunks = TM // L
    dirs = ((qf, kf, vf, gf, of_ref, sf_ref, False, _GlaMasks(False)),
            (qb, kb, vb, gb, ob_ref, sb_ref, True, _GlaMasks(True)))
    gate_sums = [[sum(_dot(masks.tri, part) for part in _split3(g_ref[0, c * L:(c + 1) * L, :]))
                  for c in range(n_chunks)] for _, _, _, g_ref, _, _, _, masks in dirs]
    work = []
    for step in range(n_chunks):
        for d_i, (q_ref, k_ref, v_ref, g_ref, o_ref, state_ref, reverse, masks) in enumerate(dirs):
            c = n_chunks - 1 - step if reverse else step
            rows = slice(c * L, (c + 1) * L)
            for h in range(GLA_HEADS):
                work.append((o_ref, state_ref, rows, h, masks,
                             _gla_chunk_matmuls(q_ref, k_ref, v_ref, gate_sums[d_i][c], rows, h, reverse)))
    for o_ref, state_ref, rows, h, masks, (pieces, q_int, update, b_last, v, vcols) in work:
        a = sum(jnp.where(m, p, 0.0) for m, p in zip(masks.pairs, pieces)).astype(BF16)
        state_t = state_ref[h]
        o_ref[0, rows, vcols] = _dot(a, v) + _dot_nt(q_int, state_t.astype(BF16))
        state_ref[h] = jnp.exp(b_last) * state_t + update


def _gla_scan_call(q, k, v, g):
    b, t, d = v.shape
    nt = t // TM
    n_lat = nt - 1
    dk = GLA_HEADS * GLA_DK
    fwd = lambda j: jnp.where(j == 0, n_lat, j - 1)
    bwd = lambda j: jnp.where(j == 0, n_lat, n_lat - j)
    spec = lambda n, idx, col: pl.BlockSpec((1, TM, n), lambda bi, j: (bi, idx(j), col))
    return pl.pallas_call(
        _gla_scan_kernel,
        out_shape=(jax.ShapeDtypeStruct((b, n_lat * TM, d), F32), jax.ShapeDtypeStruct((b, n_lat * TM, d), F32)),
        grid=(b, nt),
        in_specs=[spec(dk, fwd, 0), spec(dk, fwd, 0), spec(d, fwd, 0), spec(dk, fwd, 0),
                  spec(dk, bwd, 0), spec(dk, bwd, 0), spec(d, bwd, 0), spec(dk, bwd, 1)],
        out_specs=(pl.BlockSpec((1, TM, d), lambda bi, j: (bi, jnp.maximum(j - 1, 0), 0)),
                   pl.BlockSpec((1, TM, d), lambda bi, j: (bi, n_lat - jnp.maximum(j, 1), 0))),
        scratch_shapes=[pltpu.VMEM((GLA_HEADS, GLA_DV, GLA_DK), F32), pltpu.VMEM((GLA_HEADS, GLA_DV, GLA_DK), F32)],
        compiler_params=_cparams(("parallel", "arbitrary")),
        name="gla_scan",
    )(q, k, v, g, q, k, v, g)


def _gla_out_kernel(of_ref, ob_ref, r_ref, x_ref, mod_ref, ng_ref, wo_ref, lng_ref, lnb_ref, rwt_ref, rb_ref,
                    x1_ref, ri_ref, rw_ref, cnt_ref, gt_ref, *, d):
    first = (pl.program_id(0) == 0) & (pl.program_id(1) == 0)
    o = of_ref[0] + ob_ref[0]
    r = r_ref[0].astype(F32)
    gate = _silu(r)
    ng = ng_ref[...]
    parts = []
    for h in range(GLA_HEADS):
        cols = slice(h * GLA_DV, (h + 1) * GLA_DV)
        oh = o[:, cols]
        ms = jnp.mean(oh * oh, axis=-1, keepdims=True)
        parts.append((oh * lax.rsqrt(ms + LN_EPS) * ng * gate[:, cols]).astype(BF16))
    y = _dot(jnp.concatenate(parts, axis=-1), wo_ref[...])
    _post_mixer(y, x_ref[0], mod_ref[0], lng_ref[...], lnb_ref[...], rwt_ref[...], rb_ref[...],
                x1_ref, ri_ref, rw_ref, cnt_ref, gt_ref, first, d)


def _gla_out_call(o_f, o_b, r, xs, mod, ng, wo, lng, lnb, rwt, rb):
    b, s_len, d = o_f.shape
    n_lat = s_len // TM
    row = lambda a: pl.BlockSpec(a.shape, lambda bi, j: (0,) * a.ndim)
    tile = pl.BlockSpec((1, TM, d), lambda bi, j: (bi, j, 0))
    return pl.pallas_call(
        functools.partial(_gla_out_kernel, d=d),
        out_shape=_route_out_shapes(b, n_lat, s_len, d),
        grid=(b, n_lat),
        in_specs=[tile, tile, tile, tile, pl.BlockSpec((1, 1, 6 * d), lambda bi, j: (bi, 0, 0)),
                  row(ng), row(wo), row(lng), row(lnb), row(rwt), row(rb)],
        out_specs=_route_out_specs(n_lat, d),
        compiler_params=_cparams(("arbitrary", "arbitrary")),
        name="gla_out_route",
    )(o_f, o_b, r, xs, mod, ng, wo, lng, lnb, rwt, rb)


def kernel(x, c, ctx, c_ctx, ada_w, ada_b, ln_g, ln_b, attn_w_qkv, attn_w_o, attn_lambda, attn_subln_g, gla_w_in, gla_wg_down, gla_wg_up, gla_bg, gla_norm_g, gla_w_o, router_w, router_b, moe_w_gate, moe_w_up, moe_w_down):
    b, s_len, d = x.shape
    c_len = ctx.shape[1]
    assert ada_w.shape[0] == DEPTH and c_len == TM and s_len % TQ == 0 and TQ % TM == 0 and s_len % GRID_W == 0
    assert d == DIFF_HEADS * 2 * DIFF_HEAD_DIM == GLA_HEADS * GLA_DV
    assert STAGE_ROWS % LANES == 0 and STAGE_ROWS >= 2 * TM + (ROW_GROUP - 1) * N_EXPERTS
    n_lat = s_len // TM
    nt = n_lat + 1
    ctx_row = b

    cond_rows = -(-(b + 1) // 8) * 8
    cond = jnp.concatenate([c, c_ctx[None, :], jnp.zeros((cond_rows - b - 1, d), F32)], axis=0)
    mods = _ada_call(cond, ada_w, ada_b)
    rw_pad = jnp.concatenate([router_w, jnp.zeros((d, LANES - N_EXPERTS), F32)], axis=1)
    rw_hi = rw_pad.astype(BF16)
    rwt = jnp.concatenate([rw_hi, (rw_pad - rw_hi.astype(F32)).astype(BF16)], axis=1)
    rb = router_b.reshape(N_EXPERTS, 1)

    mod0 = mods[0].reshape(cond_rows, 1, 6 * d)
    wqkv = attn_w_qkv[0]
    wqk = wqkv[:, :2 * d].astype(BF16)
    wvt = wqkv[:, 2 * d:].T.astype(BF16)
    tables = tuple(jnp.asarray(t) for t in _rope_tables(s_len, c_len))
    q, k, vt = _qkv_call(x, ctx, mod0, wqk, wvt, tables, ctx_row)
    lambda_init = 0.8 - 0.6 * math.exp(-0.3 * 0)
    attn = functools.partial(_attn_call, q, k, vt, attn_lambda[0], attn_subln_g[0], lambda_init)
    o_lat = attn(tq=TQ, q_row0=0, n_rows=s_len, chunk0=0, n_chunks=nt, name="diff_attn_lat")
    o_ctx = attn(tq=c_len, q_row0=s_len, n_rows=c_len, chunk0=n_lat, n_chunks=1, name="diff_attn_ctx")
    x1, ri, rw, counts, gt = _attn_out_call(o_lat, o_ctx, x, ctx, mod0, attn_w_o[0].astype(BF16),
                                        ln_g[0, 0][None], ln_b[0, 0][None], rwt, rb, ctx_row)

    mod1 = mods[1].reshape(cond_rows, 1, 6 * d)
    dk = GLA_HEADS * GLA_DK
    wgd = jnp.concatenate([gla_wg_down[0, 0], gla_wg_down[0, 1],
                           jnp.zeros((d, LANES - 2 * GLA_GATE_RANK), F32)], axis=1).astype(BF16)
    wgu = jnp.zeros((LANES, 2 * dk), F32)
    wgu = wgu.at[0:GLA_GATE_RANK, 0:dk].set(gla_wg_up[0, 0])
    wgu = wgu.at[GLA_GATE_RANK:2 * GLA_GATE_RANK, dk:].set(gla_wg_up[0, 1]).astype(BF16)
    bg = gla_bg[0].reshape(1, 2 * dk)
    stream, gq, gk, gv, gr, gg = _moe(x1, mod0, ri, rw, counts, gt, ln_g[0, 1][None], ln_b[0, 1][None],
                                      moe_w_gate, moe_w_up, moe_w_down, 0, nt, ctx_row, n_lat,
                                      proj=(mod1, gla_w_in[0].astype(BF16), wgd, wgu, bg))
    o_f, o_b = _gla_scan_call(gq, gk, gv, gg)
    x1, ri, rw, counts, gt = _gla_out_call(o_f, o_b, gr, stream, mod1, gla_norm_g[0][None], gla_w_o[0].astype(BF16),
                                       ln_g[1, 0][None], ln_b[1, 0][None], rwt, rb)
    return _moe(x1, mod1, ri, rw, counts, gt, ln_g[1, 1][None], ln_b[1, 1][None],
                moe_w_gate, moe_w_up, moe_w_down, 1, n_lat, ctx_row, n_lat)
```

```python
import functools
import math

import numpy as np
import jax
import jax.numpy as jnp
from jax import lax
from jax.experimental import pallas as pl
from jax.experimental.pallas import tpu as pltpu

F32 = jnp.float32
BF16 = jnp.bfloat16
I32 = jnp.int32

DEPTH = 2
ALPHA = (2.0 * DEPTH) ** 0.25
LN_EPS = 1e-5
GRID_W = 64
ROPE_BASE = 10000.0
DIFF_HEADS = 8
DIFF_HEAD_DIM = 64
ROT_FREQS = DIFF_HEAD_DIM // 4
GLA_HEADS = 4
GLA_DK = 128
GLA_DV = 256
GLA_GATE_RANK = 16
GLA_TAU = 16.0
GLA_CHUNK = 64
GLA_SUB = 16
N_EXPERTS = 16
N_GROUPS = 4
EXPERTS_PER_GROUP = 4
LOG2E = 1.4426950408889634

LANES = 128
TM = 256
TK = 256
TQ = 512
ATTN_GROUP = 4
MOE_BLK = 512
FFN_SLICES = 2
ADA_TN = 1536
ROW_GROUP = 16
STAGE_ROWS = 768
GROUP_TAB = 64
VMEM_LIMIT = 48 * 1024 * 1024
NEG_BIG = -1e30
SOFTMAX_HEADROOM = 64.0
BOUND_SLACK = 1.01
SOFTMAX_MIN_SUM = 2.0 ** -40


def _cparams(sem):
    return pltpu.CompilerParams(dimension_semantics=sem, vmem_limit_bytes=VMEM_LIMIT)


def _silu(x):
    return x * (1.0 / (1.0 + jnp.exp(-x)))


def _layer_norm(z, g, b):
    mu = jnp.mean(z, axis=-1, keepdims=True)
    zc = z - mu
    var = jnp.mean(zc * zc, axis=-1, keepdims=True)
    return zc * lax.rsqrt(var + LN_EPS) * g + b


def _dot(a, b):
    return jnp.dot(a, b, preferred_element_type=F32)


def _dot_nt(a, b, precision=None):
    return lax.dot_general(a, b, (((1,), (1,)), ((), ())), preferred_element_type=F32, precision=precision)


def _dot_tn(a, b):
    return lax.dot_general(a, b, (((0,), (0,)), ((), ())), preferred_element_type=F32)


def _ada_kernel(c_ref, w_ref, b_ref, o_ref):
    s = _silu(c_ref[...])
    o_ref[0] = jnp.dot(s, w_ref[0], preferred_element_type=F32, precision=lax.Precision.HIGHEST) + b_ref[0]


def _ada_call(cond, ada_w, ada_b):
    depth, d, n = ada_w.shape
    rows = cond.shape[0]
    return pl.pallas_call(
        _ada_kernel,
        out_shape=jax.ShapeDtypeStruct((depth, rows, n), F32),
        grid=(depth, n // ADA_TN),
        in_specs=[
            pl.BlockSpec((rows, d), lambda i, j: (0, 0)),
            pl.BlockSpec((1, d, ADA_TN), lambda i, j: (i, 0, j)),
            pl.BlockSpec((1, 1, ADA_TN), lambda i, j: (i, 0, j)),
        ],
        out_specs=pl.BlockSpec((1, rows, ADA_TN), lambda i, j: (i, 0, j)),
        compiler_params=_cparams(("parallel", "parallel")),
        name="ada_mod",
    )(cond, ada_w, ada_b.reshape(depth, 1, n))


def _qkv_kernel(xl_ref, xc_ref, mod_ref, wqk_ref, wvt_ref, cos_ref, sa_ref, sb_ref, q_ref, k_ref, vt_ref, *, d, n_lat):
    mod = mod_ref[0]
    x = jnp.where(pl.program_id(1) < n_lat, xl_ref[0], xc_ref[0])
    h = (x * (1.0 + mod[:, d:2 * d]) + mod[:, 0:d]).astype(BF16)
    qk = _dot(h, wqk_ref[...])
    vt_ref[0, 0] = _dot_nt(wvt_ref[...], h).astype(BF16)
    cos, sa, sb = cos_ref[...], sa_ref[...], sb_ref[...]
    q_scale = DIFF_HEAD_DIM ** -0.5 * LOG2E
    for s in range(2 * d // LANES):
        z = qk[:, s * LANES:(s + 1) * LANES]
        zr = z * cos + pltpu.roll(z, LANES - ROT_FREQS, axis=1) * sa + pltpu.roll(z, ROT_FREQS, axis=1) * sb
        if s < d // LANES:
            q_ref[0, :, s * LANES:(s + 1) * LANES] = (zr * q_scale).astype(BF16)
        else:
            k_ref[0, :, (s * LANES - d):(s * LANES - d + LANES)] = zr.astype(BF16)


def _rope_tables(s_len, c_len):
    rows = s_len // GRID_W
    row = np.repeat(np.arange(rows, dtype=np.float32), GRID_W)
    col = np.tile(np.arange(GRID_W, dtype=np.float32), rows)
    inv_freq = np.float32(ROPE_BASE) ** (-np.arange(ROT_FREQS, dtype=np.float32) / np.float32(ROT_FREQS))
    ang_r = row[:, None] * inv_freq
    ang_c = col[:, None] * inv_freq
    ang = np.concatenate([ang_r, ang_r, ang_c, ang_c], axis=-1).astype(np.float64)
    cos, sin = np.cos(ang), np.sin(ang)
    first_half = (np.arange(DIFF_HEAD_DIM) // ROT_FREQS) % 2 == 0
    sa = np.where(first_half[None, :], -sin, 0.0)
    sb = np.where(first_half[None, :], 0.0, sin)
    pad = lambda t, fill: np.concatenate([t, np.full((c_len, DIFF_HEAD_DIM), fill)], axis=0)
    two = lambda t: np.concatenate([t, t], axis=-1).astype(np.float32)
    return two(pad(cos, 1.0)), two(pad(sa, 0.0)), two(pad(sb, 0.0))


def _qkv_call(x, ctx, mod, wqk, wvt, tables, ctx_row):
    b, s_len, d = x.shape
    n_lat = s_len // TM
    nt = n_lat + 1
    t = nt * TM
    mod_idx = lambda bi, j: (jnp.where(j < n_lat, bi, ctx_row), 0, 0)
    return pl.pallas_call(
        functools.partial(_qkv_kernel, d=d, n_lat=n_lat),
        out_shape=(jax.ShapeDtypeStruct((b, t, d), BF16), jax.ShapeDtypeStruct((b, t, d), BF16),
                   jax.ShapeDtypeStruct((b, nt, d, TM), BF16)),
        grid=(b, nt),
        in_specs=[
            pl.BlockSpec((1, TM, d), lambda bi, j: (bi, jnp.minimum(j, n_lat - 1), 0)),
            pl.BlockSpec((1, TM, d), lambda bi, j: (bi, 0, 0)),
            pl.BlockSpec((1, 1, 6 * d), mod_idx),
            pl.BlockSpec((d, 2 * d), lambda bi, j: (0, 0)),
            pl.BlockSpec((d, d), lambda bi, j: (0, 0)),
            pl.BlockSpec((TM, LANES), lambda bi, j: (j, 0)),
            pl.BlockSpec((TM, LANES), lambda bi, j: (j, 0)),
            pl.BlockSpec((TM, LANES), lambda bi, j: (j, 0)),
        ],
        out_specs=(pl.BlockSpec((1, TM, d), lambda bi, j: (bi, j, 0)),
                   pl.BlockSpec((1, TM, d), lambda bi, j: (bi, j, 0)),
                   pl.BlockSpec((1, 1, d, TM), lambda bi, j: (bi, j, 0, 0))),
        compiler_params=_cparams(("parallel", "parallel")),
        name="qkv_rope",
    )(x, ctx, mod, wqk, wvt, *tables)


def _attn_kernel(q_ref, k_ref, vt_ref, lam_ref, g_ref, o_ref, kn_ref, *, chunks, lambda_init, tq):
    q = q_ref[0]
    lane = lax.broadcasted_iota(I32, q.shape, 1)
    zero = jnp.zeros_like(q)
    qq = jnp.concatenate([jnp.where(lane < DIFF_HEAD_DIM, q, zero),
                          jnp.where(lane >= DIFF_HEAD_DIM, q, zero)], axis=0)
    k_chunk = lambda c: k_ref[0, c * TK:(c + 1) * TK, :]

    def finish(acc, l):
        acc = acc / l
        lv = lam_ref[...]
        lam = (jnp.exp(jnp.sum(lv[0:1] * lv[1:2], axis=1, keepdims=True))
               - jnp.exp(jnp.sum(lv[2:3] * lv[3:4], axis=1, keepdims=True)) + lambda_init)
        ot = acc[:, :tq] - lam * acc[:, tq:]
        ms = jnp.mean(ot * ot, axis=0, keepdims=True)
        ot = ot * lax.rsqrt(ms + LN_EPS) * g_ref[...] * (1.0 - lambda_init)
        o_ref[0] = ot.T.astype(BF16)

    @pl.when(pl.program_id(2) == 0)
    def _():
        sel_r = lax.broadcasted_iota(I32, (LANES, LANES), 0)
        sel_c = lax.broadcasted_iota(I32, (LANES, LANES), 1)
        halves = jnp.where((sel_r < DIFF_HEAD_DIM) == (sel_c < DIFF_HEAD_DIM), 1.0, 0.0).astype(BF16)
        best = jnp.zeros((1, LANES), F32)
        for c in chunks:
            kf = k_chunk(c).astype(F32)
            norms = _dot((kf * kf).astype(BF16), halves)
            best = jnp.maximum(best, jnp.max(norms, axis=0, keepdims=True))
        kn_ref[...] = jnp.broadcast_to(best, kn_ref.shape)

    kn = kn_ref[0:1, :]
    qf = qq.astype(F32)
    qsq = jnp.sum(qf * qf, axis=1, keepdims=True)
    half = lax.broadcasted_iota(I32, (2 * tq, 1), 0) < tq
    bound = jnp.sqrt(qsq * jnp.where(half, kn[:, 0:1], kn[:, DIFF_HEAD_DIM:DIFF_HEAD_DIM + 1])) * BOUND_SLACK
    shift = SOFTMAX_HEADROOM - bound
    lane2 = lax.broadcasted_iota(I32, (2 * tq, LANES), 1)
    q_aug = jnp.concatenate([qq, jnp.where(lane2 == 0, shift, 0.0).astype(BF16)], axis=1)
    groups = [chunks[i:i + ATTN_GROUP] for i in range(0, len(chunks), ATTN_GROUP)]

    def scores(grp):
        kg = k_ref[0, grp[0] * TK:(grp[-1] + 1) * TK, :]
        one_lane = jnp.where(lax.broadcasted_iota(I32, kg.shape, 1) == 0, 1.0, 0.0).astype(BF16)
        return _dot_nt(jnp.concatenate([kg, one_lane], axis=1), q_aug)

    acc = jnp.zeros((LANES, 2 * tq), F32)
    l = jnp.zeros((1, 2 * tq), F32)
    st_next = scores(groups[0])
    for i, grp in enumerate(groups):
        st = st_next
        if i + 1 < len(groups):
            st_next = scores(groups[i + 1])
        p = jnp.exp2(st)
        l = l + jnp.sum(p, axis=0, keepdims=True)
        vt = jnp.concatenate([vt_ref[0, c] for c in grp], axis=1)
        acc = acc + _dot(vt, p.astype(BF16))
    finish(acc, l)
    underflowed = jnp.sum(jnp.where(l > SOFTMAX_MIN_SUM, 0.0, 1.0)) > 0.0

    @pl.when(underflowed)
    def _():
        m = jnp.full((1, 2 * tq), NEG_BIG, F32)
        l = jnp.zeros((1, 2 * tq), F32)
        acc = jnp.zeros((LANES, 2 * tq), F32)
        for c in chunks:
            st = _dot_nt(k_chunk(c), qq)
            m_new = jnp.maximum(m, jnp.max(st, axis=0, keepdims=True))
            alpha = jnp.exp2(m - m_new)
            p = jnp.exp2(st - m_new)
            l = l * alpha + jnp.sum(p, axis=0, keepdims=True)
            acc = acc * alpha + _dot(vt_ref[0, c], p.astype(BF16))
            m = m_new
        finish(acc, l)


def _attn_call(q, k, vt, lam_vecs, subln_g, lambda_init, *, tq, q_row0, n_rows, chunk0, n_chunks, name):
    b, t, d = q.shape
    q0 = q_row0 // tq
    kb = chunk0 // n_chunks
    assert q_row0 % tq == 0 and chunk0 % n_chunks == 0
    return pl.pallas_call(
        functools.partial(_attn_kernel, chunks=tuple(range(n_chunks)), lambda_init=lambda_init, tq=tq),
        out_shape=jax.ShapeDtypeStruct((b, n_rows, d), BF16),
        grid=(b, DIFF_HEADS, n_rows // tq),
        in_specs=[
            pl.BlockSpec((1, tq, LANES), lambda bi, h, qi: (bi, q0 + qi, h)),
            pl.BlockSpec((1, n_chunks * TK, LANES), lambda bi, h, qi: (bi, kb, h)),
            pl.BlockSpec((1, n_chunks, LANES, TM), lambda bi, h, qi: (bi, kb, h, 0)),
            pl.BlockSpec((4, DIFF_HEAD_DIM), lambda bi, h, qi: (0, 0)),
            pl.BlockSpec((LANES, 1), lambda bi, h, qi: (0, 0)),
        ],
        out_specs=pl.BlockSpec((1, tq, LANES), lambda bi, h, qi: (bi, qi, h)),
        scratch_shapes=[pltpu.VMEM((8, LANES), F32)],
        compiler_params=_cparams(("parallel", "parallel", "arbitrary")),
        name=name,
    )(q, k, vt, lam_vecs, subln_g.reshape(LANES, 1))


def _route(logits_t, rb, ri_ref, rw_ref, cnt_ref, gt_ref, first):
    tm = logits_t.shape[1]
    aff = 1.0 / (1.0 + jnp.exp(-logits_t))
    sel = aff + rb
    row = lambda a, e: a[e:e + 1, :]
    gscore = []
    for g in range(N_GROUPS):
        a, b, c, d = (row(sel, g * EXPERTS_PER_GROUP + j) for j in range(EXPERTS_PER_GROUP))
        hi1, lo1, hi2, lo2 = jnp.maximum(a, b), jnp.minimum(a, b), jnp.maximum(c, d), jnp.minimum(c, d)
        gscore.append(jnp.maximum(hi1, hi2) + jnp.maximum(jnp.minimum(hi1, hi2), jnp.maximum(lo1, lo2)))
    best, grp = gscore[0], jnp.zeros((1, tm), I32)
    for g in range(1, N_GROUPS):
        better = gscore[g] > best
        best = jnp.where(better, gscore[g], best)
        grp = jnp.where(better, g, grp)
    in_sel, in_aff = [], []
    for j in range(EXPERTS_PER_GROUP):
        s_j, a_j = row(sel, j), row(aff, j)
        for g in range(1, N_GROUPS):
            s_j = jnp.where(grp == g, row(sel, g * EXPERTS_PER_GROUP + j), s_j)
            a_j = jnp.where(grp == g, row(aff, g * EXPERTS_PER_GROUP + j), a_j)
        in_sel.append(s_j)
        in_aff.append(a_j)
    v0, i0, a0 = in_sel[0], jnp.zeros((1, tm), I32), in_aff[0]
    for j in range(1, EXPERTS_PER_GROUP):
        better = in_sel[j] > v0
        v0 = jnp.where(better, in_sel[j], v0)
        i0 = jnp.where(better, j, i0)
        a0 = jnp.where(better, in_aff[j], a0)
    v1, i1, a1 = jnp.full((1, tm), -jnp.inf, F32), jnp.zeros((1, tm), I32), jnp.zeros((1, tm), F32)
    for j in range(EXPERTS_PER_GROUP):
        better = (in_sel[j] > v1) & (i0 != j)
        v1 = jnp.where(better, in_sel[j], v1)
        i1 = jnp.where(better, j, i1)
        a1 = jnp.where(better, in_aff[j], a1)
    e0 = grp * EXPERTS_PER_GROUP + i0
    e1 = grp * EXPERTS_PER_GROUP + i1
    wsum = a0 + a1
    erow = lax.broadcasted_iota(I32, (N_EXPERTS, tm), 0)
    oh0 = (erow == e0).astype(F32)
    oh1 = (erow == e1).astype(F32)
    both = oh0 + oh1
    both_bf = both.astype(BF16)
    tri = (lax.broadcasted_iota(I32, (tm, tm), 0) < lax.broadcasted_iota(I32, (tm, tm), 1)).astype(BF16)
    before = _dot(both_bf, tri)
    round_up = lambda n: jnp.floor((n + (ROW_GROUP - 1.0)) * (1.0 / ROW_GROUP)) * ROW_GROUP
    run_col = round_up(jnp.sum(both, axis=1, keepdims=True))
    e_r = lax.broadcasted_iota(I32, (N_EXPERTS, N_EXPERTS), 0)
    e_c = lax.broadcasted_iota(I32, (N_EXPERTS, N_EXPERTS), 1)
    off_col = _dot((e_c < e_r).astype(BF16), jnp.broadcast_to(run_col, (N_EXPERTS, LANES)).astype(BF16))[:, 0:1]
    base = off_col + before
    s0 = jnp.sum(oh0 * base, axis=0, keepdims=True)
    s1 = jnp.sum(oh1 * base, axis=0, keepdims=True)
    both_pad = jnp.concatenate([both_bf, jnp.zeros((LANES - N_EXPERTS, tm), BF16)], axis=0)
    run_row = round_up(_dot_nt(jnp.ones((8, tm), BF16), both_pad))
    l_r = lax.broadcasted_iota(I32, (LANES, LANES), 0)
    l_c = lax.broadcasted_iota(I32, (LANES, LANES), 1)
    off_row = _dot(run_row.astype(BF16), (l_r < l_c).astype(BF16))

    @pl.when(first)
    def _():
        cnt_ref[...] = jnp.zeros_like(cnt_ref)

    g_row0 = lax.broadcasted_iota(I32, (GROUP_TAB, LANES), 0).astype(F32) * ROW_GROUP
    e_lane = lax.broadcasted_iota(I32, (GROUP_TAB, LANES), 1).astype(F32)
    off_b, run_b, used_b = off_row[0:1], run_row[0:1], cnt_ref[0:1, :]
    in_run = (g_row0 >= off_b) & (g_row0 < off_b + run_b)
    code = jnp.sum(jnp.where(in_run, (used_b + g_row0 - off_b) * N_EXPERTS + e_lane, 0.0), axis=1, keepdims=True)
    n_groups = jnp.sum(run_b, axis=1, keepdims=True) * (1.0 / ROW_GROUP)
    code = jnp.where(lax.broadcasted_iota(I32, (GROUP_TAB, 1), 0) == GROUP_TAB - 1, n_groups, code)
    gt_ref[0] = jnp.broadcast_to(code, (GROUP_TAB, LANES)).astype(I32)

    ri_ref[0, 0:1, :] = s0.astype(I32)
    ri_ref[0, 1:2, :] = s1.astype(I32)
    ri_ref[0, 2:3, :] = e0
    ri_ref[0, 3:4, :] = e1
    ri_ref[0, 4:8, :] = jnp.zeros((4, tm), I32)
    cnt_ref[...] = cnt_ref[...] + run_row
    rw_ref[0, 0:1, :] = a0 / wsum
    rw_ref[0, 1:2, :] = a1 / wsum
    rw_ref[0, 2:8, :] = jnp.zeros((6, tm), F32)


def _post_mixer(y, x, mod, lng, lnb, rwt, rb, x1_ref, ri_ref, rw_ref, cnt_ref, gt_ref, first, d):
    x1 = _layer_norm(ALPHA * x + mod[:, 2 * d:3 * d] * y, lng, lnb)
    x1_ref[0] = x1
    tok = x1 * (1.0 + mod[:, 4 * d:5 * d]) + mod[:, 3 * d:4 * d]
    t_hi = tok.astype(BF16)
    t_mid = (tok - t_hi.astype(F32)).astype(BF16)
    hi_terms = _dot(t_hi, rwt)
    logits = hi_terms[:, 0:LANES] + (hi_terms[:, LANES:] + _dot(t_mid, rwt[:, 0:LANES]))
    _route(logits.T[0:N_EXPERTS, :], rb, ri_ref, rw_ref, cnt_ref, gt_ref, first)


def _attn_out_kernel(ol_ref, oc_ref, xl_ref, xc_ref, mod_ref, wo_ref, lng_ref, lnb_ref, rwt_ref, rb_ref,
                     x1_ref, ri_ref, rw_ref, cnt_ref, gt_ref, *, d, n_lat):
    first = (pl.program_id(0) == 0) & (pl.program_id(1) == 0)
    latent = pl.program_id(1) < n_lat
    o = jnp.where(latent, ol_ref[0], oc_ref[0])
    x = jnp.where(latent, xl_ref[0], xc_ref[0])
    y = _dot(o, wo_ref[...])
    _post_mixer(y, x, mod_ref[0], lng_ref[...], lnb_ref[...], rwt_ref[...], rb_ref[...],
                x1_ref, ri_ref, rw_ref, cnt_ref, gt_ref, first, d)


def _route_out_shapes(b, n_tiles, t_out, d):
    return (jax.ShapeDtypeStruct((b, t_out, d), F32),
            jax.ShapeDtypeStruct((b * n_tiles, 8, TM), I32),
            jax.ShapeDtypeStruct((b * n_tiles, 8, TM), F32),
            jax.ShapeDtypeStruct((8, LANES), F32),
            jax.ShapeDtypeStruct((b * n_tiles, GROUP_TAB, LANES), I32))


def _route_out_specs(n_tiles, d):
    return (pl.BlockSpec((1, TM, d), lambda bi, j: (bi, j, 0)),
            pl.BlockSpec((1, 8, TM), lambda bi, j: (bi * n_tiles + j, 0, 0)),
            pl.BlockSpec((1, 8, TM), lambda bi, j: (bi * n_tiles + j, 0, 0)),
            pl.BlockSpec((8, LANES), lambda bi, j: (0, 0)),
            pl.BlockSpec((1, GROUP_TAB, LANES), lambda bi, j: (bi * n_tiles + j, 0, 0)))


def _attn_out_call(o_lat, o_ctx, x, ctx, mod, wo, lng, lnb, rwt, rb, ctx_row):
    b, s_len, d = x.shape
    n_lat = s_len // TM
    nt = n_lat + 1
    row = lambda a: pl.BlockSpec(a.shape, lambda bi, j: (0,) * a.ndim)
    lat = pl.BlockSpec((1, TM, d), lambda bi, j: (bi, jnp.minimum(j, n_lat - 1), 0))
    cxt = pl.BlockSpec((1, TM, d), lambda bi, j: (bi, 0, 0))
    return pl.pallas_call(
        functools.partial(_attn_out_kernel, d=d, n_lat=n_lat),
        out_shape=_route_out_shapes(b, nt, nt * TM, d),
        grid=(b, nt),
        in_specs=[
            lat, cxt, lat, cxt,
            pl.BlockSpec((1, 1, 6 * d), lambda bi, j: (jnp.where(j < n_lat, bi, ctx_row), 0, 0)),
            row(wo), row(lng), row(lnb), row(rwt), row(rb),
        ],
        out_specs=_route_out_specs(nt, d),
        compiler_params=_cparams(("arbitrary", "arbitrary")),
        name="attn_out_route",
    )(o_lat, o_ctx, x, ctx, mod, wo, lng, lnb, rwt, rb)


def _group_copy(src, r_src, dst, r_dst, sem):
    return pltpu.make_async_copy(src.at[pl.ds(pl.multiple_of(r_src, ROW_GROUP), ROW_GROUP)],
                                 dst.at[pl.ds(pl.multiple_of(r_dst, ROW_GROUP), ROW_GROUP)], sem)


def _n_groups(tab_ref, tile):
    return tab_ref[tile * GROUP_TAB + GROUP_TAB - 1]


def _run_copies(tab_ref, pad_ref, tile, stage, hbm, sem, to_hbm):
    def issue(g, carry):
        code = tab_ref[tile * GROUP_TAB + g]
        s = g * ROW_GROUP
        h = pad_ref[jnp.bitwise_and(code, N_EXPERTS - 1)] + jnp.right_shift(code, N_EXPERTS.bit_length() - 1)
        (_group_copy(stage, s, hbm, h, sem) if to_hbm else _group_copy(hbm, h, stage, s, sem)).start()
        return carry

    lax.fori_loop(0, _n_groups(tab_ref, tile), issue, 0)


def _run_waits(tab_ref, tile, stage, hbm, sem, to_hbm):
    def drain(g, carry):
        (_group_copy(stage, 0, hbm, 0, sem) if to_hbm else _group_copy(hbm, 0, stage, 0, sem)).wait()
        return carry

    lax.fori_loop(0, _n_groups(tab_ref, tile), drain, 0)


def _dispatch_kernel(tab_ref, pad_ref, fill_ref, x_ref, mod_ref, ri_ref, xs_ref, stage_ref, zero_ref, sems, sem_z,
                     *, d, n_steps):
    step = pl.program_id(0) * pl.num_programs(1) + pl.program_id(1)
    buf = step % 2
    mod = mod_ref[0]
    tok = (x_ref[0] * (1.0 + mod[:, 4 * d:5 * d]) + mod[:, 3 * d:4 * d]).astype(BF16)
    slot = lax.broadcasted_iota(I32, (STAGE_ROWS, TM), 0)
    ri = ri_ref[0]
    pick = jnp.where((slot == ri[0:1, :]) | (slot == ri[1:2, :]), 1.0, 0.0).astype(BF16)
    for bsel in range(2):
        @pl.when(buf == bsel)
        def _():
            stage_ref[bsel] = _dot(pick, tok).astype(BF16)
            _run_copies(tab_ref, pad_ref, step, stage_ref.at[bsel], xs_ref, sems.at[bsel], True)

    for bsel in range(2):
        @pl.when((buf != bsel) & (step > 0))
        def _():
            _run_waits(tab_ref, step - 1, stage_ref.at[bsel], xs_ref, sems.at[bsel], True)

    @pl.when(step == n_steps - 1)
    def _():
        for bsel in range(2):
            @pl.when(buf == bsel)
            def _():
                _run_waits(tab_ref, step, stage_ref.at[bsel], xs_ref, sems.at[bsel], True)

        zero_ref[...] = jnp.zeros_like(zero_ref)
        for e in range(N_EXPERTS + 1):
            lo, hi = fill_ref[2 * e], fill_ref[2 * e + 1]

            def fill(g, carry):
                _group_copy(zero_ref, 0, xs_ref, g * ROW_GROUP, sem_z).start()
                return carry

            def fill_wait(g, carry):
                _group_copy(zero_ref, 0, xs_ref, 0, sem_z).wait()
                return carry

            lax.fori_loop(lo, hi, fill, 0)
            lax.fori_loop(lo, hi, fill_wait, 0)


def _dispatch_call(tab, pad_start, fill, x1, mod, ri, n_tiles, n_pad, ctx_row, n_lat):
    b, t, d = x1.shape
    grid_spec = pltpu.PrefetchScalarGridSpec(
        num_scalar_prefetch=3,
        grid=(b, n_tiles),
        in_specs=[
            pl.BlockSpec((1, TM, d), lambda bi, j, *_: (bi, j, 0)),
            pl.BlockSpec((1, 1, 6 * d), lambda bi, j, *_: (jnp.where(j < n_lat, bi, ctx_row), 0, 0)),
            pl.BlockSpec((1, 8, TM), lambda bi, j, *_: (bi * n_tiles + j, 0, 0)),
        ],
        out_specs=pl.BlockSpec(memory_space=pl.ANY),
        scratch_shapes=[pltpu.VMEM((2, STAGE_ROWS, d), BF16), pltpu.VMEM((ROW_GROUP, d), BF16),
                        pltpu.SemaphoreType.DMA((2,)), pltpu.SemaphoreType.DMA],
    )
    return pl.pallas_call(
        functools.partial(_dispatch_kernel, d=d, n_steps=b * n_tiles),
        out_shape=jax.ShapeDtypeStruct((n_pad, d), BF16),
        grid_spec=grid_spec,
        compiler_params=_cparams(("arbitrary", "arbitrary")),
        name="moe_dispatch",
    )(tab, pad_start, fill, x1, mod, ri)


def _ffn_kernel(be_ref, nu_ref, xs_ref, wg_ref, wu_ref, wd_ref, ys_ref, wg_s, wu_s, wd_s):
    i = pl.program_id(0)

    @pl.when(i < nu_ref[0])
    def _():
        prev = be_ref[jnp.maximum(i - 1, 0)]

        @pl.when((i == 0) | (be_ref[i] != prev))
        def _():
            wg_s[...] = wg_ref[0, 0].astype(BF16)
            wu_s[...] = wu_ref[0, 0].astype(BF16)
            wd_s[...] = wd_ref[0, 0].astype(BF16)

        rows = [slice(r, r + MOE_BLK // FFN_SLICES) for r in range(0, MOE_BLK, MOE_BLK // FFN_SLICES)]
        gate_up = [(_dot(xs_ref[r, :], wg_s[...]), _dot(xs_ref[r, :], wu_s[...])) for r in rows]
        for r, (g, u) in zip(rows, gate_up):
            ys_ref[r, :] = _dot((_silu(g) * u).astype(BF16), wd_s[...]).astype(BF16)


def _ffn_call(block_expert, n_used, xs, w_gate, w_up, w_down, layer):
    n_pad, d = xs.shape
    de = w_gate.shape[-1]
    n_blocks = n_pad // MOE_BLK
    blk = lambda i, be, nu: (jnp.minimum(i, nu[0] - 1), 0)
    exp = lambda i, be, nu: (layer, be[jnp.minimum(i, nu[0] - 1)], 0, 0)
    grid_spec = pltpu.PrefetchScalarGridSpec(
        num_scalar_prefetch=2,
        grid=(n_blocks,),
        in_specs=[
            pl.BlockSpec((MOE_BLK, d), blk),
            pl.BlockSpec((1, 1, d, de), exp),
            pl.BlockSpec((1, 1, d, de), exp),
            pl.BlockSpec((1, 1, de, d), exp),
        ],
        out_specs=pl.BlockSpec((MOE_BLK, d), blk),
        scratch_shapes=[pltpu.VMEM((d, de), BF16), pltpu.VMEM((d, de), BF16), pltpu.VMEM((de, d), BF16)],
    )
    return pl.pallas_call(
        _ffn_kernel,
        out_shape=jax.ShapeDtypeStruct((n_pad, d), BF16),
        grid_spec=grid_spec,
        input_output_aliases={2: 0},
        compiler_params=_cparams(("arbitrary",)),
        name="moe_ffn",
    )(block_expert, n_used, xs, w_gate, w_up, w_down)


def _combine_kernel(tab_ref, pad_ref, x_ref, mod_ref, rit_ref, rw_ref, lng_ref, lnb_ref, ys_ref, *rest,
                    d, n_steps, with_proj):
    if with_proj:
        proj_in, (o_ref, *proj_out), (stage_ref, sems) = rest[:5], rest[5:11], rest[11:]
    else:
        o_ref, stage_ref, sems = rest
    step = pl.program_id(0) * pl.num_programs(1) + pl.program_id(1)
    buf = step % 2

    @pl.when(step == 0)
    def _():
        stage_ref[...] = jnp.zeros_like(stage_ref)
        _run_copies(tab_ref, pad_ref, step, stage_ref.at[0], ys_ref, sems.at[0], False)

    for bsel in range(2):
        @pl.when((buf != bsel) & (step + 1 < n_steps))
        def _():
            _run_copies(tab_ref, pad_ref, step + 1, stage_ref.at[bsel], ys_ref, sems.at[bsel], False)

    rit = rit_ref[0]
    lane = lax.broadcasted_iota(I32, (TM, STAGE_ROWS), 1)
    pick0 = jnp.where(lane == rit[:, 0:1], 1.0, 0.0).astype(BF16)
    pick1 = jnp.where(lane == rit[:, 1:2], 1.0, 0.0).astype(BF16)
    w = rw_ref[0]
    mod = mod_ref[0]
    for bsel in range(2):
        @pl.when(buf == bsel)
        def _():
            _run_waits(tab_ref, step, stage_ref.at[bsel], ys_ref, sems.at[bsel], False)
            stage = stage_ref[bsel]
            moe = w[:, 0:1] * _dot(pick0, stage) + w[:, 1:2] * _dot(pick1, stage)
            o_ref[0] = _layer_norm(ALPHA * x_ref[0] + mod[:, 5 * d:6 * d] * moe, lng_ref[...], lnb_ref[...])

    if with_proj:
        mod_n, *weights = proj_in
        _gla_proj(o_ref[0], mod_n[0], *weights, *proj_out, d=d)


def _combine_call(tab, pad_start, x1, mod, ri_t, rw_t, lng, lnb, ys, n_tiles, ctx_row, n_lat, proj=None):
    b, _, d = x1.shape
    t = n_tiles * TM
    row = lambda a: pl.BlockSpec(a.shape, lambda bi, j, *_: (0,) * a.ndim)
    tile = lambda n: pl.BlockSpec((1, TM, n), lambda bi, j, *_: (bi, j, 0))
    mod_spec = pl.BlockSpec((1, 1, 6 * d), lambda bi, j, *_: (jnp.where(j < n_lat, bi, ctx_row), 0, 0))
    in_specs = [
        tile(d), mod_spec,
        pl.BlockSpec((1, TM, 8), lambda bi, j, *_: (bi * n_tiles + j, 0, 0)),
        pl.BlockSpec((1, TM, 8), lambda bi, j, *_: (bi * n_tiles + j, 0, 0)),
        row(lng), row(lnb),
        pl.BlockSpec(memory_space=pl.ANY),
    ]
    out_specs, out_shape, args = tile(d), jax.ShapeDtypeStruct((b, t, d), F32), ()
    if proj is not None:
        dk = GLA_HEADS * GLA_DK
        in_specs += [mod_spec] + [row(a) for a in proj[1:]]
        out_specs = (out_specs, tile(dk), tile(dk), tile(d), tile(d), tile(2 * dk))
        out_shape = (out_shape, *_gla_proj_shapes(b, t, d))
        args = proj
    grid_spec = pltpu.PrefetchScalarGridSpec(
        num_scalar_prefetch=2,
        grid=(b, n_tiles),
        in_specs=in_specs,
        out_specs=out_specs,
        scratch_shapes=[pltpu.VMEM((2, STAGE_ROWS, d), BF16), pltpu.SemaphoreType.DMA((2,))],
    )
    return pl.pallas_call(
        functools.partial(_combine_kernel, d=d, n_steps=b * n_tiles, with_proj=proj is not None),
        out_shape=out_shape,
        grid_spec=grid_spec,
        compiler_params=_cparams(("arbitrary", "arbitrary")),
        name="moe_combine_proj" if proj is not None else "moe_combine",
    )(tab, pad_start, x1, mod, ri_t, rw_t, lng, lnb, ys, *args)


def _moe(x1, mod, ri, rw, counts, gt, lng, lnb, w_gate, w_up, w_down, layer, n_tiles, ctx_row, n_lat, proj=None):
    b, _, d = x1.shape
    n_steps = b * n_tiles
    n_rows_max = 2 * n_steps * TM + (ROW_GROUP - 1) * N_EXPERTS * n_steps
    n_pad = -(-n_rows_max // MOE_BLK) * MOE_BLK + N_EXPERTS * MOE_BLK
    n_blocks = n_pad // MOE_BLK
    cnt = counts[0, :N_EXPERTS].astype(I32)
    padded = (cnt + MOE_BLK - 1) // MOE_BLK * MOE_BLK
    pad_end = jnp.cumsum(padded)
    pad_start = jnp.concatenate([jnp.zeros((1,), I32), pad_end]).astype(I32)
    block_row0 = jnp.arange(n_blocks, dtype=I32) * MOE_BLK
    block_expert = jnp.minimum(
        jnp.sum((pad_end[None, :] <= block_row0[:, None]).astype(I32), axis=1), N_EXPERTS - 1).astype(I32)
    n_used = (pad_end[-1:] // MOE_BLK).astype(I32)
    fill_lo = jnp.concatenate([pad_start[:-1] + cnt, pad_end[-1:]]) // ROW_GROUP
    fill_hi = jnp.concatenate([pad_end, jnp.full((1,), n_pad, I32)]) // ROW_GROUP
    fill = jnp.stack([fill_lo, fill_hi], axis=1).reshape(-1).astype(I32)
    tab = gt[:, :, 0].reshape(-1)
    xs = _dispatch_call(tab, pad_start, fill, x1, mod, ri, n_tiles, n_pad, ctx_row, n_lat)
    ys = _ffn_call(block_expert, n_used, xs, w_gate, w_up, w_down, layer)
    ri_t, rw_t = jnp.swapaxes(ri, 1, 2), jnp.swapaxes(rw, 1, 2)
    return _combine_call(tab, pad_start, x1, mod, ri_t, rw_t, lng, lnb, ys, n_tiles, ctx_row, n_lat, proj)


def _gla_proj(x, mod, win_ref, wgd_ref, wgu_ref, bg_ref, q_ref, k_ref, v_ref, r_ref, g_ref, *, d):
    h = (x * (1.0 + mod[:, d:2 * d]) + mod[:, 0:d]).astype(BF16)
    dk = GLA_HEADS * GLA_DK
    p = _dot(h, win_ref[...])
    q_ref[0] = (p[:, 0:dk] * GLA_DK ** -0.5).astype(BF16)
    k_ref[0] = p[:, dk:2 * dk].astype(BF16)
    v_ref[0] = p[:, 2 * dk:2 * dk + d].astype(BF16)
    r_ref[0] = p[:, 2 * dk + d:].astype(BF16)
    low = _dot(h, wgd_ref[...]).astype(BF16)
    pre = _dot(low, wgu_ref[...]) + bg_ref[...]
    g_ref[0] = (jnp.minimum(pre, 0.0) - jnp.log1p(jnp.exp(-jnp.abs(pre)))) * (1.0 / GLA_TAU)


def _gla_proj_shapes(b, t, d):
    dk = GLA_HEADS * GLA_DK
    return (jax.ShapeDtypeStruct((b, t, dk), BF16), jax.ShapeDtypeStruct((b, t, dk), BF16),
            jax.ShapeDtypeStruct((b, t, d), BF16), jax.ShapeDtypeStruct((b, t, d), BF16),
            jax.ShapeDtypeStruct((b, t, 2 * dk), F32))


def _split3(x):
    hi = x.astype(BF16)
    r1 = x - hi.astype(F32)
    mid = r1.astype(BF16)
    lo = (r1 - mid.astype(F32)).astype(BF16)
    return hi, mid, lo


class _GlaMasks:
    def __init__(self, reverse):
        L = GLA_CHUNK
        t = lax.broadcasted_iota(I32, (L, L), 0)
        s = lax.broadcasted_iota(I32, (L, L), 1)
        seen = (s >= t) if reverse else (s <= t)
        self.tri = seen.astype(BF16)
        blk = lambda r: jnp.right_shift(r, GLA_SUB.bit_length() - 1)
        dist = (blk(s) - blk(t)) if reverse else (blk(t) - blk(s))
        self.pairs = [seen & (dist == 0)] + [dist == n for n in range(1, L // GLA_SUB)]


def _gla_chunk_matmuls(q_ref, k_ref, v_ref, b_all, rows, h, reverse):
    L, SB, DK = GLA_CHUNK, GLA_SUB, GLA_DK
    nb = L // SB
    kcols = slice(h * DK, (h + 1) * DK)
    vcols = slice(h * GLA_DV, (h + 1) * GLA_DV)
    bh = b_all[:, kcols]
    zero = jnp.zeros((1, DK), F32)
    earlier = lambda i, n: i + n if reverse else i - n

    def start_ref(i):
        if not 0 <= i < nb or i == (nb - 1 if reverse else 0):
            return zero
        r = (i + 1) * SB if reverse else i * SB - 1
        return bh[r:r + 1, :]

    def end_ref(i):
        r = i * SB if reverse else i * SB + SB - 1
        return bh[r:r + 1, :]

    per_block = lambda f: jnp.concatenate([jnp.broadcast_to(f(i), (SB, DK)) for i in range(nb)], axis=0)
    bs, be = per_block(start_ref), per_block(end_ref)
    b_last = end_ref(0 if reverse else nb - 1)
    q = q_ref[0, rows, kcols].astype(F32)
    k = k_ref[0, rows, kcols].astype(F32)
    v = v_ref[0, rows, vcols]
    scaled = lambda x, e: (x * jnp.exp(e)).astype(BF16)
    q_own = scaled(q, bh - bs)
    k_own, k_end = scaled(k, bs - bh), scaled(k, be - bh)
    pieces = [_dot_nt(q_own, k_own), _dot_nt(q_own, k_end)]
    for n in range(2, nb):
        pieces.append(_dot_nt(scaled(q, bh - per_block(lambda i: start_ref(earlier(i, n - 1)))), k_end))
    update = _dot_tn(v, scaled(k, b_last - bh))
    return pieces, scaled(q, bh), update, b_last, v, vcols


def _gla_scan_kernel(qf, kf, vf, gf, qb, kb, vb, gb, of_ref, ob_ref, sf_ref, sb_ref):
    @pl.when(pl.program_id(1) == 0)
    def _():
        sf_ref[...] = jnp.zeros_like(sf_ref)
        sb_ref[...] = jnp.zeros_like(sb_ref)

    L = GLA_CHUNK
    n_chunks = TM // L
    dirs = ((qf, kf, vf, gf, of_ref, sf_ref, False, _GlaMasks(False)),
            (qb, kb, vb, gb, ob_ref, sb_ref, True, _GlaMasks(True)))
    gate_sums = [[sum(_dot(masks.tri, part) for part in _split3(g_ref[0, c * L:(c + 1) * L, :]))
                  for c in range(n_chunks)] for _, _, _, g_ref, _, _, _, masks in dirs]
    work = []
    for step in range(n_chunks):
        for d_i, (q_ref, k_ref, v_ref, g_ref, o_ref, state_ref, reverse, masks) in enumerate(dirs):
            c = n_chunks - 1 - step if reverse else step
            rows = slice(c * L, (c + 1) * L)
            for h in range(GLA_HEADS):
                work.append((o_ref, state_ref, rows, h, masks,
                             _gla_chunk_matmuls(q_ref, k_ref, v_ref, gate_sums[d_i][c], rows, h, reverse)))
    for o_ref, state_ref, rows, h, masks, (pieces, q_int, update, b_last, v, vcols) in work:
        a = sum(jnp.where(m, p, 0.0) for m, p in zip(masks.pairs, pieces)).astype(BF16)
        state_t = state_ref[h]
        o_ref[0, rows, vcols] = _dot(a, v) + _dot_nt(q_int, state_t.astype(BF16))
        state_ref[h] = jnp.exp(b_last) * state_t + update


def _gla_scan_call(q, k, v, g):
    b, t, d = v.shape
    nt = t // TM
    n_lat = nt - 1
    dk = GLA_HEADS * GLA_DK
    fwd = lambda j: jnp.where(j == 0, n_lat, j - 1)
    bwd = lambda j: jnp.where(j == 0, n_lat, n_lat - j)
    spec = lambda n, idx, col: pl.BlockSpec((1, TM, n), lambda bi, j: (bi, idx(j), col))
    return pl.pallas_call(
        _gla_scan_kernel,
        out_shape=(jax.ShapeDtypeStruct((b, n_lat * TM, d), F32), jax.ShapeDtypeStruct((b, n_lat * TM, d), F32)),
        grid=(b, nt),
        in_specs=[spec(dk, fwd, 0), spec(dk, fwd, 0), spec(d, fwd, 0), spec(dk, fwd, 0),
                  spec(dk, bwd, 0), spec(dk, bwd, 0), spec(d, bwd, 0), spec(dk, bwd, 1)],
        out_specs=(pl.BlockSpec((1, TM, d), lambda bi, j: (bi, jnp.maximum(j - 1, 0), 0)),
                   pl.BlockSpec((1, TM, d), lambda bi, j: (bi, n_lat - jnp.maximum(j, 1), 0))),
        scratch_shapes=[pltpu.VMEM((GLA_HEADS, GLA_DV, GLA_DK), F32), pltpu.VMEM((GLA_HEADS, GLA_DV, GLA_DK), F32)],
        compiler_params=_cparams(("parallel", "arbitrary")),
        name="gla_scan",
    )(q, k, v, g, q, k, v, g)


def _gla_out_kernel(of_ref, ob_ref, r_ref, x_ref, mod_ref, ng_ref, wo_ref, lng_ref, lnb_ref, rwt_ref, rb_ref,
                    x1_ref, ri_ref, rw_ref, cnt_ref, gt_ref, *, d):
    first = (pl.program_id(0) == 0) & (pl.program_id(1) == 0)
    o = of_ref[0] + ob_ref[0]
    r = r_ref[0].astype(F32)
    gate = _silu(r)
    ng = ng_ref[...]
    parts = []
    for h in range(GLA_HEADS):
        cols = slice(h * GLA_DV, (h + 1) * GLA_DV)
        oh = o[:, cols]
        ms = jnp.mean(oh * oh, axis=-1, keepdims=True)
        parts.append((oh * lax.rsqrt(ms + LN_EPS) * ng * gate[:, cols]).astype(BF16))
    y = _dot(jnp.concatenate(parts, axis=-1), wo_ref[...])
    _post_mixer(y, x_ref[0], mod_ref[0], lng_ref[...], lnb_ref[...], rwt_ref[...], rb_ref[...],
                x1_ref, ri_ref, rw_ref, cnt_ref, gt_ref, first, d)


def _gla_out_call(o_f, o_b, r, xs, mod, ng, wo, lng, lnb, rwt, rb):
    b, s_len, d = o_f.shape
    n_lat = s_len // TM
    row = lambda a: pl.BlockSpec(a.shape, lambda bi, j: (0,) * a.ndim)
    tile = pl.BlockSpec((1, TM, d), lambda bi, j: (bi, j, 0))
    return pl.pallas_call(
        functools.partial(_gla_out_kernel, d=d),
        out_shape=_route_out_shapes(b, n_lat, s_len, d),
        grid=(b, n_lat),
        in_specs=[tile, tile, tile, tile, pl.BlockSpec((1, 1, 6 * d), lambda bi, j: (bi, 0, 0)),
                  row(ng), row(wo), row(lng), row(lnb), row(rwt), row(rb)],
        out_specs=_route_out_specs(n_lat, d),
        compiler_params=_cparams(("arbitrary", "arbitrary")),
        name="gla_out_route",
    )(o_f, o_b, r, xs, mod, ng, wo, lng, lnb, rwt, rb)


def kernel(x, c, ctx, c_ctx, ada_w, ada_b, ln_g, ln_b, attn_w_qkv, attn_w_o, attn_lambda, attn_subln_g, gla_w_in, gla_wg_down, gla_wg_up, gla_bg, gla_norm_g, gla_w_o, router_w, router_b, moe_w_gate, moe_w_up, moe_w_down):
    b, s_len, d = x.shape
    c_len = ctx.shape[1]
    assert ada_w.shape[0] == DEPTH and c_len == TM and s_len % TQ == 0 and TQ % TM == 0 and s_len % GRID_W == 0
    assert d == DIFF_HEADS * 2 * DIFF_HEAD_DIM == GLA_HEADS * GLA_DV
    assert STAGE_ROWS % LANES == 0 and STAGE_ROWS >= 2 * TM + (ROW_GROUP - 1) * N_EXPERTS
    n_lat = s_len // TM
    nt = n_lat + 1
    ctx_row = b

    cond_rows = -(-(b + 1) // 8) * 8
    cond = jnp.concatenate([c, c_ctx[None, :], jnp.zeros((cond_rows - b - 1, d), F32)], axis=0)
    mods = _ada_call(cond, ada_w, ada_b)
    rw_pad = jnp.concatenate([router_w, jnp.zeros((d, LANES - N_EXPERTS), F32)], axis=1)
    rw_hi = rw_pad.astype(BF16)
    rwt = jnp.concatenate([rw_hi, (rw_pad - rw_hi.astype(F32)).astype(BF16)], axis=1)
    rb = router_b.reshape(N_EXPERTS, 1)

    mod0 = mods[0].reshape(cond_rows, 1, 6 * d)
    wqkv = attn_w_qkv[0]
    wqk = wqkv[:, :2 * d].astype(BF16)
    wvt = wqkv[:, 2 * d:].T.astype(BF16)
    tables = tuple(jnp.asarray(t) for t in _rope_tables(s_len, c_len))
    q, k, vt = _qkv_call(x, ctx, mod0, wqk, wvt, tables, ctx_row)
    lambda_init = 0.8 - 0.6 * math.exp(-0.3 * 0)
    attn = functools.partial(_attn_call, q, k, vt, attn_lambda[0], attn_subln_g[0], lambda_init)
    o_lat = attn(tq=TQ, q_row0=0, n_rows=s_len, chunk0=0, n_chunks=nt, name="diff_attn_lat")
    o_ctx = attn(tq=c_len, q_row0=s_len, n_rows=c_len, chunk0=n_lat, n_chunks=1, name="diff_attn_ctx")
    x1, ri, rw, counts, gt = _attn_out_call(o_lat, o_ctx, x, ctx, mod0, attn_w_o[0].astype(BF16),
                                        ln_g[0, 0][None], ln_b[0, 0][None], rwt, rb, ctx_row)

    mod1 = mods[1].reshape(cond_rows, 1, 6 * d)
    dk = GLA_HEADS * GLA_DK
    wgd = jnp.concatenate([gla_wg_down[0, 0], gla_wg_down[0, 1],
                           jnp.zeros((d, LANES - 2 * GLA_GATE_RANK), F32)], axis=1).astype(BF16)
    wgu = jnp.zeros((LANES, 2 * dk), F32)
    wgu = wgu.at[0:GLA_GATE_RANK, 0:dk].set(gla_wg_up[0, 0])
    wgu = wgu.at[GLA_GATE_RANK:2 * GLA_GATE_RANK, dk:].set(gla_wg_up[0, 1]).astype(BF16)
    bg = gla_bg[0].reshape(1, 2 * dk)
    stream, gq, gk, gv, gr, gg = _moe(x1, mod0, ri, rw, counts, gt, ln_g[0, 1][None], ln_b[0, 1][None],
                                      moe_w_gate, moe_w_up, moe_w_down, 0, nt, ctx_row, n_lat,
                                      proj=(mod1, gla_w_in[0].astype(BF16), wgd, wgu, bg))
    o_f, o_b = _gla_scan_call(gq, gk, gv, gg)
    x1, ri, rw, counts, gt = _gla_out_call(o_f, o_b, gr, stream, mod1, gla_norm_g[0][None], gla_w_o[0].astype(BF16),
                                       ln_g[1, 0][None], ln_b[1, 0][None], rwt, rb)
    return _moe(x1, mod1, ri, rw, counts, gt, ln_g[1, 1][None], ln_b[1, 1][None],
                moe_w_gate, moe_w_up, moe_w_down, 1, n_lat, ctx_row, n_lat)
```
